```python
import jax, jax.numpy as jnp
from jax import lax
import numpy as np

D_MODEL = 1024
BATCH = 8
SEQ = 2048
DEPTH = 1
DEC_BATCH = 8
DEC_SEQ = 64
PAST_LEN = 4096

CHUNK = 64
D_MIX = D_MODEL
POOL_WIDTH = D_MIX // 2
POOL_WINDOWS = (2, 4, 8, 16)
N_POOL_GROUPS = len(POOL_WINDOWS)
POOL_GROUP_W = POOL_WIDTH // N_POOL_GROUPS
POOL_HIST = max(POOL_WINDOWS) - 1
HEAD_DIM = 64
N_HEADS = (D_MIX - POOL_WIDTH) // HEAD_DIM
N_KV_HEADS = 2
GQ = N_HEADS // N_KV_HEADS
ATTN_WIDTH = N_HEADS * HEAD_DIM
KV_WIDTH = N_KV_HEADS * HEAD_DIM
IN_WIDTH = POOL_WIDTH + ATTN_WIDTH + 2 * KV_WIDTH
WINDOW = 128
PREV_CHUNKS = WINDOW // CHUNK
N_EXPERTS = 64
TOP_K = 6
N_EXPERT_GROUPS = 8
TOPK_GROUPS = 4
D_EXPERT = 256
D_SHARED = 256
ROUTED_SCALE = 2.5
EXPERT_BLOCK = 128
NORM_EPS = 1e-6

kernel_name = 'streaming_pool_swa_sink_moe'


def rmsnorm(x, g):
    xf = x.astype(jnp.float32)
    y = xf * lax.rsqrt(jnp.mean(xf * xf, axis=-1, keepdims=True) + NORM_EPS) * g.astype(jnp.float32)
    return y.astype(x.dtype)


def adaln(c, w_ada, b_ada):
    m = (jax.nn.silu(c) @ w_ada + b_ada).reshape(c.shape[0], 6, 1, c.shape[-1])
    return m[:, 0], m[:, 1], m[:, 2], m[:, 3], m[:, 4], m[:, 5]


def pool_mixer(u_ext, n_new, w_pool, pool_scale):
    B, L, C = u_ext.shape
    uf = u_ext.astype(jnp.float32).reshape(B, L, N_POOL_GROUPS, POOL_GROUP_W)
    cs = jnp.concatenate([jnp.zeros((B, 1, N_POOL_GROUPS, POOL_GROUP_W), jnp.float32),
                          jnp.cumsum(uf, axis=1)], axis=1)
    j = np.arange(L - n_new, L)
    means = []
    for g, w in enumerate(POOL_WINDOWS):
        lo = np.maximum(j + 1 - w, 0)
        cnt = np.minimum(j + 1, w).astype(np.float32)
        means.append((cs[:, j + 1, g] - cs[:, lo, g]) / cnt[None, :, None])
    d = jnp.stack(means, axis=2) - uf[:, L - n_new:]
    y = jnp.einsum('btgc,gcd->btgd', d, w_pool.astype(jnp.float32)).reshape(B, n_new, C)
    return (y * pool_scale.astype(jnp.float32)).astype(u_ext.dtype)


def sink_probs(scores, sink):
    m = jnp.maximum(jnp.max(scores, axis=-1, keepdims=True), sink)
    p = jnp.exp(scores - m)
    return p / (jnp.sum(p, axis=-1, keepdims=True) + jnp.exp(sink - m))


def banded_window_attention(q, k, v, sinks):
    B, S = q.shape[0], q.shape[1]
    NC = S // CHUNK
    NK = (PREV_CHUNKS + 1) * CHUNK
    qb = q.reshape(B, NC, CHUNK, N_KV_HEADS, GQ, HEAD_DIM)
    pad = ((0, 0), (PREV_CHUNKS * CHUNK, 0), (0, 0), (0, 0))
    kp = jnp.pad(k, pad).reshape(B, NC + PREV_CHUNKS, CHUNK, N_KV_HEADS, HEAD_DIM)
    vp = jnp.pad(v, pad).reshape(B, NC + PREV_CHUNKS, CHUNK, N_KV_HEADS, HEAD_DIM)
    kb = jnp.concatenate([kp[:, i:i + NC] for i in range(PREV_CHUNKS + 1)], axis=2)
    vb = jnp.concatenate([vp[:, i:i + NC] for i in range(PREV_CHUNKS + 1)], axis=2)
    scores = jnp.einsum('bnqkgd,bnskd->bnkgqs', qb, kb).astype(jnp.float32) * (HEAD_DIM ** -0.5)
    key_pos = np.arange(NC)[:, None] * CHUNK + np.arange(NK)[None, :] - PREV_CHUNKS * CHUNK
    valid = (key_pos >= 0)[None, :, None, None, None, :]
    scores = jnp.where(valid, scores, -jnp.inf)
    sink = sinks.astype(jnp.float32).reshape(N_KV_HEADS, GQ)[None, None, :, :, None, None]
    p = sink_probs(scores, sink).astype(v.dtype)
    o = jnp.einsum('bnkgqs,bnskd->bnqkgd', p, vb)
    return o.reshape(B, S, ATTN_WIDTH)


def cached_window_attention(q, k_ext, v_ext, sinks):
    B, T = q.shape[0], q.shape[1]
    qs = q.reshape(B, T, N_KV_HEADS, GQ, HEAD_DIM)
    scores = jnp.einsum('btkgd,bskd->bkgts', qs, k_ext).astype(jnp.float32) * (HEAD_DIM ** -0.5)
    sink = sinks.astype(jnp.float32).reshape(N_KV_HEADS, GQ)[None, :, :, None, None]
    p = sink_probs(scores, sink).astype(v_ext.dtype)
    o = jnp.einsum('bkgts,bskd->btkgd', p, v_ext)
    return o.reshape(B, T, ATTN_WIDTH)


def swiglu(h, wg, wu, wd):
    return (jax.nn.silu(h @ wg) * (h @ wu)) @ wd


def routed_moe(h, w_router, router_bias, w_gate, w_up, w_down):
    N, D = h.shape
    scores = jax.nn.sigmoid((h @ w_router).astype(jnp.float32))
    sel = scores + router_bias.astype(jnp.float32)
    grp = sel.reshape(N, N_EXPERT_GROUPS, N_EXPERTS // N_EXPERT_GROUPS)
    group_score = jnp.sum(lax.top_k(grp, 2)[0], axis=-1)
    _, gidx = lax.top_k(group_score, TOPK_GROUPS)
    gmask = jnp.sum(jax.nn.one_hot(gidx, N_EXPERT_GROUPS, dtype=jnp.float32), axis=-2) > 0
    emask = jnp.repeat(gmask, N_EXPERTS // N_EXPERT_GROUPS, axis=-1)
    _, eidx = lax.top_k(jnp.where(emask, sel, -jnp.inf), TOP_K)
    wts = jnp.take_along_axis(scores, eidx, axis=-1)
    wts = wts / jnp.sum(wts, axis=-1, keepdims=True) * ROUTED_SCALE
    A = N * TOP_K
    flat_e = eidx.reshape(A)
    flat_t = jnp.repeat(jnp.arange(N, dtype=jnp.int32), TOP_K)
    flat_w = wts.reshape(A)
    order = jnp.argsort(flat_e)
    se = flat_e[order]
    counts = jnp.bincount(flat_e, length=N_EXPERTS)
    starts = jnp.cumsum(counts) - counts
    padded = (counts + EXPERT_BLOCK - 1) // EXPERT_BLOCK * EXPERT_BLOCK
    pstarts = jnp.cumsum(padded) - padded
    pends = pstarts + padded
    dest = pstarts[se] + jnp.arange(A) - starts[se]
    n_blocks = -(-(A + N_EXPERTS * (EXPERT_BLOCK - 1)) // EXPERT_BLOCK)
    P = n_blocks * EXPERT_BLOCK
    tok_buf = jnp.full((P,), N, jnp.int32).at[dest].set(flat_t[order])
    w_buf = jnp.zeros((P,), jnp.float32).at[dest].set(flat_w[order])
    block_start = jnp.arange(n_blocks) * EXPERT_BLOCK
    block_e = jnp.minimum(jnp.sum(block_start[:, None] >= pends[None, :], axis=-1), N_EXPERTS - 1)
    x_pad = jnp.concatenate([h, jnp.zeros((1, D), h.dtype)], axis=0)
    xb = x_pad[tok_buf].reshape(n_blocks, EXPERT_BLOCK, D)

    def expert_block(args):
        xblk, e = args
        return swiglu(xblk, w_gate[e], w_up[e], w_down[e])

    yb = lax.map(expert_block, (xb, block_e))
    y = jnp.zeros((N + 1, D), jnp.float32).at[tok_buf].add(
        yb.reshape(P, D).astype(jnp.float32) * w_buf[:, None])
    return y[:N].astype(h.dtype)


def trunk_layer(x, c, lp, pool_hist, k_hist, v_hist):
    (w_ada, b_ada, norm_gains, w_in, w_pool, pool_scale, attn_sinks, w_out,
     w_router, router_bias, w_gate, w_up, w_down, ws_gate, ws_up, ws_down) = lp
    B, T, D = x.shape
    sh_m, sc_m, gt_m, sh_f, sc_f, gt_f = adaln(c, w_ada, b_ada)
    h = rmsnorm(x, norm_gains[0]) * (1 + sc_m) + sh_m
    proj = h @ w_in
    u = proj[..., :POOL_WIDTH]
    q = proj[..., POOL_WIDTH:POOL_WIDTH + ATTN_WIDTH].reshape(B, T, N_HEADS, HEAD_DIM)
    o_k = POOL_WIDTH + ATTN_WIDTH
    k = proj[..., o_k:o_k + KV_WIDTH].reshape(B, T, N_KV_HEADS, HEAD_DIM)
    v = proj[..., o_k + KV_WIDTH:].reshape(B, T, N_KV_HEADS, HEAD_DIM)
    if pool_hist is None:
        u_ext, k_ext, v_ext = u, k, v
        attn = banded_window_attention(q, k, v, attn_sinks)
    else:
        u_ext = jnp.concatenate([pool_hist.astype(u.dtype), u], axis=1)
        k_ext = jnp.concatenate([k_hist.astype(k.dtype), k], axis=1)
        v_ext = jnp.concatenate([v_hist.astype(v.dtype), v], axis=1)
        attn = cached_window_attention(q, k_ext, v_ext, attn_sinks)
    pool = pool_mixer(u_ext, T, w_pool, pool_scale)
    mix = jnp.concatenate([pool, attn], axis=-1) @ w_out
    x = x + gt_m * rmsnorm(mix, norm_gains[1])
    h = (rmsnorm(x, norm_gains[2]) * (1 + sc_f) + sh_f).reshape(B * T, D)
    ffn = routed_moe(h, w_router, router_bias, w_gate, w_up, w_down) + swiglu(h, ws_gate, ws_up, ws_down)
    x = x + gt_f * rmsnorm(ffn.reshape(B, T, D), norm_gains[3])
    return x, u_ext[:, -POOL_HIST:], k_ext[:, -WINDOW:], v_ext[:, -WINDOW:]


def setup_inputs(seed: int = 0) -> dict:
    key = jax.random.key(seed)
    ks = jax.random.split(key, 24)

    def nrm(k, shape, scale):
        return jax.random.normal(k, shape, jnp.float32) * scale

    kv_rows = min(WINDOW, PAST_LEN)
    return {
        'x_prompt': nrm(ks[0], (BATCH, SEQ, D_MODEL), 1.0),
        'x_sample': nrm(ks[1], (DEC_BATCH, DEC_SEQ, D_MODEL), 1.0),
        'c_prompt': nrm(ks[2], (BATCH, D_MODEL), 1.0),
        'c_sample': nrm(ks[3], (DEC_BATCH, D_MODEL), 1.0),
        'state_pool': nrm(ks[4], (DEPTH, DEC_BATCH, POOL_HIST, POOL_WIDTH), 1.0),
        'cache_k': nrm(ks[5], (DEPTH, DEC_BATCH, kv_rows, N_KV_HEADS, HEAD_DIM), 1.0),
        'cache_v': nrm(ks[6], (DEPTH, DEC_BATCH, kv_rows, N_KV_HEADS, HEAD_DIM), 1.0),
        'w_ada': nrm(ks[7], (DEPTH, D_MODEL, 6 * D_MODEL), 0.5 * D_MODEL ** -0.5),
        'b_ada': nrm(ks[8], (DEPTH, 6 * D_MODEL), 0.02),
        'norm_gains': 1.0 + nrm(ks[9], (DEPTH, 4, D_MODEL), 0.05),
        'w_in': nrm(ks[10], (DEPTH, D_MODEL, IN_WIDTH), D_MODEL ** -0.5),
        'w_pool': nrm(ks[11], (DEPTH, N_POOL_GROUPS, POOL_GROUP_W, POOL_GROUP_W), POOL_GROUP_W ** -0.5),
        'pool_scale': 1.0 + nrm(ks[12], (DEPTH, POOL_WIDTH), 0.05),
        'attn_sinks': nrm(ks[13], (DEPTH, N_HEADS), 0.5),
        'w_out': nrm(ks[14], (DEPTH, D_MIX, D_MODEL), D_MIX ** -0.5),
        'w_router': nrm(ks[15], (DEPTH, D_MODEL, N_EXPERTS), D_MODEL ** -0.5),
        'router_bias': nrm(ks[16], (DEPTH, N_EXPERTS), 0.01),
        'w_gate': nrm(ks[17], (DEPTH, N_EXPERTS, D_MODEL, D_EXPERT), D_MODEL ** -0.5),
        'w_up': nrm(ks[18], (DEPTH, N_EXPERTS, D_MODEL, D_EXPERT), D_MODEL ** -0.5),
        'w_down': nrm(ks[19], (DEPTH, N_EXPERTS, D_EXPERT, D_MODEL), D_EXPERT ** -0.5),
        'ws_gate': nrm(ks[20], (DEPTH, D_MODEL, D_SHARED), D_MODEL ** -0.5),
        'ws_up': nrm(ks[21], (DEPTH, D_MODEL, D_SHARED), D_MODEL ** -0.5),
        'ws_down': nrm(ks[22], (DEPTH, D_SHARED, D_MODEL), D_SHARED ** -0.5),
    }


def reference(x_prompt, x_sample, c_prompt, c_sample, state_pool, cache_k, cache_v,
              w_ada, b_ada, norm_gains, w_in, w_pool, pool_scale, attn_sinks, w_out,
              w_router, router_bias, w_gate, w_up, w_down, ws_gate, ws_up, ws_down):
    yp, ys = x_prompt, x_sample
    pool_p, k_p, v_p, pool_s, k_s, v_s = [], [], [], [], [], []
    for l in range(DEPTH):
        lp = (w_ada[l], b_ada[l], norm_gains[l], w_in[l], w_pool[l], pool_scale[l], attn_sinks[l], w_out[l],
              w_router[l], router_bias[l], w_gate[l], w_up[l], w_down[l], ws_gate[l], ws_up[l], ws_down[l])
        yp, sp, kp, vp = trunk_layer(yp, c_prompt, lp, None, None, None)
        ys, ss, kss, vss = trunk_layer(ys, c_sample, lp, state_pool[l], cache_k[l], cache_v[l])
        pool_p.append(sp); k_p.append(kp); v_p.append(vp)
        pool_s.append(ss); k_s.append(kss); v_s.append(vss)
    new_pool_prompt = jnp.stack(pool_p)
    new_k_prompt = jnp.stack(k_p)
    new_v_prompt = jnp.stack(v_p)
    new_pool_sample = jnp.stack(pool_s)
    new_k_sample = jnp.stack(k_s)
    new_v_sample = jnp.stack(v_s)
    return (yp, ys, new_pool_prompt, new_k_prompt, new_v_prompt, new_pool_sample, new_k_sample, new_v_sample)
```

```python
import functools

import jax
import jax.numpy as jnp
import numpy as np
from jax import lax
from jax.experimental import pallas as pl
from jax.experimental.pallas import tpu as pltpu

F32 = jnp.float32
BF16 = jnp.bfloat16
I32 = jnp.int32

D_MODEL = 1024
CHUNK = 64
POOL_WIDTH = 512
POOL_WINDOWS = (2, 4, 8, 16)
POOL_GROUP_W = 128
POOL_HIST_PAD = 16
HEAD_DIM = 64
N_HEADS = 8
N_KV_HEADS = 2
GQ = N_HEADS // N_KV_HEADS
ATTN_WIDTH = N_HEADS * HEAD_DIM
KV_WIDTH = N_KV_HEADS * HEAD_DIM
IN_WIDTH = POOL_WIDTH + ATTN_WIDTH + 2 * KV_WIDTH
WINDOW = 128
N_EXPERTS = 64
TOP_K = 6
N_EXPERT_GROUPS = 8
GROUP_SIZE = N_EXPERTS // N_EXPERT_GROUPS
TOPK_GROUPS = 4
D_EXPERT = 256
D_SHARED = 256
ROUTED_SCALE = 2.5
NORM_EPS = 1e-6
NEG_BIG = -1e30

LANES = 128
ROUTE_ROWS = 8
EXPERT_ROWS = 256
DISPATCH_TILE = 256
COMBINE_TILE = 128
VMEM_LIMIT = 56 * 1024 * 1024

_NT = (((1,), (1,)), ((), ()))


def _rms(v):
    return v * lax.rsqrt(jnp.mean(v * v, axis=-1, keepdims=True) + NORM_EPS)


def _sigmoid(v):
    return 1.0 / (1.0 + jnp.exp(-v))


def _ada_body(c_ref, w_ref, b_ref, o_ref):
    c = c_ref[...]
    s = c * _sigmoid(c)
    o_ref[...] = jnp.dot(s, w_ref[...], preferred_element_type=F32) + b_ref[...]


def _ada_call(c_all, w_ada, b_ada):
    nb = c_all.shape[0]
    n_out = w_ada.shape[1]
    tile = D_MODEL
    return pl.pallas_call(
        _ada_body,
        grid=(n_out // tile,),
        in_specs=[
            pl.BlockSpec((nb, D_MODEL), lambda j: (0, 0)),
            pl.BlockSpec((D_MODEL, tile), lambda j: (0, j)),
            pl.BlockSpec((1, tile), lambda j: (0, j)),
        ],
        out_specs=pl.BlockSpec((nb, tile), lambda j: (0, j)),
        out_shape=jax.ShapeDtypeStruct((nb, n_out), F32),
        compiler_params=pltpu.CompilerParams(dimension_semantics=("arbitrary",)),
        name="ada",
    )(c_all, w_ada, b_ada.reshape(1, n_out))


def _front_body(tile, hist_keys, has_cache,
                sinks_ref, x_ref, mod_ref, gains_ref, hp_ref, hk_ref, hv_ref, cnt0_ref,
                win_ref, wpool_ref, pscale_ref, wout_ref, wrt_ref, rbias_ref,
                x1_ref, h2_ref, npool_ref, nk_ref, nv_ref, eidx_ref, wts_ref, rank_ref, cnt_ref,
                ubuf, khist, vhist, run):
    b = pl.program_id(0)
    i = pl.program_id(1)
    n_keys = hist_keys + tile

    @pl.when(i == 0)
    def _():
        ubuf[0:POOL_HIST_PAD, :] = hp_ref[0]
        khist[...] = hk_ref[0]
        vhist[...] = hv_ref[0]

    @pl.when((b == 0) & (i == 0))
    def _():
        run[...] = cnt0_ref[...]

    mod = mod_ref[0]
    gains = gains_ref[...]
    xt = x_ref[0]

    h = _rms(xt) * gains[0:1] * (1.0 + mod[1:2]) + mod[0:1]
    proj = jnp.dot(h.astype(BF16), win_ref[...], preferred_element_type=F32)
    u = proj[:, :POOL_WIDTH]
    o_k = POOL_WIDTH + ATTN_WIDTH
    k_new = proj[:, o_k:o_k + KV_WIDTH]
    v_new = proj[:, o_k + KV_WIDTH:]

    kw = jnp.concatenate([khist[...], k_new], axis=0)
    vw = jnp.concatenate([vhist[...], v_new], axis=0)
    kwb = kw.astype(BF16)
    vwb = vw.astype(BF16)
    qrow = lax.broadcasted_iota(I32, (tile, n_keys), 0)
    kpos = lax.broadcasted_iota(I32, (tile, n_keys), 1) - hist_keys
    qchunk = jnp.right_shift(qrow, 6)
    vis = (kpos >= CHUNK * (qchunk - 2)) & (kpos < CHUNK * (qchunk + 1))
    if not has_cache:
        vis = vis & (kpos + i * tile >= 0)
    side0 = lax.broadcasted_iota(I32, (tile, LANES), 1) < HEAD_DIM
    attn_blocks = []
    for j in range(GQ):
        qp = proj[:, POOL_WIDTH + LANES * j:POOL_WIDTH + LANES * (j + 1)]
        outs = []
        for s in range(N_KV_HEADS):
            keep = side0 if s == 0 else jnp.logical_not(side0)
            qm = jnp.where(keep, qp, 0.0).astype(BF16)
            sc = lax.dot_general(qm, kwb, _NT, preferred_element_type=F32) * (HEAD_DIM ** -0.5)
            sc = jnp.where(vis, sc, NEG_BIG)
            sink = sinks_ref[j + GQ * s]
            m = jnp.maximum(jnp.max(sc, axis=-1, keepdims=True), sink)
            p = jnp.exp(sc - m)
            den = jnp.sum(p, axis=-1, keepdims=True) + jnp.exp(sink - m)
            p = (p / den).astype(BF16)
            outs.append(jnp.dot(p, vwb, preferred_element_type=F32))
        attn_blocks.append(jnp.where(side0, outs[0], outs[1]))

    ubuf[POOL_HIST_PAD:POOL_HIST_PAD + tile, :] = u
    if has_cache:
        seen = None
    else:
        seen = (lax.broadcasted_iota(I32, (tile, 1), 0) + i * tile + 1).astype(F32)
    pool_blocks = []
    for g, w in enumerate(POOL_WINDOWS):
        cols = slice(POOL_GROUP_W * g, POOL_GROUP_W * (g + 1))
        acc = u[:, cols]
        for s in range(1, w):
            acc = acc + ubuf[POOL_HIST_PAD - s:POOL_HIST_PAD - s + tile, cols]
        cnt = float(w) if seen is None else jnp.minimum(seen, float(w))
        dlt = acc / cnt - u[:, cols]
        pool_blocks.append(jnp.dot(dlt.astype(BF16), wpool_ref[g], preferred_element_type=F32))
    pool = jnp.concatenate(pool_blocks, axis=-1) * pscale_ref[...]

    mixin = jnp.concatenate([pool] + attn_blocks, axis=-1).astype(BF16)
    mix = jnp.dot(mixin, wout_ref[...], preferred_element_type=F32)
    x1 = xt + mod[2:3] * (_rms(mix) * gains[1:2])
    x1_ref[0] = x1

    new_hist = ubuf[tile:tile + POOL_HIST_PAD, :]
    ubuf[0:POOL_HIST_PAD, :] = new_hist
    npool_ref[0] = new_hist
    khist[...] = kw[tile:, :]
    vhist[...] = vw[tile:, :]
    nk_ref[0] = kw[n_keys - WINDOW:, :]
    nv_ref[0] = vw[n_keys - WINDOW:, :]

    h2f = _rms(x1) * gains[2:3] * (1.0 + mod[4:5]) + mod[3:4]
    h2hi = h2f.astype(BF16)
    h2_ref[0] = h2hi
    h2lo = (h2f - h2hi.astype(F32)).astype(BF16)
    wrt = wrt_ref[...]
    part = lax.dot_general(wrt, h2hi, _NT, preferred_element_type=F32)
    logits = (part[:N_EXPERTS] + part[N_EXPERTS:]
              + lax.dot_general(wrt[:N_EXPERTS], h2lo, _NT, preferred_element_type=F32))
    scores = _sigmoid(logits)
    sel = scores + rbias_ref[...]

    sub_g = lax.broadcasted_iota(I32, (GROUP_SIZE, tile), 0).astype(F32)
    gscore = jnp.zeros((N_EXPERT_GROUPS, tile), F32)
    for gi in range(N_EXPERT_GROUPS):
        blk = sel[GROUP_SIZE * gi:GROUP_SIZE * (gi + 1), :]
        m1 = jnp.max(blk, axis=0, keepdims=True)
        i1 = jnp.min(jnp.where(blk == m1, sub_g, float(GROUP_SIZE)), axis=0, keepdims=True)
        m2 = jnp.max(jnp.where(sub_g == i1, -jnp.inf, blk), axis=0, keepdims=True)
        gscore = jnp.where(sub_g == gi, m1 + m2, gscore)
    chosen = jnp.zeros((N_EXPERT_GROUPS, tile), F32)
    for _ in range(TOPK_GROUPS):
        m = jnp.max(gscore, axis=0, keepdims=True)
        idx = jnp.min(jnp.where(gscore == m, sub_g, float(N_EXPERT_GROUPS)), axis=0, keepdims=True)
        pick = sub_g == idx
        chosen = jnp.where(pick, 1.0, chosen)
        gscore = jnp.where(pick, -jnp.inf, gscore)
    emask = jnp.concatenate(
        [jnp.broadcast_to(chosen[gi:gi + 1, :], (GROUP_SIZE, tile)) for gi in range(N_EXPERT_GROUPS)], axis=0)
    selm = jnp.where(emask > 0.0, sel, -jnp.inf)

    sub_e = lax.broadcasted_iota(I32, (N_EXPERTS, tile), 0).astype(F32)
    picks, idxs, raw_w = [], [], []
    for _ in range(TOP_K):
        m = jnp.max(selm, axis=0, keepdims=True)
        idx = jnp.min(jnp.where(selm == m, sub_e, float(N_EXPERTS)), axis=0, keepdims=True)
        pick = sub_e == idx
        raw_w.append(jnp.sum(jnp.where(pick, scores, 0.0), axis=0, keepdims=True))
        selm = jnp.where(pick, -jnp.inf, selm)
        picks.append(pick)
        idxs.append(idx)
    wsum = raw_w[0]
    for kk in range(1, TOP_K):
        wsum = wsum + raw_w[kk]

    onehot = jnp.zeros((N_EXPERTS, tile), F32)
    for kk in range(TOP_K):
        onehot = jnp.where(picks[kk], 1.0, onehot)
    tri = (lax.broadcasted_iota(I32, (tile, tile), 0) < lax.broadcasted_iota(I32, (tile, tile), 1)).astype(BF16)
    before = jnp.dot(onehot.astype(BF16), tri, preferred_element_type=F32) + run[:, 0:1]
    sub_r = lax.broadcasted_iota(I32, (ROUTE_ROWS, tile), 0)
    eidx_o = jnp.zeros((ROUTE_ROWS, tile), I32)
    wts_o = jnp.zeros((ROUTE_ROWS, tile), F32)
    rank_o = jnp.zeros((ROUTE_ROWS, tile), I32)
    for kk in range(TOP_K):
        rk = jnp.sum(jnp.where(picks[kk], before, 0.0), axis=0, keepdims=True).astype(I32)
        eidx_o = jnp.where(sub_r == kk, idxs[kk].astype(I32), eidx_o)
        wts_o = jnp.where(sub_r == kk, raw_w[kk] / wsum * ROUTED_SCALE, wts_o)
        rank_o = jnp.where(sub_r == kk, rk, rank_o)
    eidx_ref[0] = eidx_o
    wts_ref[0] = wts_o
    rank_ref[0] = rank_o
    new_run = run[...] + jnp.sum(onehot, axis=1, keepdims=True)
    run[...] = new_run
    cnt_ref[...] = new_run


def _front_call(x, mod, gains, hist_pool, hist_k, hist_v, cnt0, sinks,
                w_in, w_pool, pool_scale, w_out, wr_t, rbias, *, tile, has_cache):
    bsz, seq, _ = x.shape
    n_tiles = seq // tile
    hist_keys = hist_k.shape[1]
    body = functools.partial(_front_body, tile, hist_keys, has_cache)
    whole = lambda shape: pl.BlockSpec(shape, lambda b, i: (0,) * len(shape))
    per_b = lambda shape: pl.BlockSpec((1,) + shape, lambda b, i: (b,) + (0,) * len(shape))
    route = pl.BlockSpec((1, ROUTE_ROWS, tile), lambda b, i: (b * n_tiles + i, 0, 0))
    out_shape = [
        jax.ShapeDtypeStruct((bsz, seq, D_MODEL), F32),
        jax.ShapeDtypeStruct((bsz, seq, D_MODEL), BF16),
        jax.ShapeDtypeStruct((bsz, POOL_HIST_PAD, POOL_WIDTH), F32),
        jax.ShapeDtypeStruct((bsz, WINDOW, KV_WIDTH), F32),
        jax.ShapeDtypeStruct((bsz, WINDOW, KV_WIDTH), F32),
        jax.ShapeDtypeStruct((bsz * n_tiles, ROUTE_ROWS, tile), I32),
        jax.ShapeDtypeStruct((bsz * n_tiles, ROUTE_ROWS, tile), F32),
        jax.ShapeDtypeStruct((bsz * n_tiles, ROUTE_ROWS, tile), I32),
        jax.ShapeDtypeStruct((N_EXPERTS, LANES), F32),
    ]
    return pl.pallas_call(
        body,
        grid=(bsz, n_tiles),
        in_specs=[
            pl.BlockSpec(memory_space=pltpu.SMEM),
            pl.BlockSpec((1, tile, D_MODEL), lambda b, i: (b, i, 0)),
            per_b((6, D_MODEL)),
            whole((4, D_MODEL)),
            per_b((POOL_HIST_PAD, POOL_WIDTH)),
            per_b((hist_keys, KV_WIDTH)),
            per_b((hist_keys, KV_WIDTH)),
            whole((N_EXPERTS, LANES)),
            whole((D_MODEL, IN_WIDTH)),
            whole((len(POOL_WINDOWS), POOL_GROUP_W, POOL_GROUP_W)),
            whole((1, POOL_WIDTH)),
            whole((D_MODEL, D_MODEL)),
            whole((2 * N_EXPERTS, D_MODEL)),
            whole((N_EXPERTS, 1)),
        ],
        out_specs=[
            pl.BlockSpec((1, tile, D_MODEL), lambda b, i: (b, i, 0)),
            pl.BlockSpec((1, tile, D_MODEL), lambda b, i: (b, i, 0)),
            per_b((POOL_HIST_PAD, POOL_WIDTH)),
            per_b((WINDOW, KV_WIDTH)),
            per_b((WINDOW, KV_WIDTH)),
            route, route, route,
            whole((N_EXPERTS, LANES)),
        ],
        out_shape=out_shape,
        scratch_shapes=[
            pltpu.VMEM((tile + POOL_HIST_PAD, POOL_WIDTH), F32),
            pltpu.VMEM((hist_keys, KV_WIDTH), F32),
            pltpu.VMEM((hist_keys, KV_WIDTH), F32),
            pltpu.VMEM((N_EXPERTS, LANES), F32),
        ],
        compiler_params=pltpu.CompilerParams(
            dimension_semantics=("arbitrary", "arbitrary"), vmem_limit_bytes=VMEM_LIMIT),
        name="front_cached" if has_cache else "front_prompt",
    )(sinks, x, mod, gains, hist_pool, hist_k, hist_v, cnt0,
      w_in, w_pool, pool_scale, w_out, wr_t, rbias)


def _row_copy(src, dst, s_row, d_row, sem):
    return pltpu.make_async_copy(src.at[pl.ds(s_row, 1)], dst.at[pl.ds(d_row, 1)], sem)


def _dispatch_body(h2_ref, dest_hbm, xs_in, xs_out, rows, dsm, isem, sem):
    del xs_in
    i = pl.program_id(0)
    icopy = pltpu.make_async_copy(dest_hbm.at[pl.ds(i, 1)], dsm, isem)
    icopy.start()
    rows[...] = h2_ref[...].astype(F32)
    icopy.wait()

    def issue(t, carry):
        for kk in range(TOP_K):
            _row_copy(rows, xs_out, t, dsm[0, kk * DISPATCH_TILE + t], sem).start()
        return carry

    lax.fori_loop(0, DISPATCH_TILE, issue, 0)

    def drain(t, carry):
        for kk in range(TOP_K):
            _row_copy(rows, xs_out, 0, 0, sem).wait()
        return carry

    lax.fori_loop(0, DISPATCH_TILE, drain, 0)


def _dispatch_call(h2_all, dest_tiles, xs_zero):
    n_tok = h2_all.shape[0]
    n_steps = n_tok // DISPATCH_TILE
    return pl.pallas_call(
        _dispatch_body,
        grid=(n_steps,),
        in_specs=[
            pl.BlockSpec((DISPATCH_TILE, D_MODEL), lambda i: (i, 0)),
            pl.BlockSpec(memory_space=pl.ANY),
            pl.BlockSpec(memory_space=pl.ANY),
        ],
        out_specs=pl.BlockSpec(memory_space=pl.ANY),
        out_shape=jax.ShapeDtypeStruct(xs_zero.shape, F32),
        scratch_shapes=[
            pltpu.VMEM((DISPATCH_TILE, D_MODEL), F32),
            pltpu.SMEM((1, TOP_K * DISPATCH_TILE), I32),
            pltpu.SemaphoreType.DMA,
            pltpu.SemaphoreType.DMA,
        ],
        input_output_aliases={2: 0},
        compiler_params=pltpu.CompilerParams(dimension_semantics=("arbitrary",)),
        name="dispatch",
    )(h2_all, dest_tiles, xs_zero)


def _experts_body(be_ref, nu_ref, x_ref, wg_ref, wu_ref, wd_ref, o_ref):
    j = pl.program_id(0)

    @pl.when(j < nu_ref[0])
    def _():
        x = x_ref[...].astype(BF16)
        g = jnp.dot(x, wg_ref[0].astype(BF16), preferred_element_type=F32)
        u = jnp.dot(x, wu_ref[0].astype(BF16), preferred_element_type=F32)
        a = (g * _sigmoid(g) * u).astype(BF16)
        o_ref[...] = jnp.dot(a, wd_ref[0].astype(BF16), preferred_element_type=F32)

    @pl.when(j >= nu_ref[0])
    def _():
        o_ref[...] = jnp.zeros_like(o_ref)


def _experts_call(block_e, n_used, xs, w_gate, w_up, w_down):
    n_blocks = xs.shape[0] // EXPERT_ROWS
    grid_spec = pltpu.PrefetchScalarGridSpec(
        num_scalar_prefetch=2,
        grid=(n_blocks,),
        in_specs=[
            pl.BlockSpec((EXPERT_ROWS, D_MODEL), lambda j, be, nu: (jnp.minimum(j, nu[0] - 1), 0)),
            pl.BlockSpec((1, D_MODEL, D_EXPERT), lambda j, be, nu: (be[j], 0, 0)),
            pl.BlockSpec((1, D_MODEL, D_EXPERT), lambda j, be, nu: (be[j], 0, 0)),
            pl.BlockSpec((1, D_EXPERT, D_MODEL), lambda j, be, nu: (be[j], 0, 0)),
        ],
        out_specs=pl.BlockSpec((EXPERT_ROWS, D_MODEL), lambda j, be, nu: (j, 0)),
    )
    return pl.pallas_call(
        _experts_body,
        grid_spec=grid_spec,
        out_shape=jax.ShapeDtypeStruct(xs.shape, F32),
        compiler_params=pltpu.CompilerParams(
            dimension_semantics=("arbitrary",), vmem_limit_bytes=VMEM_LIMIT),
        name="experts",
    )(block_e, n_used, xs, w_gate, w_up, w_down)


def _combine_body(n_prompt_steps,
                  x1_ref, h2_ref, moda_ref, modb_ref, gains_ref, wts_ref, dest_hbm, ys_hbm,
                  wsg_ref, wsu_ref, wsd_ref, outp_ref, outs_ref, gath, dsm, isem, sem):
    i = pl.program_id(0)
    icopy = pltpu.make_async_copy(dest_hbm.at[pl.ds(i, 1)], dsm, isem)
    icopy.start()
    icopy.wait()

    def issue(t, carry):
        for kk in range(TOP_K):
            _row_copy(ys_hbm, gath.at[kk], dsm[0, kk * COMBINE_TILE + t], t, sem).start()
        return carry

    lax.fori_loop(0, COMBINE_TILE, issue, 0)

    h2 = h2_ref[...]
    g = jnp.dot(h2, wsg_ref[...], preferred_element_type=F32)
    u = jnp.dot(h2, wsu_ref[...], preferred_element_type=F32)
    a = (g * _sigmoid(g) * u).astype(BF16)
    ffn = jnp.dot(a, wsd_ref[...], preferred_element_type=F32)

    w = wts_ref[0]
    w_hi = w.astype(BF16)
    r1 = w - w_hi.astype(F32)
    w_mid = r1.astype(BF16)
    w_lo = (r1 - w_mid.astype(F32)).astype(BF16)
    eye = (lax.broadcasted_iota(I32, (COMBINE_TILE, COMBINE_TILE), 0)
           == lax.broadcasted_iota(I32, (COMBINE_TILE, COMBINE_TILE), 1)).astype(BF16)
    w_t = (lax.dot_general(eye, w_hi, _NT, preferred_element_type=F32)
           + lax.dot_general(eye, w_mid, _NT, preferred_element_type=F32)
           + lax.dot_general(eye, w_lo, _NT, preferred_element_type=F32))

    def drain(t, carry):
        for kk in range(TOP_K):
            _row_copy(ys_hbm, gath.at[kk], 0, 0, sem).wait()
        return carry

    lax.fori_loop(0, COMBINE_TILE, drain, 0)

    for kk in range(TOP_K):
        ffn = ffn + w_t[:, kk:kk + 1] * gath[kk]

    half = COMBINE_TILE // 2
    gains = gains_ref[...]
    gate = jnp.concatenate([jnp.broadcast_to(moda_ref[0][5:6], (half, D_MODEL)),
                            jnp.broadcast_to(modb_ref[0][5:6], (half, D_MODEL))], axis=0)
    out = x1_ref[...] + gate * (_rms(ffn) * gains[3:4])

    @pl.when(i < n_prompt_steps)
    def _():
        outp_ref[...] = out

    @pl.when(i >= n_prompt_steps)
    def _():
        outs_ref[...] = out


def _combine_call(x1_all, h2_all, mod, gains, wts_tiles, dest_tiles, ys,
                  ws_gate, ws_up, ws_down, *, n_prompt, seq_prompt, seq_sample):
    n_tok = x1_all.shape[0]
    n_steps = n_tok // COMBINE_TILE
    n_prompt_steps = n_prompt // COMBINE_TILE
    n_sample = n_tok - n_prompt
    half = COMBINE_TILE // 2
    b_prompt = n_prompt // seq_prompt

    def batch_of(c):
        return jnp.where(c < n_prompt // half, c // (seq_prompt // half),
                         b_prompt + (c - n_prompt // half) // (seq_sample // half))

    body = functools.partial(_combine_body, n_prompt_steps)
    tok = pl.BlockSpec((COMBINE_TILE, D_MODEL), lambda i: (i, 0))
    whole = lambda shape: pl.BlockSpec(shape, lambda i: (0,) * len(shape))
    return pl.pallas_call(
        body,
        grid=(n_steps,),
        in_specs=[
            tok, tok,
            pl.BlockSpec((1, 6, D_MODEL), lambda i: (batch_of(2 * i), 0, 0)),
            pl.BlockSpec((1, 6, D_MODEL), lambda i: (batch_of(2 * i + 1), 0, 0)),
            whole((4, D_MODEL)),
            pl.BlockSpec((1, ROUTE_ROWS, COMBINE_TILE), lambda i: (i, 0, 0)),
            pl.BlockSpec(memory_space=pl.ANY),
            pl.BlockSpec(memory_space=pl.ANY),
            whole((D_MODEL, D_SHARED)), whole((D_MODEL, D_SHARED)), whole((D_SHARED, D_MODEL)),
        ],
        out_specs=[
            pl.BlockSpec((COMBINE_TILE, D_MODEL), lambda i: (jnp.minimum(i, n_prompt_steps - 1), 0)),
            pl.BlockSpec((COMBINE_TILE, D_MODEL), lambda i: (jnp.maximum(i - n_prompt_steps, 0), 0)),
        ],
        out_shape=[jax.ShapeDtypeStruct((n_prompt, D_MODEL), F32),
                   jax.ShapeDtypeStruct((n_sample, D_MODEL), F32)],
        scratch_shapes=[
            pltpu.VMEM((TOP_K, COMBINE_TILE, D_MODEL), F32),
            pltpu.SMEM((1, TOP_K * COMBINE_TILE), I32),
            pltpu.SemaphoreType.DMA,
            pltpu.SemaphoreType.DMA,
        ],
        compiler_params=pltpu.CompilerParams(
            dimension_semantics=("arbitrary",), vmem_limit_bytes=VMEM_LIMIT),
        name="combine",
    )(x1_all, h2_all, mod, mod, gains, wts_tiles, dest_tiles, ys, ws_gate, ws_up, ws_down)


def _pair_layout_perm():
    perm = []
    for j in range(GQ):
        for s in range(N_KV_HEADS):
            head = j + GQ * s
            perm.extend(range(head * HEAD_DIM, (head + 1) * HEAD_DIM))
    return np.asarray(perm, np.int32)


def _tiles_of(route, tile):
    n_tok = route.shape[1]
    r = route[:TOP_K].reshape(TOP_K, n_tok // tile, tile)
    return jnp.transpose(r, (1, 0, 2)).reshape(n_tok // tile, TOP_K * tile)


def kernel(x_prompt, x_sample, c_prompt, c_sample, state_pool, cache_k, cache_v, w_ada, b_ada, norm_gains,
           w_in, w_pool, pool_scale, attn_sinks, w_out, w_router, router_bias, w_gate, w_up, w_down,
           ws_gate, ws_up, ws_down):
    assert w_ada.shape[0] == 1, "single-layer kernel"
    bsz, seq, _ = x_prompt.shape
    dbsz, dseq, _ = x_sample.shape
    n_prompt, n_sample = bsz * seq, dbsz * dseq
    n_tok = n_prompt + n_sample
    assert dseq == CHUNK and seq % 128 == 0 and n_tok % DISPATCH_TILE == 0 and n_prompt % COMBINE_TILE == 0

    perm = _pair_layout_perm()
    w_in0, w_out0 = w_in[0], w_out[0]
    w_in_p = jnp.concatenate(
        [w_in0[:, :POOL_WIDTH], w_in0[:, POOL_WIDTH:POOL_WIDTH + ATTN_WIDTH][:, perm],
         w_in0[:, POOL_WIDTH + ATTN_WIDTH:]], axis=1).astype(BF16)
    w_out_p = jnp.concatenate([w_out0[:POOL_WIDTH], w_out0[POOL_WIDTH:][perm]], axis=0).astype(BF16)
    w_pool_b = w_pool[0].astype(BF16)
    pscale = pool_scale[0].reshape(1, POOL_WIDTH)
    wr_t = w_router[0].T
    wr_hi = wr_t.astype(BF16)
    wr_lo = (wr_t - wr_hi.astype(F32)).astype(BF16)
    wr_split = jnp.concatenate([wr_hi, wr_lo], axis=0)
    rbias = router_bias[0].reshape(N_EXPERTS, 1)
    gains = norm_gains[0]
    sinks = attn_sinks[0]

    mod = _ada_call(jnp.concatenate([c_prompt, c_sample], axis=0), w_ada[0], b_ada[0])
    mod = mod.reshape(bsz + dbsz, 6, D_MODEL)

    zeros_pool = jnp.zeros((bsz, POOL_HIST_PAD, POOL_WIDTH), F32)
    zeros_kv = jnp.zeros((bsz, WINDOW, KV_WIDTH), F32)
    cnt0 = jnp.zeros((N_EXPERTS, LANES), F32)
    (x1_p, h2_p, pool_p, nk_p, nv_p, eidx_p, wts_p, rank_p, cnt_p) = _front_call(
        x_prompt, mod[:bsz], gains, zeros_pool, zeros_kv, zeros_kv, cnt0, sinks,
        w_in_p, w_pool_b, pscale, w_out_p, wr_split, rbias, tile=128, has_cache=False)

    hist_pad = 2 * LANES - CHUNK - WINDOW
    pool_s0 = jnp.pad(state_pool[0], ((0, 0), (POOL_HIST_PAD - state_pool.shape[2], 0), (0, 0)))
    ck = jnp.pad(cache_k[0].reshape(dbsz, WINDOW, KV_WIDTH), ((0, 0), (hist_pad, 0), (0, 0)))
    cv = jnp.pad(cache_v[0].reshape(dbsz, WINDOW, KV_WIDTH), ((0, 0), (hist_pad, 0), (0, 0)))
    (x1_s, h2_s, pool_s, nk_s, nv_s, eidx_s, wts_s, rank_s, cnt_all) = _front_call(
        x_sample, mod[bsz:], gains, pool_s0, ck, cv, cnt_p, sinks,
        w_in_p, w_pool_b, pscale, w_out_p, wr_split, rbias, tile=CHUNK, has_cache=True)

    def flat(r):
        return jnp.transpose(r, (1, 0, 2)).reshape(ROUTE_ROWS, -1)

    eidx = jnp.concatenate([flat(eidx_p), flat(eidx_s)], axis=1)
    rank = jnp.concatenate([flat(rank_p), flat(rank_s)], axis=1)
    wts = jnp.concatenate([flat(wts_p), flat(wts_s)], axis=1)
    counts = cnt_all[:, 0].astype(I32)
    padded = (counts + EXPERT_ROWS - 1) // EXPERT_ROWS * EXPERT_ROWS
    pends = jnp.cumsum(padded)
    pstarts = pends - padded
    dest = pstarts[eidx] + rank
    n_blocks = -(-(n_tok * TOP_K + N_EXPERTS * (EXPERT_ROWS - 1)) // EXPERT_ROWS)
    n_used = (pends[-1] // EXPERT_ROWS).astype(I32).reshape(1)
    block_start = jnp.arange(n_blocks, dtype=I32) * EXPERT_ROWS
    block_start = jnp.minimum(block_start, pends[-1] - EXPERT_ROWS)
    block_e = jnp.minimum(jnp.sum(block_start[:, None] >= pends[None, :], axis=-1), N_EXPERTS - 1).astype(I32)

    h2_all = jnp.concatenate([h2_p.reshape(n_prompt, D_MODEL), h2_s.reshape(n_sample, D_MODEL)], axis=0)
    x1_all = jnp.concatenate([x1_p.reshape(n_prompt, D_MODEL), x1_s.reshape(n_sample, D_MODEL)], axis=0)

    xs = _dispatch_call(h2_all, _tiles_of(dest, DISPATCH_TILE),
                        jnp.zeros((n_blocks * EXPERT_ROWS, D_MODEL), F32))
    ys = _experts_call(block_e, n_used, xs, w_gate[0], w_up[0], w_down[0])
    wts_tiles = jnp.transpose(wts.reshape(ROUTE_ROWS, n_tok // COMBINE_TILE, COMBINE_TILE), (1, 0, 2))
    y_p, y_s = _combine_call(
        x1_all, h2_all, mod, gains, wts_tiles, _tiles_of(dest, COMBINE_TILE), ys,
        ws_gate[0].astype(BF16), ws_up[0].astype(BF16), ws_down[0].astype(BF16),
        n_prompt=n_prompt, seq_prompt=seq, seq_sample=dseq)

    n_hist = state_pool.shape[2]
    kv_shape = (1, -1, WINDOW, N_KV_HEADS, HEAD_DIM)
    return (y_p.reshape(bsz, seq, D_MODEL), y_s.reshape(dbsz, dseq, D_MODEL),
            pool_p[None, :, POOL_HIST_PAD - n_hist:], nk_p.reshape(kv_shape), nv_p.reshape(kv_shape),
            pool_s[None, :, POOL_HIST_PAD - n_hist:], nk_s.reshape(kv_shape), nv_s.reshape(kv_shape))
```

```python
import functools

import jax
import jax.numpy as jnp
import numpy as np
from jax import lax
from jax.experimental import pallas as pl
from jax.experimental.pallas import tpu as pltpu

F32 = jnp.float32
BF16 = jnp.bfloat16
I32 = jnp.int32

D_MODEL = 1024
CHUNK = 64
POOL_WIDTH = 512
POOL_WINDOWS = (2, 4, 8, 16)
POOL_GROUP_W = 128
POOL_HIST_PAD = 16
HEAD_DIM = 64
N_HEADS = 8
N_KV_HEADS = 2
GQ = N_HEADS // N_KV_HEADS
ATTN_WIDTH = N_HEADS * HEAD_DIM
KV_WIDTH = N_KV_HEADS * HEAD_DIM
IN_WIDTH = POOL_WIDTH + ATTN_WIDTH + 2 * KV_WIDTH
WINDOW = 128
N_EXPERTS = 64
TOP_K = 6
N_EXPERT_GROUPS = 8
GROUP_SIZE = N_EXPERTS // N_EXPERT_GROUPS
TOPK_GROUPS = 4
D_EXPERT = 256
D_SHARED = 256
ROUTED_SCALE = 2.5
NORM_EPS = 1e-6
NEG_BIG = -1e30

LANES = 128
SUBLANES = 8
ATTN_KEYS = 2 * LANES
ROUTE_ROWS = 8
EXPERT_ROWS = 256
MOE_TILE = 256
GATE_GROUP = MOE_TILE // SUBLANES
RUN_ROWS = 2 * SUBLANES
LOCAL_ROWS = 2560
ROW_W = D_MODEL + LANES
VMEM_LIMIT = 56 * 1024 * 1024

assert TOP_K * MOE_TILE + N_EXPERTS * (RUN_ROWS - 1) <= LOCAL_ROWS

_NT = (((1,), (1,)), ((), ()))


def _rms(v):
    return v * lax.rsqrt(jnp.mean(v * v, axis=-1, keepdims=True) + NORM_EPS)


def _sigmoid(v):
    return 1.0 / (1.0 + jnp.exp(-v))


def _split3(v):
    hi = v.astype(BF16)
    r1 = v - hi.astype(F32)
    mid = r1.astype(BF16)
    lo = (r1 - mid.astype(F32)).astype(BF16)
    return hi, mid, lo


def _to_sublanes(rows):
    n = rows.shape[1]
    hi, mid, lo = _split3(rows)
    eye = (lax.broadcasted_iota(I32, (n, n), 0) == lax.broadcasted_iota(I32, (n, n), 1)).astype(BF16)
    return (lax.dot_general(eye, hi, _NT, preferred_element_type=F32)
            + lax.dot_general(eye, mid, _NT, preferred_element_type=F32)
            + lax.dot_general(eye, lo, _NT, preferred_element_type=F32))


def _ada_body(c_ref, w_ref, b_ref, o_ref):
    c = c_ref[...]
    s = c * _sigmoid(c)
    o_ref[...] = jnp.dot(s, w_ref[...], preferred_element_type=F32) + b_ref[...]


def _ada_call(c_all, w_ada, b_ada):
    nb = c_all.shape[0]
    n_out = w_ada.shape[1]
    tile = D_MODEL
    return pl.pallas_call(
        _ada_body,
        grid=(n_out // tile,),
        in_specs=[
            pl.BlockSpec((nb, D_MODEL), lambda j: (0, 0)),
            pl.BlockSpec((D_MODEL, tile), lambda j: (0, j)),
            pl.BlockSpec((1, tile), lambda j: (0, j)),
        ],
        out_specs=pl.BlockSpec((nb, tile), lambda j: (0, j)),
        out_shape=jax.ShapeDtypeStruct((nb, n_out), F32),
        compiler_params=pltpu.CompilerParams(dimension_semantics=("arbitrary",)),
        name="ada",
    )(c_all, w_ada, b_ada.reshape(1, n_out))


def _front_body(tile, has_cache,
                sinks_ref, x_ref, mod_ref, gains_ref, hp_ref, hk_ref, hv_ref, cnt0_ref,
                win_ref, wpool_ref, pscale_ref, wout_ref, wrt_ref, rbias_ref,
                x1_ref, h2_ref, npool_ref, nk_ref, nv_ref, eidx_ref, wts_ref, rank_ref, tcnt_ref,
                ubuf, khist, vhist, run):
    b = pl.program_id(0)
    i = pl.program_id(1)
    sub_q = min(tile, LANES)
    n_sub = tile // sub_q
    hist_keys = ATTN_KEYS - sub_q
    n_keys = hist_keys + tile

    @pl.when(i == 0)
    def _():
        ubuf[0:POOL_HIST_PAD, :] = hp_ref[0]
        khist[...] = hk_ref[0]
        vhist[...] = hv_ref[0]

    @pl.when((b == 0) & (i == 0))
    def _():
        run[...] = cnt0_ref[...]

    mod = mod_ref[0]
    gains = gains_ref[...]
    xt = x_ref[0]

    h = _rms(xt) * gains[0:1] * (1.0 + mod[1:2]) + mod[0:1]
    proj = jnp.dot(h.astype(BF16), win_ref[...], preferred_element_type=F32)
    u = proj[:, :POOL_WIDTH]
    o_k = POOL_WIDTH + ATTN_WIDTH
    k_new = proj[:, o_k:o_k + KV_WIDTH]
    v_new = proj[:, o_k + KV_WIDTH:]

    kw = jnp.concatenate([khist[...], k_new], axis=0)
    vw = jnp.concatenate([vhist[...], v_new], axis=0)
    kwb = kw.astype(BF16)
    vwb = vw.astype(BF16)
    qrow = lax.broadcasted_iota(I32, (sub_q, ATTN_KEYS), 0)
    kpos = lax.broadcasted_iota(I32, (sub_q, ATTN_KEYS), 1) - hist_keys
    qchunk = jnp.right_shift(qrow, 6)
    vis_band = (kpos >= CHUNK * (qchunk - 2)) & (kpos < CHUNK * (qchunk + 1))
    side0 = lax.broadcasted_iota(I32, (sub_q, LANES), 1) < HEAD_DIM
    attn_rows = []
    for r in range(n_sub):
        vis = vis_band
        if not has_cache:
            vis = vis & (kpos + (i * tile + r * sub_q) >= 0)
        kb = kwb[r * sub_q:r * sub_q + ATTN_KEYS]
        vb = vwb[r * sub_q:r * sub_q + ATTN_KEYS]
        blocks = []
        for j in range(GQ):
            qp = proj[r * sub_q:(r + 1) * sub_q, POOL_WIDTH + LANES * j:POOL_WIDTH + LANES * (j + 1)]
            outs = []
            for s in range(N_KV_HEADS):
                keep = side0 if s == 0 else jnp.logical_not(side0)
                qm = jnp.where(keep, qp, 0.0).astype(BF16)
                sc = lax.dot_general(qm, kb, _NT, preferred_element_type=F32) * (HEAD_DIM ** -0.5)
                sc = jnp.where(vis, sc, NEG_BIG)
                sink = sinks_ref[j + GQ * s]
                m = jnp.maximum(jnp.max(sc, axis=-1, keepdims=True), sink)
                p = jnp.exp(sc - m)
                den = jnp.sum(p, axis=-1, keepdims=True) + jnp.exp(sink - m)
                p = (p / den).astype(BF16)
                outs.append(jnp.dot(p, vb, preferred_element_type=F32))
            blocks.append(jnp.where(side0, outs[0], outs[1]))
        attn_rows.append(jnp.concatenate(blocks, axis=-1))
    attn = attn_rows[0] if n_sub == 1 else jnp.concatenate(attn_rows, axis=0)

    ubuf[POOL_HIST_PAD:POOL_HIST_PAD + tile, :] = u
    if has_cache:
        seen = None
    else:
        seen = (lax.broadcasted_iota(I32, (tile, 1), 0) + i * tile + 1).astype(F32)
    pool_blocks = []
    for g, w in enumerate(POOL_WINDOWS):
        cols = slice(POOL_GROUP_W * g, POOL_GROUP_W * (g + 1))
        acc = u[:, cols]
        for s in range(1, w):
            acc = acc + ubuf[POOL_HIST_PAD - s:POOL_HIST_PAD - s + tile, cols]
        cnt = float(w) if seen is None else jnp.minimum(seen, float(w))
        dlt = acc / cnt - u[:, cols]
        pool_blocks.append(jnp.dot(dlt.astype(BF16), wpool_ref[g], preferred_element_type=F32))
    pool = jnp.concatenate(pool_blocks, axis=-1) * pscale_ref[...]

    mixin = jnp.concatenate([pool, attn], axis=-1).astype(BF16)
    mix = jnp.dot(mixin, wout_ref[...], preferred_element_type=F32)
    x1 = xt + mod[2:3] * (_rms(mix) * gains[1:2])
    x1_ref[0] = x1

    new_hist = ubuf[tile:tile + POOL_HIST_PAD, :]
    ubuf[0:POOL_HIST_PAD, :] = new_hist
    npool_ref[0] = new_hist
    khist[...] = kw[tile:, :]
    vhist[...] = vw[tile:, :]
    nk_ref[0] = kw[n_keys - WINDOW:, :]
    nv_ref[0] = vw[n_keys - WINDOW:, :]

    h2f = _rms(x1) * gains[2:3] * (1.0 + mod[4:5]) + mod[3:4]
    h2hi = h2f.astype(BF16)
    h2_ref[0] = h2hi
    h2lo = (h2f - h2hi.astype(F32)).astype(BF16)
    wrt = wrt_ref[...]
    part = lax.dot_general(wrt, h2hi, _NT, preferred_element_type=F32)
    logits = (part[:N_EXPERTS] + part[N_EXPERTS:]
              + lax.dot_general(wrt[:N_EXPERTS], h2lo, _NT, preferred_element_type=F32))
    scores = _sigmoid(logits)
    sel = scores + rbias_ref[...]

    sub_g = lax.broadcasted_iota(I32, (GROUP_SIZE, tile), 0).astype(F32)
    gscore = jnp.zeros((N_EXPERT_GROUPS, tile), F32)
    for gi in range(N_EXPERT_GROUPS):
        blk = sel[GROUP_SIZE * gi:GROUP_SIZE * (gi + 1), :]
        m1 = jnp.max(blk, axis=0, keepdims=True)
        i1 = jnp.min(jnp.where(blk == m1, sub_g, float(GROUP_SIZE)), axis=0, keepdims=True)
        m2 = jnp.max(jnp.where(sub_g == i1, -jnp.inf, blk), axis=0, keepdims=True)
        gscore = jnp.where(sub_g == gi, m1 + m2, gscore)
    chosen = jnp.zeros((N_EXPERT_GROUPS, tile), F32)
    for _ in range(TOPK_GROUPS):
        m = jnp.max(gscore, axis=0, keepdims=True)
        idx = jnp.min(jnp.where(gscore == m, sub_g, float(N_EXPERT_GROUPS)), axis=0, keepdims=True)
        pick = sub_g == idx
        chosen = jnp.where(pick, 1.0, chosen)
        gscore = jnp.where(pick, -jnp.inf, gscore)
    emask = jnp.concatenate(
        [jnp.broadcast_to(chosen[gi:gi + 1, :], (GROUP_SIZE, tile)) for gi in range(N_EXPERT_GROUPS)], axis=0)
    selm = jnp.where(emask > 0.0, sel, -jnp.inf)

    sub_e = lax.broadcasted_iota(I32, (N_EXPERTS, tile), 0).astype(F32)
    picks, idxs, raw_w = [], [], []
    for _ in range(TOP_K):
        m = jnp.max(selm, axis=0, keepdims=True)
        idx = jnp.min(jnp.where(selm == m, sub_e, float(N_EXPERTS)), axis=0, keepdims=True)
        pick = sub_e == idx
        raw_w.append(jnp.sum(jnp.where(pick, scores, 0.0), axis=0, keepdims=True))
        selm = jnp.where(pick, -jnp.inf, selm)
        picks.append(pick)
        idxs.append(idx)
    wsum = raw_w[0]
    for kk in range(1, TOP_K):
        wsum = wsum + raw_w[kk]

    onehot = jnp.zeros((N_EXPERTS, tile), F32)
    for kk in range(TOP_K):
        onehot = jnp.where(picks[kk], 1.0, onehot)
    onehot_b = onehot.astype(BF16)
    tri = (lax.broadcasted_iota(I32, (tile, tile), 0) < lax.broadcasted_iota(I32, (tile, tile), 1)).astype(BF16)
    before = jnp.dot(onehot_b, tri, preferred_element_type=F32) + run[:, 0:1]
    sub_r = lax.broadcasted_iota(I32, (ROUTE_ROWS, tile), 0)
    eidx_o = jnp.zeros((ROUTE_ROWS, tile), I32)
    wts_o = jnp.zeros((ROUTE_ROWS, tile), F32)
    rank_o = jnp.zeros((ROUTE_ROWS, tile), I32)
    for kk in range(TOP_K):
        rk = jnp.sum(jnp.where(picks[kk], before, 0.0), axis=0, keepdims=True).astype(I32)
        eidx_o = jnp.where(sub_r == kk, idxs[kk].astype(I32), eidx_o)
        wts_o = jnp.where(sub_r == kk, raw_w[kk] / wsum * ROUTED_SCALE, wts_o)
        rank_o = jnp.where(sub_r == kk, rk, rank_o)
    eidx_ref[0] = eidx_o
    wts_ref[0] = wts_o
    rank_ref[0] = rank_o
    tcnt_ref[0] = lax.dot_general(jnp.ones((ROUTE_ROWS, tile), BF16), onehot_b, _NT,
                                  preferred_element_type=F32)
    run[...] = run[...] + jnp.sum(onehot, axis=1, keepdims=True)


def _front_call(x, mod, gains, hist_pool, hist_k, hist_v, cnt0, sinks,
                w_in, w_pool, pool_scale, w_out, wr_t, rbias, *, tile, has_cache):
    bsz, seq, _ = x.shape
    n_tiles = seq // tile
    hist_keys = hist_k.shape[1]
    assert hist_keys == ATTN_KEYS - min(tile, LANES)
    body = functools.partial(_front_body, tile, has_cache)
    whole = lambda shape: pl.BlockSpec(shape, lambda b, i: (0,) * len(shape))
    per_b = lambda shape: pl.BlockSpec((1,) + shape, lambda b, i: (b,) + (0,) * len(shape))
    route = pl.BlockSpec((1, ROUTE_ROWS, tile), lambda b, i: (b * n_tiles + i, 0, 0))
    out_shape = [
        jax.ShapeDtypeStruct((bsz, seq, D_MODEL), F32),
        jax.ShapeDtypeStruct((bsz, seq, D_MODEL), BF16),
        jax.ShapeDtypeStruct((bsz, POOL_HIST_PAD, POOL_WIDTH), F32),
        jax.ShapeDtypeStruct((bsz, WINDOW, KV_WIDTH), F32),
        jax.ShapeDtypeStruct((bsz, WINDOW, KV_WIDTH), F32),
        jax.ShapeDtypeStruct((bsz * n_tiles, ROUTE_ROWS, tile), I32),
        jax.ShapeDtypeStruct((bsz * n_tiles, ROUTE_ROWS, tile), F32),
        jax.ShapeDtypeStruct((bsz * n_tiles, ROUTE_ROWS, tile), I32),
        jax.ShapeDtypeStruct((bsz * n_tiles, ROUTE_ROWS, N_EXPERTS), F32),
    ]
    return pl.pallas_call(
        body,
        grid=(bsz, n_tiles),
        in_specs=[
            pl.BlockSpec(memory_space=pltpu.SMEM),
            pl.BlockSpec((1, tile, D_MODEL), lambda b, i: (b, i, 0)),
            per_b((6, D_MODEL)),
            whole((4, D_MODEL)),
            per_b((POOL_HIST_PAD, POOL_WIDTH)),
            per_b((hist_keys, KV_WIDTH)),
            per_b((hist_keys, KV_WIDTH)),
            whole((N_EXPERTS, LANES)),
            whole((D_MODEL, IN_WIDTH)),
            whole((len(POOL_WINDOWS), POOL_GROUP_W, POOL_GROUP_W)),
            whole((1, POOL_WIDTH)),
            whole((D_MODEL, D_MODEL)),
            whole((2 * N_EXPERTS, D_MODEL)),
            whole((N_EXPERTS, 1)),
        ],
        out_specs=[
            pl.BlockSpec((1, tile, D_MODEL), lambda b, i: (b, i, 0)),
            pl.BlockSpec((1, tile, D_MODEL), lambda b, i: (b, i, 0)),
            per_b((POOL_HIST_PAD, POOL_WIDTH)),
            per_b((WINDOW, KV_WIDTH)),
            per_b((WINDOW, KV_WIDTH)),
            route, route, route,
            pl.BlockSpec((1, ROUTE_ROWS, N_EXPERTS), lambda b, i: (b * n_tiles + i, 0, 0)),
        ],
        out_shape=out_shape,
        scratch_shapes=[
            pltpu.VMEM((tile + POOL_HIST_PAD, POOL_WIDTH), F32),
            pltpu.VMEM((hist_keys, KV_WIDTH), F32),
            pltpu.VMEM((hist_keys, KV_WIDTH), F32),
            pltpu.VMEM((N_EXPERTS, LANES), F32),
        ],
        compiler_params=pltpu.CompilerParams(
            dimension_semantics=("arbitrary", "arbitrary"), vmem_limit_bytes=VMEM_LIMIT),
        name="front_cached" if has_cache else "front_prompt",
    )(sinks, x, mod, gains, hist_pool, hist_k, hist_v, cnt0,
      w_in, w_pool, pool_scale, w_out, wr_t, rbias)


def _run_copy(src, dst, s_row, d_row, sem):
    return pltpu.make_async_copy(src.at[pl.ds(s_row, RUN_ROWS)], dst.at[pl.ds(d_row, RUN_ROWS)], sem)


def _for_each_run_chunk(step, grow_ref, lo8_ref, nch_ref, fn):
    def per_expert(e, carry):
        idx = step * N_EXPERTS + e
        local0 = lo8_ref[idx]
        global0 = grow_ref[idx]

        def per_chunk(c, carry2):
            fn(pl.multiple_of(local0 + RUN_ROWS * c, RUN_ROWS), pl.multiple_of(global0 + RUN_ROWS * c, RUN_ROWS))
            return carry2

        lax.fori_loop(0, nch_ref[idx], per_chunk, 0)
        return carry

    lax.fori_loop(0, N_EXPERTS, per_expert, 0)


def _dispatch_body(n_prompt_steps, grow_ref, lo8_ref, nch_ref, nct_ref, zrow_ref, znch_ref, nused_ref,
                   h2p_ref, h2s_ref, slot_ref, wts_ref, xs_out, loc, zrows, sem):
    i = pl.program_id(0)
    n_steps = pl.num_programs(0)
    par = lax.rem(i, 2)
    h2 = jnp.where(i < n_prompt_steps, h2p_ref[...], h2s_ref[...])
    slots = slot_ref[0]
    wts = wts_ref[0]
    row_id = lax.broadcasted_iota(I32, (LOCAL_ROWS, MOE_TILE), 0)
    sel = jnp.zeros((LOCAL_ROWS, MOE_TILE), F32)
    selw = jnp.zeros((LOCAL_ROWS, MOE_TILE), F32)
    for kk in range(TOP_K):
        hit = row_id == slots[kk:kk + 1, :]
        sel = jnp.where(hit, 1.0, sel)
        selw = jnp.where(hit, wts[kk:kk + 1, :], selw)
    sorted_rows = jnp.dot(sel.astype(BF16), h2, preferred_element_type=F32)
    w_slot = jnp.sum(selw, axis=1, keepdims=True)
    buf = loc.at[par]
    buf[:, :D_MODEL] = sorted_rows.astype(BF16)
    w_hi, w_mid, w_lo = _split3(w_slot)
    lane = lax.broadcasted_iota(I32, (LOCAL_ROWS, LANES), 1)
    w_lanes = jnp.where(lane == 0, w_hi.astype(F32),
                        jnp.where(lane == 1, w_mid.astype(F32), jnp.where(lane == 2, w_lo.astype(F32), 0.0)))
    buf[:, D_MODEL:] = w_lanes.astype(BF16)

    def drain(n_chunks):
        def one(c, carry):
            _run_copy(buf, xs_out, 0, 0, sem).wait()
            return carry
        lax.fori_loop(0, n_chunks, one, 0)

    @pl.when(i > 0)
    def _():
        drain(nct_ref[i - 1])

    _for_each_run_chunk(i, grow_ref, lo8_ref, nch_ref,
                        lambda lrow, grow: _run_copy(buf, xs_out, lrow, grow, sem).start())

    @pl.when(i == n_steps - 1)
    def _():
        zrows[...] = jnp.zeros_like(zrows)

        def per_expert(e, total):
            def per_chunk(c, carry):
                _run_copy(zrows, xs_out, 0, pl.multiple_of(zrow_ref[e] + RUN_ROWS * c, RUN_ROWS), sem).start()
                return carry
            lax.fori_loop(0, znch_ref[e], per_chunk, 0)
            return total + znch_ref[e]

        n_zero = lax.fori_loop(0, N_EXPERTS, per_expert, 0)
        drain(nct_ref[i] + n_zero)

        def block_copy(blk):
            return pltpu.make_async_copy(
                zrows, xs_out.at[pl.ds(pl.multiple_of(blk * EXPERT_ROWS, EXPERT_ROWS), EXPERT_ROWS)], sem)

        n_blocks = xs_out.shape[0] // EXPERT_ROWS

        def start_block(blk, carry):
            block_copy(blk).start()
            return carry

        def wait_block(blk, carry):
            block_copy(blk).wait()
            return carry

        lax.fori_loop(nused_ref[0], n_blocks, start_block, 0)
        lax.fori_loop(nused_ref[0], n_blocks, wait_block, 0)


def _dispatch_call(tables, zero_tables, h2_p, h2_s, slot_tiles, wts_tiles, n_rows):
    n_prompt_steps = h2_p.shape[0] // MOE_TILE
    n_steps = n_prompt_steps + h2_s.shape[0] // MOE_TILE
    grid_spec = pltpu.PrefetchScalarGridSpec(
        num_scalar_prefetch=7,
        grid=(n_steps,),
        in_specs=[
            pl.BlockSpec((MOE_TILE, D_MODEL), lambda i, *_: (jnp.minimum(i, n_prompt_steps - 1), 0)),
            pl.BlockSpec((MOE_TILE, D_MODEL), lambda i, *_: (jnp.maximum(i - n_prompt_steps, 0), 0)),
            pl.BlockSpec((1, ROUTE_ROWS, MOE_TILE), lambda i, *_: (i, 0, 0)),
            pl.BlockSpec((1, ROUTE_ROWS, MOE_TILE), lambda i, *_: (i, 0, 0)),
        ],
        out_specs=pl.BlockSpec(memory_space=pl.ANY),
        scratch_shapes=[
            pltpu.VMEM((2, LOCAL_ROWS, ROW_W), BF16),
            pltpu.VMEM((EXPERT_ROWS, ROW_W), BF16),
            pltpu.SemaphoreType.DMA,
        ],
    )
    return pl.pallas_call(
        functools.partial(_dispatch_body, n_prompt_steps),
        grid_spec=grid_spec,
        out_shape=jax.ShapeDtypeStruct((n_rows, ROW_W), BF16),
        compiler_params=pltpu.CompilerParams(
            dimension_semantics=("arbitrary",), vmem_limit_bytes=VMEM_LIMIT),
        name="dispatch",
    )(*tables, *zero_tables, h2_p, h2_s, slot_tiles, wts_tiles)


def _experts_body(be_ref, nu_ref, x_ref, wg_ref, wu_ref, wd_ref, o_ref, wgb, wub, wdb):
    j = pl.program_id(0)
    prev = be_ref[jnp.maximum(j - 1, 0)]

    @pl.when((j == 0) | (be_ref[j] != prev))
    def _():
        wgb[...] = wg_ref[0].astype(BF16)
        wub[...] = wu_ref[0].astype(BF16)
        wdb[...] = wd_ref[0].astype(BF16)

    @pl.when(j < nu_ref[0])
    def _():
        x = x_ref[:, :D_MODEL]
        w_parts = x_ref[:, D_MODEL:].astype(F32)
        w_row = w_parts[:, 0:1] + w_parts[:, 1:2] + w_parts[:, 2:3]
        g = jnp.dot(x, wgb[...], preferred_element_type=F32)
        u = jnp.dot(x, wub[...], preferred_element_type=F32)
        a = (g * _sigmoid(g) * u).astype(BF16)
        y = jnp.dot(a, wdb[...], preferred_element_type=F32) * w_row
        o_ref[...] = y.astype(BF16)

    @pl.when(j >= nu_ref[0])
    def _():
        o_ref[...] = jnp.zeros_like(o_ref)


def _experts_call(block_e, n_used, xs, w_gate, w_up, w_down):
    n_blocks = xs.shape[0] // EXPERT_ROWS
    grid_spec = pltpu.PrefetchScalarGridSpec(
        num_scalar_prefetch=2,
        grid=(n_blocks,),
        in_specs=[
            pl.BlockSpec((EXPERT_ROWS, ROW_W), lambda j, be, nu: (jnp.minimum(j, nu[0] - 1), 0)),
            pl.BlockSpec((1, D_MODEL, D_EXPERT), lambda j, be, nu: (be[j], 0, 0)),
            pl.BlockSpec((1, D_MODEL, D_EXPERT), lambda j, be, nu: (be[j], 0, 0)),
            pl.BlockSpec((1, D_EXPERT, D_MODEL), lambda j, be, nu: (be[j], 0, 0)),
        ],
        out_specs=pl.BlockSpec((EXPERT_ROWS, D_MODEL), lambda j, be, nu: (j, 0)),
        scratch_shapes=[
            pltpu.VMEM((D_MODEL, D_EXPERT), BF16),
            pltpu.VMEM((D_MODEL, D_EXPERT), BF16),
            pltpu.VMEM((D_EXPERT, D_MODEL), BF16),
        ],
    )
    return pl.pallas_call(
        _experts_body,
        grid_spec=grid_spec,
        out_shape=jax.ShapeDtypeStruct((xs.shape[0], D_MODEL), BF16),
        compiler_params=pltpu.CompilerParams(
            dimension_semantics=("arbitrary",), vmem_limit_bytes=VMEM_LIMIT),
        name="experts",
    )(block_e, n_used, xs, w_gate, w_up, w_down)


def _combine_body(n_prompt_steps, grow_ref, lo8_ref, nch_ref, nct_ref,
                  x1p_ref, x1s_ref, h2p_ref, h2s_ref, gate_ref, gains_ref, slot_ref, ys_hbm,
                  wsg_ref, wsu_ref, wsd_ref, outp_ref, outs_ref, gath, sem):
    i = pl.program_id(0)
    n_steps = pl.num_programs(0)
    par = lax.rem(i, 2)

    def fetch(step, slot):
        buf = gath.at[slot]
        _for_each_run_chunk(step, grow_ref, lo8_ref, nch_ref,
                            lambda lrow, grow: _run_copy(ys_hbm, buf, grow, lrow, sem.at[slot]).start())

    @pl.when(i == 0)
    def _():
        gath[...] = jnp.zeros_like(gath)
        fetch(0, 0)

    @pl.when(i + 1 < n_steps)
    def _():
        fetch(i + 1, 1 - par)

    is_prompt = i < n_prompt_steps
    h2 = jnp.where(is_prompt, h2p_ref[...], h2s_ref[...])
    g = jnp.dot(h2, wsg_ref[...], preferred_element_type=F32)
    u = jnp.dot(h2, wsu_ref[...], preferred_element_type=F32)
    a = (g * _sigmoid(g) * u).astype(BF16)
    ffn = jnp.dot(a, wsd_ref[...], preferred_element_type=F32)

    slot_cols = _to_sublanes(slot_ref[0].astype(F32))
    col_id = lax.broadcasted_iota(I32, (MOE_TILE, LOCAL_ROWS), 1).astype(F32)
    take = jnp.zeros((MOE_TILE, LOCAL_ROWS), F32)
    for kk in range(TOP_K):
        take = jnp.where(col_id == slot_cols[:, kk:kk + 1], 1.0, take)

    buf = gath.at[par]

    def one(c, carry):
        _run_copy(ys_hbm, buf, 0, 0, sem.at[par]).wait()
        return carry

    lax.fori_loop(0, nct_ref[i], one, 0)

    ffn = ffn + jnp.dot(take.astype(BF16), buf[...], preferred_element_type=F32)

    quarter = MOE_TILE // gate_ref.shape[0]
    gate = jnp.concatenate(
        [jnp.broadcast_to(gate_ref[q:q + 1, :], (quarter, D_MODEL)) for q in range(gate_ref.shape[0])], axis=0)
    x1 = jnp.where(is_prompt, x1p_ref[...], x1s_ref[...])
    out = x1 + gate * (_rms(ffn) * gains_ref[3:4, :])

    @pl.when(is_prompt)
    def _():
        outp_ref[...] = out

    @pl.when(jnp.logical_not(is_prompt))
    def _():
        outs_ref[...] = out


def _combine_call(tables, x1_p, x1_s, h2_p, h2_s, gate_groups, gains, slot_tiles, ys,
                  ws_gate, ws_up, ws_down):
    n_prompt, n_sample = x1_p.shape[0], x1_s.shape[0]
    n_prompt_steps = n_prompt // MOE_TILE
    n_steps = n_prompt_steps + n_sample // MOE_TILE
    groups_per_tile = gate_groups.shape[0] // n_steps
    tok_p = lambda: pl.BlockSpec((MOE_TILE, D_MODEL), lambda i, *_: (jnp.minimum(i, n_prompt_steps - 1), 0))
    tok_s = lambda: pl.BlockSpec((MOE_TILE, D_MODEL), lambda i, *_: (jnp.maximum(i - n_prompt_steps, 0), 0))
    whole = lambda shape: pl.BlockSpec(shape, lambda i, *_: (0,) * len(shape))
    grid_spec = pltpu.PrefetchScalarGridSpec(
        num_scalar_prefetch=4,
        grid=(n_steps,),
        in_specs=[
            tok_p(), tok_s(), tok_p(), tok_s(),
            pl.BlockSpec((groups_per_tile, D_MODEL), lambda i, *_: (i, 0)),
            whole((4, D_MODEL)),
            pl.BlockSpec((1, ROUTE_ROWS, MOE_TILE), lambda i, *_: (i, 0, 0)),
            pl.BlockSpec(memory_space=pl.ANY),
            whole((D_MODEL, D_SHARED)), whole((D_MODEL, D_SHARED)), whole((D_SHARED, D_MODEL)),
        ],
        out_specs=[tok_p(), tok_s()],
        scratch_shapes=[
            pltpu.VMEM((2, LOCAL_ROWS, D_MODEL), BF16),
            pltpu.SemaphoreType.DMA((2,)),
        ],
    )
    return pl.pallas_call(
        functools.partial(_combine_body, n_prompt_steps),
        grid_spec=grid_spec,
        out_shape=[jax.ShapeDtypeStruct((n_prompt, D_MODEL), F32),
                   jax.ShapeDtypeStruct((n_sample, D_MODEL), F32)],
        compiler_params=pltpu.CompilerParams(
            dimension_semantics=("arbitrary",), vmem_limit_bytes=VMEM_LIMIT),
        name="combine",
    )(*tables, x1_p, x1_s, h2_p, h2_s, gate_groups, gains, slot_tiles, ys, ws_gate, ws_up, ws_down)


def _route_rows(route):
    return jnp.transpose(route, (1, 0, 2)).reshape(ROUTE_ROWS, -1)


def _moe_tiles(rows):
    return jnp.transpose(rows.reshape(ROUTE_ROWS, -1, MOE_TILE), (1, 0, 2))


def kernel(x_prompt, x_sample, c_prompt, c_sample, state_pool, cache_k, cache_v, w_ada, b_ada, norm_gains,
           w_in, w_pool, pool_scale, attn_sinks, w_out, w_router, router_bias, w_gate, w_up, w_down,
           ws_gate, ws_up, ws_down):
    assert w_ada.shape[0] == 1, "single-layer kernel"
    bsz, seq, _ = x_prompt.shape
    dbsz, dseq, _ = x_sample.shape
    n_prompt, n_sample = bsz * seq, dbsz * dseq
    n_tok = n_prompt + n_sample
    assert dseq == CHUNK and seq % MOE_TILE == 0 and n_sample % MOE_TILE == 0

    w_in0, w_out0 = w_in[0], w_out[0]
    wq = w_in0[:, POOL_WIDTH:POOL_WIDTH + ATTN_WIDTH].reshape(D_MODEL, N_KV_HEADS, GQ, HEAD_DIM)
    wq = jnp.transpose(wq, (0, 2, 1, 3)).reshape(D_MODEL, ATTN_WIDTH)
    w_in_p = jnp.concatenate(
        [w_in0[:, :POOL_WIDTH], wq, w_in0[:, POOL_WIDTH + ATTN_WIDTH:]], axis=1).astype(BF16)
    wo = w_out0[POOL_WIDTH:].reshape(N_KV_HEADS, GQ, HEAD_DIM, D_MODEL)
    wo = jnp.transpose(wo, (1, 0, 2, 3)).reshape(ATTN_WIDTH, D_MODEL)
    w_out_p = jnp.concatenate([w_out0[:POOL_WIDTH], wo], axis=0).astype(BF16)
    w_pool_b = w_pool[0].astype(BF16)
    pscale = pool_scale[0].reshape(1, POOL_WIDTH)
    wr_t = w_router[0].T
    wr_hi = wr_t.astype(BF16)
    wr_lo = (wr_t - wr_hi.astype(F32)).astype(BF16)
    wr_split = jnp.concatenate([wr_hi, wr_lo], axis=0)
    rbias = router_bias[0].reshape(N_EXPERTS, 1)
    gains = norm_gains[0]
    sinks = attn_sinks[0]

    mod = _ada_call(jnp.concatenate([c_prompt, c_sample], axis=0), w_ada[0], b_ada[0])
    mod = mod.reshape(bsz + dbsz, 6, D_MODEL)

    zeros_pool = jnp.zeros((bsz, POOL_HIST_PAD, POOL_WIDTH), F32)
    zeros_kv = jnp.zeros((bsz, WINDOW, KV_WIDTH), F32)
    cnt0 = jnp.zeros((N_EXPERTS, LANES), F32)
    (x1_p, h2_p, pool_p, nk_p, nv_p, eidx_p, wts_p, rank_p, tcnt_p) = _front_call(
        x_prompt, mod[:bsz], gains, zeros_pool, zeros_kv, zeros_kv, cnt0, sinks,
        w_in_p, w_pool_b, pscale, w_out_p, wr_split, rbias, tile=MOE_TILE, has_cache=False)
    cnt_prompt = jnp.broadcast_to(jnp.sum(tcnt_p[:, 0, :], axis=0)[:, None], (N_EXPERTS, LANES))

    hist_pad = ATTN_KEYS - CHUNK - WINDOW
    pool_s0 = jnp.pad(state_pool[0], ((0, 0), (POOL_HIST_PAD - state_pool.shape[2], 0), (0, 0)))
    ck = jnp.pad(cache_k[0].reshape(dbsz, WINDOW, KV_WIDTH), ((0, 0), (hist_pad, 0), (0, 0)))
    cv = jnp.pad(cache_v[0].reshape(dbsz, WINDOW, KV_WIDTH), ((0, 0), (hist_pad, 0), (0, 0)))
    (x1_s, h2_s, pool_s, nk_s, nv_s, eidx_s, wts_s, rank_s, tcnt_s) = _front_call(
        x_sample, mod[bsz:], gains, pool_s0, ck, cv, cnt_prompt, sinks,
        w_in_p, w_pool_b, pscale, w_out_p, wr_split, rbias, tile=CHUNK, has_cache=True)

    eidx = jnp.concatenate([_route_rows(eidx_p), _route_rows(eidx_s)], axis=1)
    rank = jnp.concatenate([_route_rows(rank_p), _route_rows(rank_s)], axis=1)
    wts = jnp.concatenate([_route_rows(wts_p), _route_rows(wts_s)], axis=1)
    n_steps = n_tok // MOE_TILE
    cnt_td = jnp.concatenate(
        [tcnt_p[:, 0, :], tcnt_s[:, 0, :].reshape(-1, MOE_TILE // CHUNK, N_EXPERTS).sum(axis=1)],
        axis=0).astype(I32)
    c8 = (cnt_td + RUN_ROWS - 1) // RUN_ROWS * RUN_ROWS
    rows8 = jnp.sum(c8, axis=0)
    padded = (rows8 + EXPERT_ROWS - 1) // EXPERT_ROWS * EXPERT_ROWS
    pends = jnp.cumsum(padded)
    pstarts = pends - padded
    base = jnp.cumsum(cnt_td, axis=0) - cnt_td
    grow = pstarts[None, :] + jnp.cumsum(c8, axis=0) - c8
    lo8 = jnp.cumsum(c8, axis=1) - c8
    nch = c8 // RUN_ROWS
    nct = jnp.sum(nch, axis=1)
    tile_of = jnp.arange(n_tok, dtype=I32) // MOE_TILE
    slot = rank + (lo8 - base).reshape(-1)[tile_of[None, :] * N_EXPERTS + eidx]
    tables = (grow.reshape(-1).astype(I32), lo8.reshape(-1).astype(I32),
              nch.reshape(-1).astype(I32), nct.astype(I32))
    n_blocks = -(-(n_tok * TOP_K + n_steps * N_EXPERTS * (RUN_ROWS - 1) + N_EXPERTS * (EXPERT_ROWS - 1))
                 // EXPERT_ROWS)
    n_used = (pends[-1] // EXPERT_ROWS).astype(I32).reshape(1)
    zero_tables = ((pstarts + rows8).astype(I32), ((padded - rows8) // RUN_ROWS).astype(I32), n_used)
    block_start = jnp.arange(n_blocks, dtype=I32) * EXPERT_ROWS
    block_start = jnp.minimum(block_start, pends[-1] - EXPERT_ROWS)
    block_e = jnp.minimum(jnp.sum(block_start[:, None] >= pends[None, :], axis=-1), N_EXPERTS - 1).astype(I32)

    h2_pf, h2_sf = h2_p.reshape(n_prompt, D_MODEL), h2_s.reshape(n_sample, D_MODEL)
    slot_tiles = _moe_tiles(slot)
    xs = _dispatch_call(tables, zero_tables, h2_pf, h2_sf, slot_tiles, _moe_tiles(wts),
                        n_blocks * EXPERT_ROWS)
    ys = _experts_call(block_e, n_used, xs, w_gate[0], w_up[0], w_down[0])
    gate_groups = jnp.concatenate(
        [jnp.repeat(mod[:bsz, 5], seq // GATE_GROUP, axis=0),
         jnp.repeat(mod[bsz:, 5], dseq // GATE_GROUP, axis=0)], axis=0)
    y_p, y_s = _combine_call(
        tables, x1_p.reshape(n_prompt, D_MODEL), x1_s.reshape(n_sample, D_MODEL), h2_pf, h2_sf,
        gate_groups, gains, slot_tiles, ys,
        ws_gate[0].astype(BF16), ws_up[0].astype(BF16), ws_down[0].astype(BF16))

    n_hist = state_pool.shape[2]
    kv_shape = (1, -1, WINDOW, N_KV_HEADS, HEAD_DIM)
    return (y_p.reshape(bsz, seq, D_MODEL), y_s.reshape(dbsz, dseq, D_MODEL),
            pool_p[None, :, POOL_HIST_PAD - n_hist:], nk_p.reshape(kv_shape), nv_p.reshape(kv_shape),
            pool_s[None, :, POOL_HIST_PAD - n_hist:], nk_s.reshape(kv_shape), nv_s.reshape(kv_shape))
```

```python
import functools

import jax
import jax.numpy as jnp
import numpy as np
from jax import lax
from jax.experimental import pallas as pl
from jax.experimental.pallas import tpu as pltpu

F32 = jnp.float32
BF16 = jnp.bfloat16
I32 = jnp.int32

D_MODEL = 1024
CHUNK = 64
POOL_WIDTH = 512
POOL_WINDOWS = (2, 4, 8, 16)
POOL_GROUP_W = 128
POOL_HIST_PAD = 16
HEAD_DIM = 64
N_HEADS = 8
N_KV_HEADS = 2
GQ = N_HEADS // N_KV_HEADS
ATTN_WIDTH = N_HEADS * HEAD_DIM
KV_WIDTH = N_KV_HEADS * HEAD_DIM
IN_WIDTH = POOL_WIDTH + ATTN_WIDTH + 2 * KV_WIDTH
WINDOW = 128
N_EXPERTS = 64
TOP_K = 6
N_EXPERT_GROUPS = 8
GROUP_SIZE = N_EXPERTS // N_EXPERT_GROUPS
TOPK_GROUPS = 4
D_EXPERT = 256
D_SHARED = 256
ROUTED_SCALE = 2.5
NORM_EPS = 1e-6
NEG_BIG = -1e30

LANES = 128
SUBLANES = 8
ATTN_KEYS = 2 * LANES
ROUTE_ROWS = 8
EXPERT_ROWS = 256
MOE_TILE = 256
GATE_GROUP = MOE_TILE // SUBLANES
RUN_ROWS = 2 * SUBLANES
LOCAL_ROWS = 2560
ROW_W = D_MODEL + LANES
VMEM_LIMIT = 56 * 1024 * 1024

assert TOP_K * MOE_TILE + N_EXPERTS * (RUN_ROWS - 1) <= LOCAL_ROWS

_NT = (((1,), (1,)), ((), ()))


def _rms(v):
    return v * lax.rsqrt(jnp.mean(v * v, axis=-1, keepdims=True) + NORM_EPS)


def _sigmoid(v):
    return 1.0 / (1.0 + jnp.exp(-v))


def _split3(v):
    hi = v.astype(BF16)
    r1 = v - hi.astype(F32)
    mid = r1.astype(BF16)
    lo = (r1 - mid.astype(F32)).astype(BF16)
    return hi, mid, lo


def _to_sublanes(rows):
    n = rows.shape[1]
    hi, mid, lo = _split3(rows)
    eye = (lax.broadcasted_iota(I32, (n, n), 0) == lax.broadcasted_iota(I32, (n, n), 1)).astype(BF16)
    return (lax.dot_general(eye, hi, _NT, preferred_element_type=F32)
            + lax.dot_general(eye, mid, _NT, preferred_element_type=F32)
            + lax.dot_general(eye, lo, _NT, preferred_element_type=F32))


def _ada_body(c_ref, w_ref, b_ref, o_ref):
    c = c_ref[...]
    s = c * _sigmoid(c)
    o_ref[...] = jnp.dot(s, w_ref[...], preferred_element_type=F32) + b_ref[...]


def _ada_call(c_all, w_ada, b_ada):
    nb = c_all.shape[0]
    n_out = w_ada.shape[1]
    tile = D_MODEL
    return pl.pallas_call(
        _ada_body,
        grid=(n_out // tile,),
        in_specs=[
            pl.BlockSpec((nb, D_MODEL), lambda j: (0, 0)),
            pl.BlockSpec((D_MODEL, tile), lambda j: (0, j)),
            pl.BlockSpec((1, tile), lambda j: (0, j)),
        ],
        out_specs=pl.BlockSpec((nb, tile), lambda j: (0, j)),
        out_shape=jax.ShapeDtypeStruct((nb, n_out), F32),
        compiler_params=pltpu.CompilerParams(dimension_semantics=("arbitrary",)),
        name="ada",
    )(c_all, w_ada, b_ada.reshape(1, n_out))


def _front_body(tile, has_cache,
                sinks_ref, x_ref, mod_ref, gains_ref, hp_ref, hk_ref, hv_ref, cnt0_ref,
                win_ref, wpool_ref, pscale_ref, wout_ref, wrt_ref, rbias_ref,
                x1_ref, h2_ref, npool_ref, nk_ref, nv_ref, eidx_ref, wts_ref, rank_ref, tcnt_ref,
                ubuf, khist, vhist, run):
    b = pl.program_id(0)
    i = pl.program_id(1)
    sub_q = min(tile, LANES)
    n_sub = tile // sub_q
    hist_keys = ATTN_KEYS - sub_q
    n_keys = hist_keys + tile

    @pl.when(i == 0)
    def _():
        ubuf[0:POOL_HIST_PAD, :] = hp_ref[0]
        khist[...] = hk_ref[0]
        vhist[...] = hv_ref[0]

    @pl.when((b == 0) & (i == 0))
    def _():
        run[...] = cnt0_ref[...]

    mod = mod_ref[0]
    gains = gains_ref[...]
    xt = x_ref[0]

    h = _rms(xt) * gains[0:1] * (1.0 + mod[1:2]) + mod[0:1]
    proj = jnp.dot(h.astype(BF16), win_ref[...], preferred_element_type=F32)
    u = proj[:, :POOL_WIDTH]
    o_k = POOL_WIDTH + ATTN_WIDTH
    k_new = proj[:, o_k:o_k + KV_WIDTH]
    v_new = proj[:, o_k + KV_WIDTH:]

    kw = jnp.concatenate([khist[...], k_new], axis=0)
    vw = jnp.concatenate([vhist[...], v_new], axis=0)
    kwb = kw.astype(BF16)
    vwb = vw.astype(BF16)
    qrow = lax.broadcasted_iota(I32, (sub_q, ATTN_KEYS), 0)
    kpos = lax.broadcasted_iota(I32, (sub_q, ATTN_KEYS), 1) - hist_keys
    qchunk = jnp.right_shift(qrow, 6)
    vis_band = (kpos >= CHUNK * (qchunk - 2)) & (kpos < CHUNK * (qchunk + 1))
    side0 = lax.broadcasted_iota(I32, (sub_q, LANES), 1) < HEAD_DIM
    attn_rows = []
    for r in range(n_sub):
        vis = vis_band
        if not has_cache:
            vis = vis & (kpos + (i * tile + r * sub_q) >= 0)
        kb = kwb[r * sub_q:r * sub_q + ATTN_KEYS]
        vb = vwb[r * sub_q:r * sub_q + ATTN_KEYS]
        blocks = []
        for j in range(GQ):
            qp = proj[r * sub_q:(r + 1) * sub_q, POOL_WIDTH + LANES * j:POOL_WIDTH + LANES * (j + 1)]
            outs = []
            for s in range(N_KV_HEADS):
                keep = side0 if s == 0 else jnp.logical_not(side0)
                qm = jnp.where(keep, qp, 0.0).astype(BF16)
                sc = lax.dot_general(qm, kb, _NT, preferred_element_type=F32) * (HEAD_DIM ** -0.5)
                sc = jnp.where(vis, sc, NEG_BIG)
                sink = sinks_ref[j + GQ * s]
                m = jnp.maximum(jnp.max(sc, axis=-1, keepdims=True), sink)
                p = jnp.exp(sc - m)
                den = jnp.sum(p, axis=-1, keepdims=True) + jnp.exp(sink - m)
                p = (p / den).astype(BF16)
                outs.append(jnp.dot(p, vb, preferred_element_type=F32))
            blocks.append(jnp.where(side0, outs[0], outs[1]))
        attn_rows.append(jnp.concatenate(blocks, axis=-1))
    attn = attn_rows[0] if n_sub == 1 else jnp.concatenate(attn_rows, axis=0)

    ubuf[POOL_HIST_PAD:POOL_HIST_PAD + tile, :] = u
    if has_cache:
        seen = None
    else:
        seen = (lax.broadcasted_iota(I32, (tile, 1), 0) + i * tile + 1).astype(F32)
    pool_blocks = []
    for g, w in enumerate(POOL_WINDOWS):
        cols = slice(POOL_GROUP_W * g, POOL_GROUP_W * (g + 1))
        acc = u[:, cols]
        for s in range(1, w):
            acc = acc + ubuf[POOL_HIST_PAD - s:POOL_HIST_PAD - s + tile, cols]
        cnt = float(w) if seen is None else jnp.minimum(seen, float(w))
        dlt = acc / cnt - u[:, cols]
        pool_blocks.append(jnp.dot(dlt.astype(BF16), wpool_ref[g], preferred_element_type=F32))
    pool = jnp.concatenate(pool_blocks, axis=-1) * pscale_ref[...]

    mixin = jnp.concatenate([pool, attn], axis=-1).astype(BF16)
    mix = jnp.dot(mixin, wout_ref[...], preferred_element_type=F32)
    x1 = xt + mod[2:3] * (_rms(mix) * gains[1:2])
    x1_ref[0] = x1

    new_hist = ubuf[tile:tile + POOL_HIST_PAD, :]
    ubuf[0:POOL_HIST_PAD, :] = new_hist
    npool_ref[0] = new_hist
    khist[...] = kw[tile:, :]
    vhist[...] = vw[tile:, :]
    nk_ref[0] = kw[n_keys - WINDOW:, :]
    nv_ref[0] = vw[n_keys - WINDOW:, :]

    h2f = _rms(x1) * gains[2:3] * (1.0 + mod[4:5]) + mod[3:4]
    h2hi = h2f.astype(BF16)
    h2_ref[0] = h2hi
    h2lo = (h2f - h2hi.astype(F32)).astype(BF16)
    wrt = wrt_ref[...]
    part = lax.dot_general(wrt, h2hi, _NT, preferred_element_type=F32)
    logits = (part[:N_EXPERTS] + part[N_EXPERTS:]
              + lax.dot_general(wrt[:N_EXPERTS], h2lo, _NT, preferred_element_type=F32))
    scores = _sigmoid(logits)
    sel = scores + rbias_ref[...]

    sub_g = lax.broadcasted_iota(I32, (GROUP_SIZE, tile), 0).astype(F32)
    gscore = jnp.zeros((N_EXPERT_GROUPS, tile), F32)
    for gi in range(N_EXPERT_GROUPS):
        blk = sel[GROUP_SIZE * gi:GROUP_SIZE * (gi + 1), :]
        m1 = jnp.max(blk, axis=0, keepdims=True)
        i1 = jnp.min(jnp.where(blk == m1, sub_g, float(GROUP_SIZE)), axis=0, keepdims=True)
        m2 = jnp.max(jnp.where(sub_g == i1, -jnp.inf, blk), axis=0, keepdims=True)
        gscore = jnp.where(sub_g == gi, m1 + m2, gscore)
    chosen = jnp.zeros((N_EXPERT_GROUPS, tile), F32)
    for _ in range(TOPK_GROUPS):
        m = jnp.max(gscore, axis=0, keepdims=True)
        idx = jnp.min(jnp.where(gscore == m, sub_g, float(N_EXPERT_GROUPS)), axis=0, keepdims=True)
        pick = sub_g == idx
        chosen = jnp.where(pick, 1.0, chosen)
        gscore = jnp.where(pick, -jnp.inf, gscore)
    emask = jnp.concatenate(
        [jnp.broadcast_to(chosen[gi:gi + 1, :], (GROUP_SIZE, tile)) for gi in range(N_EXPERT_GROUPS)], axis=0)
    selm = jnp.where(emask > 0.0, sel, -jnp.inf)

    sub_e = lax.broadcasted_iota(I32, (N_EXPERTS, tile), 0).astype(F32)
    picks, idxs, raw_w = [], [], []
    for _ in range(TOP_K):
        m = jnp.max(selm, axis=0, keepdims=True)
        idx = jnp.min(jnp.where(selm == m, sub_e, float(N_EXPERTS)), axis=0, keepdims=True)
        pick = sub_e == idx
        raw_w.append(jnp.sum(jnp.where(pick, scores, 0.0), axis=0, keepdims=True))
        selm = jnp.where(pick, -jnp.inf, selm)
        picks.append(pick)
        idxs.append(idx)
    wsum = raw_w[0]
    for kk in range(1, TOP_K):
        wsum = wsum + raw_w[kk]

    onehot = jnp.zeros((N_EXPERTS, tile), F32)
    for kk in range(TOP_K):
        onehot = jnp.where(picks[kk], 1.0, onehot)
    onehot_b = onehot.astype(BF16)
    tri = (lax.broadcasted_iota(I32, (tile, tile), 0) < lax.broadcasted_iota(I32, (tile, tile), 1)).astype(BF16)
    before = jnp.dot(onehot_b, tri, preferred_element_type=F32) + run[:, 0:1]
    sub_r = lax.broadcasted_iota(I32, (ROUTE_ROWS, tile), 0)
    eidx_o = jnp.zeros((ROUTE_ROWS, tile), I32)
    wts_o = jnp.zeros((ROUTE_ROWS, tile), F32)
    rank_o = jnp.zeros((ROUTE_ROWS, tile), I32)
    for kk in range(TOP_K):
        rk = jnp.sum(jnp.where(picks[kk], before, 0.0), axis=0, keepdims=True).astype(I32)
        eidx_o = jnp.where(sub_r == kk, idxs[kk].astype(I32), eidx_o)
        wts_o = jnp.where(sub_r == kk, raw_w[kk] / wsum * ROUTED_SCALE, wts_o)
        rank_o = jnp.where(sub_r == kk, rk, rank_o)
    eidx_ref[0] = eidx_o
    wts_ref[0] = wts_o
    rank_ref[0] = rank_o
    tcnt_ref[0] = lax.dot_general(jnp.ones((ROUTE_ROWS, tile), BF16), onehot_b, _NT,
                                  preferred_element_type=F32)
    run[...] = run[...] + jnp.sum(onehot, axis=1, keepdims=True)


def _front_call(x, mod, gains, hist_pool, hist_k, hist_v, cnt0, sinks,
                w_in, w_pool, pool_scale, w_out, wr_t, rbias, *, tile, has_cache):
    bsz, seq, _ = x.shape
    n_tiles = seq // tile
    hist_keys = hist_k.shape[1]
    assert hist_keys == ATTN_KEYS - min(tile, LANES)
    body = functools.partial(_front_body, tile, has_cache)
    whole = lambda shape: pl.BlockSpec(shape, lambda b, i: (0,) * len(shape))
    per_b = lambda shape: pl.BlockSpec((1,) + shape, lambda b, i: (b,) + (0,) * len(shape))
    route = pl.BlockSpec((1, ROUTE_ROWS, tile), lambda b, i: (b * n_tiles + i, 0, 0))
    out_shape = [
        jax.ShapeDtypeStruct((bsz, seq, D_MODEL), F32),
        jax.ShapeDtypeStruct((bsz, seq, D_MODEL), BF16),
        jax.ShapeDtypeStruct((bsz, POOL_HIST_PAD, POOL_WIDTH), F32),
        jax.ShapeDtypeStruct((bsz, WINDOW, KV_WIDTH), F32),
        jax.ShapeDtypeStruct((bsz, WINDOW, KV_WIDTH), F32),
        jax.ShapeDtypeStruct((bsz * n_tiles, ROUTE_ROWS, tile), I32),
        jax.ShapeDtypeStruct((bsz * n_tiles, ROUTE_ROWS, tile), F32),
        jax.ShapeDtypeStruct((bsz * n_tiles, ROUTE_ROWS, tile), I32),
        jax.ShapeDtypeStruct((bsz * n_tiles, ROUTE_ROWS, N_EXPERTS), F32),
    ]
    return pl.pallas_call(
        body,
        grid=(bsz, n_tiles),
        in_specs=[
            pl.BlockSpec(memory_space=pltpu.SMEM),
            pl.BlockSpec((1, tile, D_MODEL), lambda b, i: (b, i, 0)),
            per_b((6, D_MODEL)),
            whole((4, D_MODEL)),
            per_b((POOL_HIST_PAD, POOL_WIDTH)),
            per_b((hist_keys, KV_WIDTH)),
            per_b((hist_keys, KV_WIDTH)),
            whole((N_EXPERTS, LANES)),
            whole((D_MODEL, IN_WIDTH)),
            whole((len(POOL_WINDOWS), POOL_GROUP_W, POOL_GROUP_W)),
            whole((1, POOL_WIDTH)),
            whole((D_MODEL, D_MODEL)),
            whole((2 * N_EXPERTS, D_MODEL)),
            whole((N_EXPERTS, 1)),
        ],
        out_specs=[
            pl.BlockSpec((1, tile, D_MODEL), lambda b, i: (b, i, 0)),
            pl.BlockSpec((1, tile, D_MODEL), lambda b, i: (b, i, 0)),
            per_b((POOL_HIST_PAD, POOL_WIDTH)),
            per_b((WINDOW, KV_WIDTH)),
            per_b((WINDOW, KV_WIDTH)),
            route, route, route,
            pl.BlockSpec((1, ROUTE_ROWS, N_EXPERTS), lambda b, i: (b * n_tiles + i, 0, 0)),
        ],
        out_shape=out_shape,
        scratch_shapes=[
            pltpu.VMEM((tile + POOL_HIST_PAD, POOL_WIDTH), F32),
            pltpu.VMEM((hist_keys, KV_WIDTH), F32),
            pltpu.VMEM((hist_keys, KV_WIDTH), F32),
            pltpu.VMEM((N_EXPERTS, LANES), F32),
        ],
        compiler_params=pltpu.CompilerParams(
            dimension_semantics=("arbitrary", "arbitrary"), vmem_limit_bytes=VMEM_LIMIT),
        name="front_cached" if has_cache else "front_prompt",
    )(sinks, x, mod, gains, hist_pool, hist_k, hist_v, cnt0,
      w_in, w_pool, pool_scale, w_out, wr_t, rbias)


def _run_copy(src, dst, s_row, d_row, sem):
    return pltpu.make_async_copy(src.at[pl.ds(s_row, RUN_ROWS)], dst.at[pl.ds(d_row, RUN_ROWS)], sem)


def _for_each_run_chunk(step, grow_ref, lo8_ref, nch_ref, fn):
    def per_expert(e, carry):
        idx = step * N_EXPERTS + e
        local0 = lo8_ref[idx]
        global0 = grow_ref[idx]

        def per_chunk(c, carry2):
            fn(pl.multiple_of(local0 + RUN_ROWS * c, RUN_ROWS), pl.multiple_of(global0 + RUN_ROWS * c, RUN_ROWS))
            return carry2

        lax.fori_loop(0, nch_ref[idx], per_chunk, 0)
        return carry

    lax.fori_loop(0, N_EXPERTS, per_expert, 0)


def _dispatch_body(n_prompt_steps, grow_ref, lo8_ref, nch_ref, nct_ref, zrow_ref, znch_ref, nused_ref,
                   h2p_ref, h2s_ref, eidx_ref, rank_ref, off_ref, wts_ref, xs_out, slot_out, loc, zrows, sem):
    i = pl.program_id(0)
    n_steps = pl.num_programs(0)
    par = lax.rem(i, 2)
    h2 = jnp.where(i < n_prompt_steps, h2p_ref[...], h2s_ref[...])
    eidx = eidx_ref[0]
    rank = rank_ref[0]
    wts = wts_ref[0]
    off = off_ref[0]
    expert_id = lax.broadcasted_iota(I32, (N_EXPERTS, MOE_TILE), 0)
    sub_r = lax.broadcasted_iota(I32, (ROUTE_ROWS, MOE_TILE), 0)
    slots = []
    slot_o = jnp.zeros((ROUTE_ROWS, MOE_TILE), I32)
    for kk in range(TOP_K):
        mine = jnp.sum(jnp.where(expert_id == eidx[kk:kk + 1, :], off, 0.0), axis=0, keepdims=True)
        slots.append(rank[kk:kk + 1, :] + mine.astype(I32))
        slot_o = jnp.where(sub_r == kk, slots[kk], slot_o)
    slot_out[0] = slot_o
    row_id = lax.broadcasted_iota(I32, (LOCAL_ROWS, MOE_TILE), 0)
    sel = jnp.zeros((LOCAL_ROWS, MOE_TILE), F32)
    selw = jnp.zeros((LOCAL_ROWS, MOE_TILE), F32)
    for kk in range(TOP_K):
        hit = row_id == slots[kk]
        sel = jnp.where(hit, 1.0, sel)
        selw = jnp.where(hit, wts[kk:kk + 1, :], selw)
    sorted_rows = jnp.dot(sel.astype(BF16), h2, preferred_element_type=F32)
    w_slot = jnp.sum(selw, axis=1, keepdims=True)
    buf = loc.at[par]
    buf[:, :D_MODEL] = sorted_rows.astype(BF16)
    w_hi, w_mid, w_lo = _split3(w_slot)
    lane = lax.broadcasted_iota(I32, (LOCAL_ROWS, LANES), 1)
    w_lanes = jnp.where(lane == 0, w_hi.astype(F32),
                        jnp.where(lane == 1, w_mid.astype(F32), jnp.where(lane == 2, w_lo.astype(F32), 0.0)))
    buf[:, D_MODEL:] = w_lanes.astype(BF16)

    def drain(n_chunks):
        def one(c, carry):
            _run_copy(buf, xs_out, 0, 0, sem).wait()
            return carry
        lax.fori_loop(0, n_chunks, one, 0)

    @pl.when(i > 0)
    def _():
        drain(nct_ref[i - 1])

    _for_each_run_chunk(i, grow_ref, lo8_ref, nch_ref,
                        lambda lrow, grow: _run_copy(buf, xs_out, lrow, grow, sem).start())

    @pl.when(i == n_steps - 1)
    def _():
        zrows[...] = jnp.zeros_like(zrows)

        def per_expert(e, total):
            def per_chunk(c, carry):
                _run_copy(zrows, xs_out, 0, pl.multiple_of(zrow_ref[e] + RUN_ROWS * c, RUN_ROWS), sem).start()
                return carry
            lax.fori_loop(0, znch_ref[e], per_chunk, 0)
            return total + znch_ref[e]

        n_zero = lax.fori_loop(0, N_EXPERTS, per_expert, 0)
        drain(nct_ref[i] + n_zero)

        def block_copy(blk):
            return pltpu.make_async_copy(
                zrows, xs_out.at[pl.ds(pl.multiple_of(blk * EXPERT_ROWS, EXPERT_ROWS), EXPERT_ROWS)], sem)

        n_blocks = xs_out.shape[0] // EXPERT_ROWS

        def start_block(blk, carry):
            block_copy(blk).start()
            return carry

        def wait_block(blk, carry):
            block_copy(blk).wait()
            return carry

        lax.fori_loop(nused_ref[0], n_blocks, start_block, 0)
        lax.fori_loop(nused_ref[0], n_blocks, wait_block, 0)


def _dispatch_call(tables, zero_tables, h2_p, h2_s, eidx_tiles, rank_tiles, off_tiles, wts_tiles, n_rows):
    n_prompt_steps = h2_p.shape[0] // MOE_TILE
    n_steps = n_prompt_steps + h2_s.shape[0] // MOE_TILE
    route = lambda: pl.BlockSpec((1, ROUTE_ROWS, MOE_TILE), lambda i, *_: (i, 0, 0))
    grid_spec = pltpu.PrefetchScalarGridSpec(
        num_scalar_prefetch=7,
        grid=(n_steps,),
        in_specs=[
            pl.BlockSpec((MOE_TILE, D_MODEL), lambda i, *_: (jnp.minimum(i, n_prompt_steps - 1), 0)),
            pl.BlockSpec((MOE_TILE, D_MODEL), lambda i, *_: (jnp.maximum(i - n_prompt_steps, 0), 0)),
            route(), route(),
            pl.BlockSpec((1, N_EXPERTS, 1), lambda i, *_: (i, 0, 0)),
            route(),
        ],
        out_specs=[pl.BlockSpec(memory_space=pl.ANY), route()],
        scratch_shapes=[
            pltpu.VMEM((2, LOCAL_ROWS, ROW_W), BF16),
            pltpu.VMEM((EXPERT_ROWS, ROW_W), BF16),
            pltpu.SemaphoreType.DMA,
        ],
    )
    return pl.pallas_call(
        functools.partial(_dispatch_body, n_prompt_steps),
        grid_spec=grid_spec,
        out_shape=[jax.ShapeDtypeStruct((n_rows, ROW_W), BF16),
                   jax.ShapeDtypeStruct((n_steps, ROUTE_ROWS, MOE_TILE), I32)],
        compiler_params=pltpu.CompilerParams(
            dimension_semantics=("arbitrary",), vmem_limit_bytes=VMEM_LIMIT),
        name="dispatch",
    )(*tables, *zero_tables, h2_p, h2_s, eidx_tiles, rank_tiles, off_tiles, wts_tiles)


def _block_rows(ref, blk):
    return ref.at[pl.ds(pl.multiple_of(blk * EXPERT_ROWS, EXPERT_ROWS), EXPERT_ROWS)]


def _experts_body(first_ref, nblk_ref, nused_ref, xs_hbm, wg_ref, wu_ref, wd_ref, ys_hbm,
                  xbuf, ybuf, wgb, wub, wdb, isem, osem):
    e = pl.program_id(0)
    first = first_ref[e]
    n_blk = nblk_ref[e]

    def in_copy(blk, slot):
        return pltpu.make_async_copy(_block_rows(xs_hbm, first + blk), xbuf.at[slot], isem.at[slot])

    def out_copy(blk, slot):
        return pltpu.make_async_copy(ybuf.at[slot], _block_rows(ys_hbm, first + blk), osem.at[slot])

    @pl.when(n_blk > 0)
    def _():
        in_copy(0, 0).start()
        wgb[...] = wg_ref[0].astype(BF16)
        wub[...] = wu_ref[0].astype(BF16)
        wdb[...] = wd_ref[0].astype(BF16)

        def one_block(blk, carry):
            slot = lax.rem(blk, 2)

            @pl.when(blk + 1 < n_blk)
            def _():
                in_copy(blk + 1, 1 - slot).start()

            in_copy(blk, slot).wait()

            @pl.when(blk >= 2)
            def _():
                out_copy(blk - 2, slot).wait()

            xrow = xbuf[slot]
            x = xrow[:, :D_MODEL]
            w_parts = xrow[:, D_MODEL:].astype(F32)
            w_row = w_parts[:, 0:1] + w_parts[:, 1:2] + w_parts[:, 2:3]
            g = jnp.dot(x, wgb[...], preferred_element_type=F32)
            u = jnp.dot(x, wub[...], preferred_element_type=F32)
            a = (g * _sigmoid(g) * u).astype(BF16)
            y = jnp.dot(a, wdb[...], preferred_element_type=F32) * w_row
            ybuf[slot] = y.astype(BF16)
            out_copy(blk, slot).start()
            return carry

        lax.fori_loop(0, n_blk, one_block, 0)

        @pl.when(n_blk >= 2)
        def _():
            out_copy(n_blk - 2, lax.rem(n_blk, 2)).wait()
        out_copy(n_blk - 1, lax.rem(n_blk - 1, 2)).wait()

    @pl.when(e == pl.num_programs(0) - 1)
    def _():
        ybuf[0] = jnp.zeros((EXPERT_ROWS, D_MODEL), BF16)
        n_blocks = ys_hbm.shape[0] // EXPERT_ROWS

        def tail_copy(blk):
            return pltpu.make_async_copy(ybuf.at[0], _block_rows(ys_hbm, blk), osem.at[0])

        def start_block(blk, carry):
            tail_copy(blk).start()
            return carry

        def wait_block(blk, carry):
            tail_copy(blk).wait()
            return carry

        lax.fori_loop(nused_ref[0], n_blocks, start_block, 0)
        lax.fori_loop(nused_ref[0], n_blocks, wait_block, 0)


def _experts_call(first_block, n_expert_blocks, n_used, xs, w_gate, w_up, w_down):
    grid_spec = pltpu.PrefetchScalarGridSpec(
        num_scalar_prefetch=3,
        grid=(N_EXPERTS,),
        in_specs=[
            pl.BlockSpec(memory_space=pl.ANY),
            pl.BlockSpec((1, D_MODEL, D_EXPERT), lambda e, *_: (e, 0, 0)),
            pl.BlockSpec((1, D_MODEL, D_EXPERT), lambda e, *_: (e, 0, 0)),
            pl.BlockSpec((1, D_EXPERT, D_MODEL), lambda e, *_: (e, 0, 0)),
        ],
        out_specs=pl.BlockSpec(memory_space=pl.ANY),
        scratch_shapes=[
            pltpu.VMEM((2, EXPERT_ROWS, ROW_W), BF16),
            pltpu.VMEM((2, EXPERT_ROWS, D_MODEL), BF16),
            pltpu.VMEM((D_MODEL, D_EXPERT), BF16),
            pltpu.VMEM((D_MODEL, D_EXPERT), BF16),
            pltpu.VMEM((D_EXPERT, D_MODEL), BF16),
            pltpu.SemaphoreType.DMA((2,)),
            pltpu.SemaphoreType.DMA((2,)),
        ],
    )
    return pl.pallas_call(
        _experts_body,
        grid_spec=grid_spec,
        out_shape=jax.ShapeDtypeStruct((xs.shape[0], D_MODEL), BF16),
        compiler_params=pltpu.CompilerParams(
            dimension_semantics=("arbitrary",), vmem_limit_bytes=VMEM_LIMIT),
        name="experts",
    )(first_block, n_expert_blocks, n_used, xs, w_gate, w_up, w_down)


def _combine_body(n_prompt_steps, grow_ref, lo8_ref, nch_ref, nct_ref,
                  x1p_ref, x1s_ref, h2p_ref, h2s_ref, gate_ref, gains_ref, slot_ref, ys_hbm,
                  wsg_ref, wsu_ref, wsd_ref, outp_ref, outs_ref, gath, sem):
    i = pl.program_id(0)
    n_steps = pl.num_programs(0)
    par = lax.rem(i, 2)

    def fetch(step, slot):
        buf = gath.at[slot]
        _for_each_run_chunk(step, grow_ref, lo8_ref, nch_ref,
                            lambda lrow, grow: _run_copy(ys_hbm, buf, grow, lrow, sem.at[slot]).start())

    @pl.when(i == 0)
    def _():
        gath[...] = jnp.zeros_like(gath)
        fetch(0, 0)

    @pl.when(i + 1 < n_steps)
    def _():
        fetch(i + 1, 1 - par)

    is_prompt = i < n_prompt_steps
    h2 = jnp.where(is_prompt, h2p_ref[...], h2s_ref[...])
    g = jnp.dot(h2, wsg_ref[...], preferred_element_type=F32)
    u = jnp.dot(h2, wsu_ref[...], preferred_element_type=F32)
    a = (g * _sigmoid(g) * u).astype(BF16)
    ffn = jnp.dot(a, wsd_ref[...], preferred_element_type=F32)

    slot_cols = _to_sublanes(slot_ref[0].astype(F32))
    col_id = lax.broadcasted_iota(I32, (MOE_TILE, LOCAL_ROWS), 1).astype(F32)
    take = jnp.zeros((MOE_TILE, LOCAL_ROWS), F32)
    for kk in range(TOP_K):
        take = jnp.where(col_id == slot_cols[:, kk:kk + 1], 1.0, take)

    buf = gath.at[par]

    def one(c, carry):
        _run_copy(ys_hbm, buf, 0, 0, sem.at[par]).wait()
        return carry

    lax.fori_loop(0, nct_ref[i], one, 0)

    ffn = ffn + jnp.dot(take.astype(BF16), buf[...], preferred_element_type=F32)

    quarter = MOE_TILE // gate_ref.shape[0]
    gate = jnp.concatenate(
        [jnp.broadcast_to(gate_ref[q:q + 1, :], (quarter, D_MODEL)) for q in range(gate_ref.shape[0])], axis=0)
    x1 = jnp.where(is_prompt, x1p_ref[...], x1s_ref[...])
    out = x1 + gate * (_rms(ffn) * gains_ref[3:4, :])

    @pl.when(is_prompt)
    def _():
        outp_ref[...] = out

    @pl.when(jnp.logical_not(is_prompt))
    def _():
        outs_ref[...] = out


def _combine_call(tables, x1_p, x1_s, h2_p, h2_s, gate_groups, gains, slot_tiles, ys,
                  ws_gate, ws_up, ws_down):
    n_prompt, n_sample = x1_p.shape[0], x1_s.shape[0]
    n_prompt_steps = n_prompt // MOE_TILE
    n_steps = n_prompt_steps + n_sample // MOE_TILE
    groups_per_tile = gate_groups.shape[0] // n_steps
    tok_p = lambda: pl.BlockSpec((MOE_TILE, D_MODEL), lambda i, *_: (jnp.minimum(i, n_prompt_steps - 1), 0))
    tok_s = lambda: pl.BlockSpec((MOE_TILE, D_MODEL), lambda i, *_: (jnp.maximum(i - n_prompt_steps, 0), 0))
    whole = lambda shape: pl.BlockSpec(shape, lambda i, *_: (0,) * len(shape))
    grid_spec = pltpu.PrefetchScalarGridSpec(
        num_scalar_prefetch=4,
        grid=(n_steps,),
        in_specs=[
            tok_p(), tok_s(), tok_p(), tok_s(),
            pl.BlockSpec((groups_per_tile, D_MODEL), lambda i, *_: (i, 0)),
            whole((4, D_MODEL)),
            pl.BlockSpec((1, ROUTE_ROWS, MOE_TILE), lambda i, *_: (i, 0, 0)),
            pl.BlockSpec(memory_space=pl.ANY),
            whole((D_MODEL, D_SHARED)), whole((D_MODEL, D_SHARED)), whole((D_SHARED, D_MODEL)),
        ],
        out_specs=[tok_p(), tok_s()],
        scratch_shapes=[
            pltpu.VMEM((2, LOCAL_ROWS, D_MODEL), BF16),
            pltpu.SemaphoreType.DMA((2,)),
        ],
    )
    return pl.pallas_call(
        functools.partial(_combine_body, n_prompt_steps),
        grid_spec=grid_spec,
        out_shape=[jax.ShapeDtypeStruct((n_prompt, D_MODEL), F32),
                   jax.ShapeDtypeStruct((n_sample, D_MODEL), F32)],
        compiler_params=pltpu.CompilerParams(
            dimension_semantics=("arbitrary",), vmem_limit_bytes=VMEM_LIMIT),
        name="combine",
    )(*tables, x1_p, x1_s, h2_p, h2_s, gate_groups, gains, slot_tiles, ys, ws_gate, ws_up, ws_down)


def kernel(x_prompt, x_sample, c_prompt, c_sample, state_pool, cache_k, cache_v, w_ada, b_ada, norm_gains,
           w_in, w_pool, pool_scale, attn_sinks, w_out, w_router, router_bias, w_gate, w_up, w_down,
           ws_gate, ws_up, ws_down):
    assert w_ada.shape[0] == 1, "single-layer kernel"
    bsz, seq, _ = x_prompt.shape
    dbsz, dseq, _ = x_sample.shape
    n_prompt, n_sample = bsz * seq, dbsz * dseq
    n_tok = n_prompt + n_sample
    assert dseq == CHUNK and seq % MOE_TILE == 0 and n_sample % MOE_TILE == 0

    w_in0, w_out0 = w_in[0], w_out[0]
    wq = w_in0[:, POOL_WIDTH:POOL_WIDTH + ATTN_WIDTH].reshape(D_MODEL, N_KV_HEADS, GQ, HEAD_DIM)
    wq = jnp.transpose(wq, (0, 2, 1, 3)).reshape(D_MODEL, ATTN_WIDTH)
    w_in_p = jnp.concatenate(
        [w_in0[:, :POOL_WIDTH], wq, w_in0[:, POOL_WIDTH + ATTN_WIDTH:]], axis=1).astype(BF16)
    wo = w_out0[POOL_WIDTH:].reshape(N_KV_HEADS, GQ, HEAD_DIM, D_MODEL)
    wo = jnp.transpose(wo, (1, 0, 2, 3)).reshape(ATTN_WIDTH, D_MODEL)
    w_out_p = jnp.concatenate([w_out0[:POOL_WIDTH], wo], axis=0).astype(BF16)
    w_pool_b = w_pool[0].astype(BF16)
    pscale = pool_scale[0].reshape(1, POOL_WIDTH)
    wr_t = w_router[0].T
    wr_hi = wr_t.astype(BF16)
    wr_lo = (wr_t - wr_hi.astype(F32)).astype(BF16)
    wr_split = jnp.concatenate([wr_hi, wr_lo], axis=0)
    rbias = router_bias[0].reshape(N_EXPERTS, 1)
    gains = norm_gains[0]
    sinks = attn_sinks[0]

    mod = _ada_call(jnp.concatenate([c_prompt, c_sample], axis=0), w_ada[0], b_ada[0])
    mod = mod.reshape(bsz + dbsz, 6, D_MODEL)

    zeros_pool = jnp.zeros((bsz, POOL_HIST_PAD, POOL_WIDTH), F32)
    zeros_kv = jnp.zeros((bsz, WINDOW, KV_WIDTH), F32)
    cnt0 = jnp.zeros((N_EXPERTS, LANES), F32)
    (x1_p, h2_p, pool_p, nk_p, nv_p, eidx_p, wts_p, rank_p, tcnt_p) = _front_call(
        x_prompt, mod[:bsz], gains, zeros_pool, zeros_kv, zeros_kv, cnt0, sinks,
        w_in_p, w_pool_b, pscale, w_out_p, wr_split, rbias, tile=MOE_TILE, has_cache=False)
    cnt_prompt = jnp.broadcast_to(jnp.sum(tcnt_p[:, 0, :], axis=0)[:, None], (N_EXPERTS, LANES))

    hist_pad = ATTN_KEYS - CHUNK - WINDOW
    pool_s0 = jnp.pad(state_pool[0], ((0, 0), (POOL_HIST_PAD - state_pool.shape[2], 0), (0, 0)))
    ck = jnp.pad(cache_k[0].reshape(dbsz, WINDOW, KV_WIDTH), ((0, 0), (hist_pad, 0), (0, 0)))
    cv = jnp.pad(cache_v[0].reshape(dbsz, WINDOW, KV_WIDTH), ((0, 0), (hist_pad, 0), (0, 0)))
    (x1_s, h2_s, pool_s, nk_s, nv_s, eidx_s, wts_s, rank_s, tcnt_s) = _front_call(
        x_sample, mod[bsz:], gains, pool_s0, ck, cv, cnt_prompt, sinks,
        w_in_p, w_pool_b, pscale, w_out_p, wr_split, rbias, tile=CHUNK, has_cache=True)

    def moe_tiles(route_p, route_s):
        per = MOE_TILE // CHUNK
        regrouped = jnp.transpose(route_s.reshape(-1, per, ROUTE_ROWS, CHUNK), (0, 2, 1, 3))
        return jnp.concatenate([route_p, regrouped.reshape(-1, ROUTE_ROWS, MOE_TILE)], axis=0)

    n_steps = n_tok // MOE_TILE
    cnt_td = jnp.concatenate(
        [tcnt_p[:, 0, :], tcnt_s[:, 0, :].reshape(-1, MOE_TILE // CHUNK, N_EXPERTS).sum(axis=1)],
        axis=0).astype(I32)
    c8 = (cnt_td + RUN_ROWS - 1) // RUN_ROWS * RUN_ROWS
    rows8 = jnp.sum(c8, axis=0)
    padded = (rows8 + EXPERT_ROWS - 1) // EXPERT_ROWS * EXPERT_ROWS
    pends = jnp.cumsum(padded)
    pstarts = pends - padded
    base = jnp.cumsum(cnt_td, axis=0) - cnt_td
    grow = pstarts[None, :] + jnp.cumsum(c8, axis=0) - c8
    lo8 = jnp.cumsum(c8, axis=1) - c8
    nch = c8 // RUN_ROWS
    nct = jnp.sum(nch, axis=1)
    off_tiles = (lo8 - base).astype(F32).reshape(n_steps, N_EXPERTS, 1)
    tables = (grow.reshape(-1).astype(I32), lo8.reshape(-1).astype(I32),
              nch.reshape(-1).astype(I32), nct.astype(I32))
    n_blocks = -(-(n_tok * TOP_K + n_steps * N_EXPERTS * (RUN_ROWS - 1) + N_EXPERTS * (EXPERT_ROWS - 1))
                 // EXPERT_ROWS)
    n_used = (pends[-1] // EXPERT_ROWS).astype(I32).reshape(1)
    zero_tables = ((pstarts + rows8).astype(I32), ((padded - rows8) // RUN_ROWS).astype(I32), n_used)

    h2_pf, h2_sf = h2_p.reshape(n_prompt, D_MODEL), h2_s.reshape(n_sample, D_MODEL)
    xs, slot_tiles = _dispatch_call(
        tables, zero_tables, h2_pf, h2_sf, moe_tiles(eidx_p, eidx_s), moe_tiles(rank_p, rank_s), off_tiles,
        moe_tiles(wts_p, wts_s), n_blocks * EXPERT_ROWS)
    ys = _experts_call((pstarts // EXPERT_ROWS).astype(I32), (padded // EXPERT_ROWS).astype(I32), n_used,
                       xs, w_gate[0], w_up[0], w_down[0])
    gate_groups = jnp.concatenate(
        [jnp.repeat(mod[:bsz, 5], seq // GATE_GROUP, axis=0),
         jnp.repeat(mod[bsz:, 5], dseq // GATE_GROUP, axis=0)], axis=0)
    y_p, y_s = _combine_call(
        tables, x1_p.reshape(n_prompt, D_MODEL), x1_s.reshape(n_sample, D_MODEL), h2_pf, h2_sf,
        gate_groups, gains, slot_tiles, ys,
        ws_gate[0].astype(BF16), ws_up[0].astype(BF16), ws_down[0].astype(BF16))

    n_hist = state_pool.shape[2]
    kv_shape = (1, -1, WINDOW, N_KV_HEADS, HEAD_DIM)
    return (y_p.reshape(bsz, seq, D_MODEL), y_s.reshape(dbsz, dseq, D_MODEL),
            pool_p[None, :, POOL_HIST_PAD - n_hist:], nk_p.reshape(kv_shape), nv_p.reshape(kv_shape),
            pool_s[None, :, POOL_HIST_PAD - n_hist:], nk_s.reshape(kv_shape), nv_s.reshape(kv_shape))
```

```python
import functools

import jax
import jax.numpy as jnp
import numpy as np
from jax import lax
from jax.experimental import pallas as pl
from jax.experimental.pallas import tpu as pltpu

F32 = jnp.float32
BF16 = jnp.bfloat16
I32 = jnp.int32

D_MODEL = 1024
CHUNK = 64
POOL_WIDTH = 512
POOL_WINDOWS = (2, 4, 8, 16)
POOL_GROUP_W = 128
POOL_HIST_PAD = 16
HEAD_DIM = 64
N_HEADS = 8
N_KV_HEADS = 2
GQ = N_HEADS // N_KV_HEADS
ATTN_WIDTH = N_HEADS * HEAD_DIM
KV_WIDTH = N_KV_HEADS * HEAD_DIM
IN_WIDTH = POOL_WIDTH + ATTN_WIDTH + 2 * KV_WIDTH
WINDOW = 128
N_EXPERTS = 64
TOP_K = 6
N_EXPERT_GROUPS = 8
GROUP_SIZE = N_EXPERTS // N_EXPERT_GROUPS
TOPK_GROUPS = 4
D_EXPERT = 256
D_SHARED = 256
ROUTED_SCALE = 2.5
NORM_EPS = 1e-6
NEG_BIG = -1e30

LANES = 128
SUBLANES = 8
ATTN_KEYS = 2 * LANES
ROUTE_ROWS = 8
EXPERT_ROWS = 256
MOE_TILE = 256
GATE_GROUP = MOE_TILE // SUBLANES
RUN_ROWS = 2 * SUBLANES
LOCAL_ROWS = 2560
ROW_W = D_MODEL + LANES
VMEM_LIMIT = 56 * 1024 * 1024

assert TOP_K * MOE_TILE + N_EXPERTS * (RUN_ROWS - 1) <= LOCAL_ROWS

_NT = (((1,), (1,)), ((), ()))


def _rms(v):
    return v * lax.rsqrt(jnp.mean(v * v, axis=-1, keepdims=True) + NORM_EPS)


def _sigmoid(v):
    return 1.0 / (1.0 + jnp.exp(-v))


def _split3(v):
    hi = v.astype(BF16)
    r1 = v - hi.astype(F32)
    mid = r1.astype(BF16)
    lo = (r1 - mid.astype(F32)).astype(BF16)
    return hi, mid, lo


def _to_sublanes(rows):
    n = rows.shape[1]
    hi, mid, lo = _split3(rows)
    eye = (lax.broadcasted_iota(I32, (n, n), 0) == lax.broadcasted_iota(I32, (n, n), 1)).astype(BF16)
    return (lax.dot_general(eye, hi, _NT, preferred_element_type=F32)
            + lax.dot_general(eye, mid, _NT, preferred_element_type=F32)
            + lax.dot_general(eye, lo, _NT, preferred_element_type=F32))


def _ada_body(c_ref, w_ref, b_ref, o_ref):
    c = c_ref[...]
    s = c * _sigmoid(c)
    o_ref[...] = jnp.dot(s, w_ref[...], preferred_element_type=F32) + b_ref[...]


def _ada_call(c_all, w_ada, b_ada):
    nb = c_all.shape[0]
    n_out = w_ada.shape[1]
    tile = D_MODEL
    return pl.pallas_call(
        _ada_body,
        grid=(n_out // tile,),
        in_specs=[
            pl.BlockSpec((nb, D_MODEL), lambda j: (0, 0)),
            pl.BlockSpec((D_MODEL, tile), lambda j: (0, j)),
            pl.BlockSpec((1, tile), lambda j: (0, j)),
        ],
        out_specs=pl.BlockSpec((nb, tile), lambda j: (0, j)),
        out_shape=jax.ShapeDtypeStruct((nb, n_out), F32),
        compiler_params=pltpu.CompilerParams(dimension_semantics=("arbitrary",)),
        name="ada",
    )(c_all, w_ada, b_ada.reshape(1, n_out))


def _front_body(tile, has_cache,
                sinks_ref, x_ref, mod_ref, gains_ref, hp_ref, hk_ref, hv_ref, cnt0_ref,
                win_ref, wpool_ref, pscale_ref, wout_ref, wrt_ref, rbias_ref,
                x1_ref, h2_ref, npool_ref, nk_ref, nv_ref, eidx_ref, wts_ref, rank_ref, tcnt_ref,
                ubuf, khist, vhist, run):
    b = pl.program_id(0)
    i = pl.program_id(1)
    sub_q = min(tile, LANES)
    n_sub = tile // sub_q
    hist_keys = ATTN_KEYS - sub_q
    n_keys = hist_keys + tile

    @pl.when(i == 0)
    def _():
        ubuf[0:POOL_HIST_PAD, :] = hp_ref[0]
        khist[...] = hk_ref[0]
        vhist[...] = hv_ref[0]

    @pl.when((b == 0) & (i == 0))
    def _():
        run[...] = cnt0_ref[...]

    mod = mod_ref[0]
    gains = gains_ref[...]
    xt = x_ref[0]

    h = _rms(xt) * gains[0:1] * (1.0 + mod[1:2]) + mod[0:1]
    proj = jnp.dot(h.astype(BF16), win_ref[...], preferred_element_type=F32)
    u = proj[:, :POOL_WIDTH]
    o_k = POOL_WIDTH + ATTN_WIDTH
    k_new = proj[:, o_k:o_k + KV_WIDTH]
    v_new = proj[:, o_k + KV_WIDTH:]

    kw = jnp.concatenate([khist[...], k_new], axis=0)
    vw = jnp.concatenate([vhist[...], v_new], axis=0)
    kwb = kw.astype(BF16)
    vwb = vw.astype(BF16)
    qrow = lax.broadcasted_iota(I32, (sub_q, ATTN_KEYS), 0)
    kpos = lax.broadcasted_iota(I32, (sub_q, ATTN_KEYS), 1) - hist_keys
    qchunk = jnp.right_shift(qrow, 6)
    vis_band = (kpos >= CHUNK * (qchunk - 2)) & (kpos < CHUNK * (qchunk + 1))
    side0 = lax.broadcasted_iota(I32, (sub_q, LANES), 1) < HEAD_DIM
    attn_rows = []
    for r in range(n_sub):
        vis = vis_band
        if not has_cache:
            vis = vis & (kpos + (i * tile + r * sub_q) >= 0)
        kb = kwb[r * sub_q:r * sub_q + ATTN_KEYS]
        vb = vwb[r * sub_q:r * sub_q + ATTN_KEYS]
        blocks = []
        for j in range(GQ):
            qp = proj[r * sub_q:(r + 1) * sub_q, POOL_WIDTH + LANES * j:POOL_WIDTH + LANES * (j + 1)]
            outs = []
            for s in range(N_KV_HEADS):
                keep = side0 if s == 0 else jnp.logical_not(side0)
                qm = jnp.where(keep, qp, 0.0).astype(BF16)
                sc = lax.dot_general(qm, kb, _NT, preferred_element_type=F32) * (HEAD_DIM ** -0.5)
                sc = jnp.where(vis, sc, NEG_BIG)
                sink = sinks_ref[j + GQ * s]
                m = jnp.maximum(jnp.max(sc, axis=-1, keepdims=True), sink)
                p = jnp.exp(sc - m)
                den = jnp.sum(p, axis=-1, keepdims=True) + jnp.exp(sink - m)
                p = (p / den).astype(BF16)
                outs.append(jnp.dot(p, vb, preferred_element_type=F32))
            blocks.append(jnp.where(side0, outs[0], outs[1]))
        attn_rows.append(jnp.concatenate(blocks, axis=-1))
    attn = attn_rows[0] if n_sub == 1 else jnp.concatenate(attn_rows, axis=0)

    ubuf[POOL_HIST_PAD:POOL_HIST_PAD + tile, :] = u
    if has_cache:
        seen = None
    else:
        seen = (lax.broadcasted_iota(I32, (tile, 1), 0) + i * tile + 1).astype(F32)
    pool_blocks = []
    for g, w in enumerate(POOL_WINDOWS):
        cols = slice(POOL_GROUP_W * g, POOL_GROUP_W * (g + 1))
        acc = u[:, cols]
        for s in range(1, w):
            acc = acc + ubuf[POOL_HIST_PAD - s:POOL_HIST_PAD - s + tile, cols]
        cnt = float(w) if seen is None else jnp.minimum(seen, float(w))
        dlt = acc / cnt - u[:, cols]
        pool_blocks.append(jnp.dot(dlt.astype(BF16), wpool_ref[g], preferred_element_type=F32))
    pool = jnp.concatenate(pool_blocks, axis=-1) * pscale_ref[...]

    mixin = jnp.concatenate([pool, attn], axis=-1).astype(BF16)
    mix = jnp.dot(mixin, wout_ref[...], preferred_element_type=F32)
    x1 = xt + mod[2:3] * (_rms(mix) * gains[1:2])
    x1_ref[0] = x1

    new_hist = ubuf[tile:tile + POOL_HIST_PAD, :]
    ubuf[0:POOL_HIST_PAD, :] = new_hist
    npool_ref[0] = new_hist
    khist[...] = kw[tile:, :]
    vhist[...] = vw[tile:, :]
    nk_ref[0] = kw[n_keys - WINDOW:, :]
    nv_ref[0] = vw[n_keys - WINDOW:, :]

    h2f = _rms(x1) * gains[2:3] * (1.0 + mod[4:5]) + mod[3:4]
    h2hi = h2f.astype(BF16)
    h2_ref[0] = h2hi
    h2lo = (h2f - h2hi.astype(F32)).astype(BF16)
    wrt = wrt_ref[...]
    part = lax.dot_general(wrt, h2hi, _NT, preferred_element_type=F32)
    logits = (part[:N_EXPERTS] + part[N_EXPERTS:]
              + lax.dot_general(wrt[:N_EXPERTS], h2lo, _NT, preferred_element_type=F32))
    scores = _sigmoid(logits)
    sel = scores + rbias_ref[...]

    sub_g = lax.broadcasted_iota(I32, (GROUP_SIZE, tile), 0).astype(F32)
    gscore = jnp.zeros((N_EXPERT_GROUPS, tile), F32)
    for gi in range(N_EXPERT_GROUPS):
        blk = sel[GROUP_SIZE * gi:GROUP_SIZE * (gi + 1), :]
        m1 = jnp.max(blk, axis=0, keepdims=True)
        i1 = jnp.min(jnp.where(blk == m1, sub_g, float(GROUP_SIZE)), axis=0, keepdims=True)
        m2 = jnp.max(jnp.where(sub_g == i1, -jnp.inf, blk), axis=0, keepdims=True)
        gscore = jnp.where(sub_g == gi, m1 + m2, gscore)
    chosen = jnp.zeros((N_EXPERT_GROUPS, tile), F32)
    for _ in range(TOPK_GROUPS):
        m = jnp.max(gscore, axis=0, keepdims=True)
        idx = jnp.min(jnp.where(gscore == m, sub_g, float(N_EXPERT_GROUPS)), axis=0, keepdims=True)
        pick = sub_g == idx
        chosen = jnp.where(pick, 1.0, chosen)
        gscore = jnp.where(pick, -jnp.inf, gscore)
    emask = jnp.concatenate(
        [jnp.broadcast_to(chosen[gi:gi + 1, :], (GROUP_SIZE, tile)) for gi in range(N_EXPERT_GROUPS)], axis=0)
    selm = jnp.where(emask > 0.0, sel, -jnp.inf)

    sub_e = lax.broadcasted_iota(I32, (N_EXPERTS, tile), 0).astype(F32)
    picks, idxs, raw_w = [], [], []
    for _ in range(TOP_K):
        m = jnp.max(selm, axis=0, keepdims=True)
        idx = jnp.min(jnp.where(selm == m, sub_e, float(N_EXPERTS)), axis=0, keepdims=True)
        pick = sub_e == idx
        raw_w.append(jnp.sum(jnp.where(pick, scores, 0.0), axis=0, keepdims=True))
        selm = jnp.where(pick, -jnp.inf, selm)
        picks.append(pick)
        idxs.append(idx)
    wsum = raw_w[0]
    for kk in range(1, TOP_K):
        wsum = wsum + raw_w[kk]

    onehot = jnp.zeros((N_EXPERTS, tile), F32)
    for kk in range(TOP_K):
        onehot = jnp.where(picks[kk], 1.0, onehot)
    onehot_b = onehot.astype(BF16)
    tri = (lax.broadcasted_iota(I32, (tile, tile), 0) < lax.broadcasted_iota(I32, (tile, tile), 1)).astype(BF16)
    before = jnp.dot(onehot_b, tri, preferred_element_type=F32) + run[:, 0:1]
    sub_r = lax.broadcasted_iota(I32, (ROUTE_ROWS, tile), 0)
    eidx_o = jnp.zeros((ROUTE_ROWS, tile), I32)
    wts_o = jnp.zeros((ROUTE_ROWS, tile), F32)
    rank_o = jnp.zeros((ROUTE_ROWS, tile), I32)
    for kk in range(TOP_K):
        rk = jnp.sum(jnp.where(picks[kk], before, 0.0), axis=0, keepdims=True).astype(I32)
        eidx_o = jnp.where(sub_r == kk, idxs[kk].astype(I32), eidx_o)
        wts_o = jnp.where(sub_r == kk, raw_w[kk] / wsum * ROUTED_SCALE, wts_o)
        rank_o = jnp.where(sub_r == kk, rk, rank_o)
    eidx_ref[0] = eidx_o
    wts_ref[0] = wts_o
    rank_ref[0] = rank_o
    tcnt_ref[0] = lax.dot_general(jnp.ones((ROUTE_ROWS, tile), BF16), onehot_b, _NT,
                                  preferred_element_type=F32)
    run[...] = run[...] + jnp.sum(onehot, axis=1, keepdims=True)


def _front_call(x, mod, gains, hist_pool, hist_k, hist_v, cnt0, sinks,
                w_in, w_pool, pool_scale, w_out, wr_t, rbias, *, tile, has_cache):
    bsz, seq, _ = x.shape
    n_tiles = seq // tile
    hist_keys = hist_k.shape[1]
    assert hist_keys == ATTN_KEYS - min(tile, LANES)
    body = functools.partial(_front_body, tile, has_cache)
    whole = lambda shape: pl.BlockSpec(shape, lambda b, i: (0,) * len(shape))
    per_b = lambda shape: pl.BlockSpec((1,) + shape, lambda b, i: (b,) + (0,) * len(shape))
    route = pl.BlockSpec((1, ROUTE_ROWS, tile), lambda b, i: (b * n_tiles + i, 0, 0))
    out_shape = [
        jax.ShapeDtypeStruct((bsz, seq, D_MODEL), F32),
        jax.ShapeDtypeStruct((bsz, seq, D_MODEL), BF16),
        jax.ShapeDtypeStruct((bsz, POOL_HIST_PAD, POOL_WIDTH), F32),
        jax.ShapeDtypeStruct((bsz, WINDOW, KV_WIDTH), F32),
        jax.ShapeDtypeStruct((bsz, WINDOW, KV_WIDTH), F32),
        jax.ShapeDtypeStruct((bsz * n_tiles, ROUTE_ROWS, tile), I32),
        jax.ShapeDtypeStruct((bsz * n_tiles, ROUTE_ROWS, tile), F32),
        jax.ShapeDtypeStruct((bsz * n_tiles, ROUTE_ROWS, tile), I32),
        jax.ShapeDtypeStruct((bsz * n_tiles, ROUTE_ROWS, N_EXPERTS), F32),
    ]
    return pl.pallas_call(
        body,
        grid=(bsz, n_tiles),
        in_specs=[
            pl.BlockSpec(memory_space=pltpu.SMEM),
            pl.BlockSpec((1, tile, D_MODEL), lambda b, i: (b, i, 0)),
            per_b((6, D_MODEL)),
            whole((4, D_MODEL)),
            per_b((POOL_HIST_PAD, POOL_WIDTH)),
            per_b((hist_keys, KV_WIDTH)),
            per_b((hist_keys, KV_WIDTH)),
            whole((N_EXPERTS, LANES)),
            whole((D_MODEL, IN_WIDTH)),
            whole((len(POOL_WINDOWS), POOL_GROUP_W, POOL_GROUP_W)),
            whole((1, POOL_WIDTH)),
            whole((D_MODEL, D_MODEL)),
            whole((2 * N_EXPERTS, D_MODEL)),
            whole((N_EXPERTS, 1)),
        ],
        out_specs=[
            pl.BlockSpec((1, tile, D_MODEL), lambda b, i: (b, i, 0)),
            pl.BlockSpec((1, tile, D_MODEL), lambda b, i: (b, i, 0)),
            per_b((POOL_HIST_PAD, POOL_WIDTH)),
            per_b((WINDOW, KV_WIDTH)),
            per_b((WINDOW, KV_WIDTH)),
            route, route, route,
            pl.BlockSpec((1, ROUTE_ROWS, N_EXPERTS), lambda b, i: (b * n_tiles + i, 0, 0)),
        ],
        out_shape=out_shape,
        scratch_shapes=[
            pltpu.VMEM((tile + POOL_HIST_PAD, POOL_WIDTH), F32),
            pltpu.VMEM((hist_keys, KV_WIDTH), F32),
            pltpu.VMEM((hist_keys, KV_WIDTH), F32),
            pltpu.VMEM((N_EXPERTS, LANES), F32),
        ],
        compiler_params=pltpu.CompilerParams(
            dimension_semantics=("arbitrary", "arbitrary"), vmem_limit_bytes=VMEM_LIMIT),
        name="front_cached" if has_cache else "front_prompt",
    )(sinks, x, mod, gains, hist_pool, hist_k, hist_v, cnt0,
      w_in, w_pool, pool_scale, w_out, wr_t, rbias)


def _run_copy(src, dst, s_row, d_row, sem):
    return pltpu.make_async_copy(src.at[pl.ds(s_row, RUN_ROWS)], dst.at[pl.ds(d_row, RUN_ROWS)], sem)


def _for_each_run_chunk(step, grow_ref, lo8_ref, nch_ref, fn):
    def per_expert(e, carry):
        idx = step * N_EXPERTS + e
        local0 = lo8_ref[idx]
        global0 = grow_ref[idx]

        def per_chunk(c, carry2):
            fn(pl.multiple_of(local0 + RUN_ROWS * c, RUN_ROWS), pl.multiple_of(global0 + RUN_ROWS * c, RUN_ROWS))
            return carry2

        lax.fori_loop(0, nch_ref[idx], per_chunk, 0)
        return carry

    lax.fori_loop(0, N_EXPERTS, per_expert, 0, unroll=2)


def _dispatch_body(n_prompt_steps, grow_ref, lo8_ref, nch_ref, nct_ref, zrow_ref, znch_ref, nused_ref,
                   h2p_ref, h2s_ref, eidx_ref, rank_ref, off_ref, wts_ref, xs_out, slot_out, loc, zrows, sem):
    i = pl.program_id(0)
    n_steps = pl.num_programs(0)
    par = lax.rem(i, 2)
    h2 = jnp.where(i < n_prompt_steps, h2p_ref[...], h2s_ref[...])
    eidx = eidx_ref[0]
    rank = rank_ref[0]
    wts = wts_ref[0]
    off = off_ref[0]
    expert_id = lax.broadcasted_iota(I32, (N_EXPERTS, MOE_TILE), 0)
    sub_r = lax.broadcasted_iota(I32, (ROUTE_ROWS, MOE_TILE), 0)
    slots = []
    slot_o = jnp.zeros((ROUTE_ROWS, MOE_TILE), I32)
    for kk in range(TOP_K):
        mine = jnp.sum(jnp.where(expert_id == eidx[kk:kk + 1, :], off, 0.0), axis=0, keepdims=True)
        slots.append(rank[kk:kk + 1, :] + mine.astype(I32))
        slot_o = jnp.where(sub_r == kk, slots[kk], slot_o)
    slot_out[0] = slot_o
    row_id = lax.broadcasted_iota(I32, (LOCAL_ROWS, MOE_TILE), 0)
    sel = jnp.zeros((LOCAL_ROWS, MOE_TILE), F32)
    selw = jnp.zeros((LOCAL_ROWS, MOE_TILE), F32)
    for kk in range(TOP_K):
        hit = row_id == slots[kk]
        sel = jnp.where(hit, 1.0, sel)
        selw = jnp.where(hit, wts[kk:kk + 1, :], selw)
    sorted_rows = jnp.dot(sel.astype(BF16), h2, preferred_element_type=F32)
    w_slot = jnp.sum(selw, axis=1, keepdims=True)
    buf = loc.at[par]
    buf[:, :D_MODEL] = sorted_rows.astype(BF16)
    w_hi, w_mid, w_lo = _split3(w_slot)
    lane = lax.broadcasted_iota(I32, (LOCAL_ROWS, LANES), 1)
    w_lanes = jnp.where(lane == 0, w_hi.astype(F32),
                        jnp.where(lane == 1, w_mid.astype(F32), jnp.where(lane == 2, w_lo.astype(F32), 0.0)))
    buf[:, D_MODEL:] = w_lanes.astype(BF16)

    def drain(n_chunks):
        def one(c, carry):
            _run_copy(buf, xs_out, 0, 0, sem).wait()
            return carry
        lax.fori_loop(0, n_chunks, one, 0)

    @pl.when(i > 0)
    def _():
        drain(nct_ref[i - 1])

    _for_each_run_chunk(i, grow_ref, lo8_ref, nch_ref,
                        lambda lrow, grow: _run_copy(buf, xs_out, lrow, grow, sem).start())

    @pl.when(i == n_steps - 1)
    def _():
        zrows[...] = jnp.zeros_like(zrows)

        def per_expert(e, total):
            def per_chunk(c, carry):
                _run_copy(zrows, xs_out, 0, pl.multiple_of(zrow_ref[e] + RUN_ROWS * c, RUN_ROWS), sem).start()
                return carry
            lax.fori_loop(0, znch_ref[e], per_chunk, 0)
            return total + znch_ref[e]

        n_zero = lax.fori_loop(0, N_EXPERTS, per_expert, 0)
        drain(nct_ref[i] + n_zero)

        def block_copy(blk):
            return pltpu.make_async_copy(
                zrows, xs_out.at[pl.ds(pl.multiple_of(blk * EXPERT_ROWS, EXPERT_ROWS), EXPERT_ROWS)], sem)

        n_blocks = xs_out.shape[0] // EXPERT_ROWS

        def start_block(blk, carry):
            block_copy(blk).start()
            return carry

        def wait_block(blk, carry):
            block_copy(blk).wait()
            return carry

        lax.fori_loop(nused_ref[0], n_blocks, start_block, 0)
        lax.fori_loop(nused_ref[0], n_blocks, wait_block, 0)


def _dispatch_call(tables, zero_tables, h2_p, h2_s, eidx_tiles, rank_tiles, off_tiles, wts_tiles, n_rows):
    n_prompt_steps = h2_p.shape[0] // MOE_TILE
    n_steps = n_prompt_steps + h2_s.shape[0] // MOE_TILE
    route = lambda: pl.BlockSpec((1, ROUTE_ROWS, MOE_TILE), lambda i, *_: (i, 0, 0))
    grid_spec = pltpu.PrefetchScalarGridSpec(
        num_scalar_prefetch=7,
        grid=(n_steps,),
        in_specs=[
            pl.BlockSpec((MOE_TILE, D_MODEL), lambda i, *_: (jnp.minimum(i, n_prompt_steps - 1), 0)),
            pl.BlockSpec((MOE_TILE, D_MODEL), lambda i, *_: (jnp.maximum(i - n_prompt_steps, 0), 0)),
            route(), route(),
            pl.BlockSpec((1, N_EXPERTS, 1), lambda i, *_: (i, 0, 0)),
            route(),
        ],
        out_specs=[pl.BlockSpec(memory_space=pl.ANY), route()],
        scratch_shapes=[
            pltpu.VMEM((2, LOCAL_ROWS, ROW_W), BF16),
            pltpu.VMEM((EXPERT_ROWS, ROW_W), BF16),
            pltpu.SemaphoreType.DMA,
        ],
    )
    return pl.pallas_call(
        functools.partial(_dispatch_body, n_prompt_steps),
        grid_spec=grid_spec,
        out_shape=[jax.ShapeDtypeStruct((n_rows, ROW_W), BF16),
                   jax.ShapeDtypeStruct((n_steps, ROUTE_ROWS, MOE_TILE), I32)],
        compiler_params=pltpu.CompilerParams(
            dimension_semantics=("arbitrary",), vmem_limit_bytes=VMEM_LIMIT),
        name="dispatch",
    )(*tables, *zero_tables, h2_p, h2_s, eidx_tiles, rank_tiles, off_tiles, wts_tiles)


def _block_rows(ref, blk):
    return ref.at[pl.ds(pl.multiple_of(blk * EXPERT_ROWS, EXPERT_ROWS), EXPERT_ROWS)]


def _experts_body(first_ref, nblk_ref, nused_ref, xs_hbm, wg_ref, wu_ref, wd_ref, ys_hbm,
                  xbuf, ybuf, wgb, wub, wdb, isem, osem):
    e = pl.program_id(0)
    first = first_ref[e]
    n_blk = nblk_ref[e]

    def fetch(block, slot):
        return pltpu.make_async_copy(_block_rows(xs_hbm, block), xbuf.at[slot], isem.at[slot])

    def in_copy(blk, slot):
        return fetch(first + blk, slot)

    def out_copy(blk, slot):
        return pltpu.make_async_copy(ybuf.at[slot], _block_rows(ys_hbm, first + blk), osem.at[slot])

    @pl.when((e == 0) & (n_blk > 0))
    def _():
        in_copy(0, 0).start()

    @pl.when(n_blk > 0)
    def _():
        wgb[...] = wg_ref[0].astype(BF16)
        wub[...] = wu_ref[0].astype(BF16)
        wdb[...] = wd_ref[0].astype(BF16)

        def one_block(blk, carry):
            slot = lax.rem(blk, 2)

            @pl.when(blk + 1 < n_blk)
            def _():
                in_copy(blk + 1, 1 - slot).start()

            in_copy(blk, slot).wait()

            @pl.when(blk >= 2)
            def _():
                out_copy(blk - 2, slot).wait()

            xrow = xbuf[slot]
            x = xrow[:, :D_MODEL]
            w_parts = xrow[:, D_MODEL:].astype(F32)
            w_row = w_parts[:, 0:1] + w_parts[:, 1:2] + w_parts[:, 2:3]
            g = jnp.dot(x, wgb[...], preferred_element_type=F32)
            u = jnp.dot(x, wub[...], preferred_element_type=F32)
            a = (g * _sigmoid(g) * u).astype(BF16)
            y = jnp.dot(a, wdb[...], preferred_element_type=F32) * w_row
            ybuf[slot] = y.astype(BF16)
            out_copy(blk, slot).start()
            return carry

        lax.fori_loop(0, n_blk, one_block, 0)

        @pl.when(n_blk >= 2)
        def _():
            out_copy(n_blk - 2, lax.rem(n_blk, 2)).wait()
        out_copy(n_blk - 1, lax.rem(n_blk - 1, 2)).wait()

    nxt = jnp.minimum(e + 1, pl.num_programs(0) - 1)

    @pl.when((e + 1 < pl.num_programs(0)) & (nblk_ref[nxt] > 0))
    def _():
        fetch(first_ref[nxt], 0).start()

    @pl.when(e == pl.num_programs(0) - 1)
    def _():
        ybuf[0] = jnp.zeros((EXPERT_ROWS, D_MODEL), BF16)
        n_blocks = ys_hbm.shape[0] // EXPERT_ROWS

        def tail_copy(blk):
            return pltpu.make_async_copy(ybuf.at[0], _block_rows(ys_hbm, blk), osem.at[0])

        def start_block(blk, carry):
            tail_copy(blk).start()
            return carry

        def wait_block(blk, carry):
            tail_copy(blk).wait()
            return carry

        lax.fori_loop(nused_ref[0], n_blocks, start_block, 0)
        lax.fori_loop(nused_ref[0], n_blocks, wait_block, 0)


def _experts_call(first_block, n_expert_blocks, n_used, xs, w_gate, w_up, w_down):
    grid_spec = pltpu.PrefetchScalarGridSpec(
        num_scalar_prefetch=3,
        grid=(N_EXPERTS,),
        in_specs=[
            pl.BlockSpec(memory_space=pl.ANY),
            pl.BlockSpec((1, D_MODEL, D_EXPERT), lambda e, *_: (e, 0, 0)),
            pl.BlockSpec((1, D_MODEL, D_EXPERT), lambda e, *_: (e, 0, 0)),
            pl.BlockSpec((1, D_EXPERT, D_MODEL), lambda e, *_: (e, 0, 0)),
        ],
        out_specs=pl.BlockSpec(memory_space=pl.ANY),
        scratch_shapes=[
            pltpu.VMEM((2, EXPERT_ROWS, ROW_W), BF16),
            pltpu.VMEM((2, EXPERT_ROWS, D_MODEL), BF16),
            pltpu.VMEM((D_MODEL, D_EXPERT), BF16),
            pltpu.VMEM((D_MODEL, D_EXPERT), BF16),
            pltpu.VMEM((D_EXPERT, D_MODEL), BF16),
            pltpu.SemaphoreType.DMA((2,)),
            pltpu.SemaphoreType.DMA((2,)),
        ],
    )
    return pl.pallas_call(
        _experts_body,
        grid_spec=grid_spec,
        out_shape=jax.ShapeDtypeStruct((xs.shape[0], D_MODEL), BF16),
        compiler_params=pltpu.CompilerParams(
            dimension_semantics=("arbitrary",), vmem_limit_bytes=VMEM_LIMIT),
        name="experts",
    )(first_block, n_expert_blocks, n_used, xs, w_gate, w_up, w_down)


def _combine_body(n_prompt_steps, grow_ref, lo8_ref, nch_ref, nct_ref,
                  x1p_ref, x1s_ref, h2p_ref, h2s_ref, gate_ref, gains_ref, slot_ref, ys_hbm,
                  wsg_ref, wsu_ref, wsd_ref, outp_ref, outs_ref, gath, sem):
    i = pl.program_id(0)
    n_steps = pl.num_programs(0)
    par = lax.rem(i, 2)

    def fetch(step, slot):
        buf = gath.at[slot]
        _for_each_run_chunk(step, grow_ref, lo8_ref, nch_ref,
                            lambda lrow, grow: _run_copy(ys_hbm, buf, grow, lrow, sem.at[slot]).start())

    @pl.when(i == 0)
    def _():
        gath[...] = jnp.zeros_like(gath)
        fetch(0, 0)

    @pl.when(i + 1 < n_steps)
    def _():
        fetch(i + 1, 1 - par)

    is_prompt = i < n_prompt_steps
    h2 = jnp.where(is_prompt, h2p_ref[...], h2s_ref[...])
    g = jnp.dot(h2, wsg_ref[...], preferred_element_type=F32)
    u = jnp.dot(h2, wsu_ref[...], preferred_element_type=F32)
    a = (g * _sigmoid(g) * u).astype(BF16)
    ffn = jnp.dot(a, wsd_ref[...], preferred_element_type=F32)

    slot_cols = _to_sublanes(slot_ref[0].astype(F32))
    col_id = lax.broadcasted_iota(I32, (MOE_TILE, LOCAL_ROWS), 1).astype(F32)
    take = jnp.zeros((MOE_TILE, LOCAL_ROWS), F32)
    for kk in range(TOP_K):
        take = jnp.where(col_id == slot_cols[:, kk:kk + 1], 1.0, take)

    buf = gath.at[par]

    def one(c, carry):
        _run_copy(ys_hbm, buf, 0, 0, sem.at[par]).wait()
        return carry

    lax.fori_loop(0, nct_ref[i], one, 0)

    ffn = ffn + jnp.dot(take.astype(BF16), buf[...], preferred_element_type=F32)

    quarter = MOE_TILE // gate_ref.shape[0]
    gate = jnp.concatenate(
        [jnp.broadcast_to(gate_ref[q:q + 1, :], (quarter, D_MODEL)) for q in range(gate_ref.shape[0])], axis=0)
    x1 = jnp.where(is_prompt, x1p_ref[...], x1s_ref[...])
    out = x1 + gate * (_rms(ffn) * gains_ref[3:4, :])

    @pl.when(is_prompt)
    def _():
        outp_ref[...] = out

    @pl.when(jnp.logical_not(is_prompt))
    def _():
        outs_ref[...] = out


def _combine_call(tables, x1_p, x1_s, h2_p, h2_s, gate_groups, gains, slot_tiles, ys,
                  ws_gate, ws_up, ws_down):
    n_prompt, n_sample = x1_p.shape[0], x1_s.shape[0]
    n_prompt_steps = n_prompt // MOE_TILE
    n_steps = n_prompt_steps + n_sample // MOE_TILE
    groups_per_tile = gate_groups.shape[0] // n_steps
    tok_p = lambda: pl.BlockSpec((MOE_TILE, D_MODEL), lambda i, *_: (jnp.minimum(i, n_prompt_steps - 1), 0))
    tok_s = lambda: pl.BlockSpec((MOE_TILE, D_MODEL), lambda i, *_: (jnp.maximum(i - n_prompt_steps, 0), 0))
    whole = lambda shape: pl.BlockSpec(shape, lambda i, *_: (0,) * len(shape))
    grid_spec = pltpu.PrefetchScalarGridSpec(
        num_scalar_prefetch=4,
        grid=(n_steps,),
        in_specs=[
            tok_p(), tok_s(), tok_p(), tok_s(),
            pl.BlockSpec((groups_per_tile, D_MODEL), lambda i, *_: (i, 0)),
            whole((4, D_MODEL)),
            pl.BlockSpec((1, ROUTE_ROWS, MOE_TILE), lambda i, *_: (i, 0, 0)),
            pl.BlockSpec(memory_space=pl.ANY),
            whole((D_MODEL, D_SHARED)), whole((D_MODEL, D_SHARED)), whole((D_SHARED, D_MODEL)),
        ],
        out_specs=[tok_p(), tok_s()],
        scratch_shapes=[
            pltpu.VMEM((2, LOCAL_ROWS, D_MODEL), BF16),
            pltpu.SemaphoreType.DMA((2,)),
        ],
    )
    return pl.pallas_call(
        functools.partial(_combine_body, n_prompt_steps),
        grid_spec=grid_spec,
        out_shape=[jax.ShapeDtypeStruct((n_prompt, D_MODEL), F32),
                   jax.ShapeDtypeStruct((n_sample, D_MODEL), F32)],
        compiler_params=pltpu.CompilerParams(
            dimension_semantics=("arbitrary",), vmem_limit_bytes=VMEM_LIMIT),
        name="combine",
    )(*tables, x1_p, x1_s, h2_p, h2_s, gate_groups, gains, slot_tiles, ys, ws_gate, ws_up, ws_down)


def kernel(x_prompt, x_sample, c_prompt, c_sample, state_pool, cache_k, cache_v, w_ada, b_ada, norm_gains,
           w_in, w_pool, pool_scale, attn_sinks, w_out, w_router, router_bias, w_gate, w_up, w_down,
           ws_gate, ws_up, ws_down):
    assert w_ada.shape[0] == 1, "single-layer kernel"
    bsz, seq, _ = x_prompt.shape
    dbsz, dseq, _ = x_sample.shape
    n_prompt, n_sample = bsz * seq, dbsz * dseq
    n_tok = n_prompt + n_sample
    assert dseq == CHUNK and seq % MOE_TILE == 0 and n_sample % MOE_TILE == 0

    w_in0, w_out0 = w_in[0], w_out[0]
    wq = w_in0[:, POOL_WIDTH:POOL_WIDTH + ATTN_WIDTH].reshape(D_MODEL, N_KV_HEADS, GQ, HEAD_DIM)
    wq = jnp.transpose(wq, (0, 2, 1, 3)).reshape(D_MODEL, ATTN_WIDTH)
    w_in_p = jnp.concatenate(
        [w_in0[:, :POOL_WIDTH], wq, w_in0[:, POOL_WIDTH + ATTN_WIDTH:]], axis=1).astype(BF16)
    wo = w_out0[POOL_WIDTH:].reshape(N_KV_HEADS, GQ, HEAD_DIM, D_MODEL)
    wo = jnp.transpose(wo, (1, 0, 2, 3)).reshape(ATTN_WIDTH, D_MODEL)
    w_out_p = jnp.concatenate([w_out0[:POOL_WIDTH], wo], axis=0).astype(BF16)
    w_pool_b = w_pool[0].astype(BF16)
    pscale = pool_scale[0].reshape(1, POOL_WIDTH)
    wr_t = w_router[0].T
    wr_hi = wr_t.astype(BF16)
    wr_lo = (wr_t - wr_hi.astype(F32)).astype(BF16)
    wr_split = jnp.concatenate([wr_hi, wr_lo], axis=0)
    rbias = router_bias[0].reshape(N_EXPERTS, 1)
    gains = norm_gains[0]
    sinks = attn_sinks[0]

    mod = _ada_call(jnp.concatenate([c_prompt, c_sample], axis=0), w_ada[0], b_ada[0])
    mod = mod.reshape(bsz + dbsz, 6, D_MODEL)

    zeros_pool = jnp.zeros((bsz, POOL_HIST_PAD, POOL_WIDTH), F32)
    zeros_kv = jnp.zeros((bsz, WINDOW, KV_WIDTH), F32)
    cnt0 = jnp.zeros((N_EXPERTS, LANES), F32)
    (x1_p, h2_p, pool_p, nk_p, nv_p, eidx_p, wts_p, rank_p, tcnt_p) = _front_call(
        x_prompt, mod[:bsz], gains, zeros_pool, zeros_kv, zeros_kv, cnt0, sinks,
        w_in_p, w_pool_b, pscale, w_out_p, wr_split, rbias, tile=MOE_TILE, has_cache=False)
    cnt_prompt = jnp.broadcast_to(jnp.sum(tcnt_p[:, 0, :], axis=0)[:, None], (N_EXPERTS, LANES))

    hist_pad = ATTN_KEYS - CHUNK - WINDOW
    pool_s0 = jnp.pad(state_pool[0], ((0, 0), (POOL_HIST_PAD - state_pool.shape[2], 0), (0, 0)))
    ck = jnp.pad(cache_k[0].reshape(dbsz, WINDOW, KV_WIDTH), ((0, 0), (hist_pad, 0), (0, 0)))
    cv = jnp.pad(cache_v[0].reshape(dbsz, WINDOW, KV_WIDTH), ((0, 0), (hist_pad, 0), (0, 0)))
    (x1_s, h2_s, pool_s, nk_s, nv_s, eidx_s, wts_s, rank_s, tcnt_s) = _front_call(
        x_sample, mod[bsz:], gains, pool_s0, ck, cv, cnt_prompt, sinks,
        w_in_p, w_pool_b, pscale, w_out_p, wr_split, rbias, tile=CHUNK, has_cache=True)

    def moe_tiles(route_p, route_s):
        per = MOE_TILE // CHUNK
        regrouped = jnp.transpose(route_s.reshape(-1, per, ROUTE_ROWS, CHUNK), (0, 2, 1, 3))
        return jnp.concatenate([route_p, regrouped.reshape(-1, ROUTE_ROWS, MOE_TILE)], axis=0)

    n_steps = n_tok // MOE_TILE
    cnt_td = jnp.concatenate(
        [tcnt_p[:, 0, :], tcnt_s[:, 0, :].reshape(-1, MOE_TILE // CHUNK, N_EXPERTS).sum(axis=1)],
        axis=0).astype(I32)
    c8 = (cnt_td + RUN_ROWS - 1) // RUN_ROWS * RUN_ROWS
    rows8 = jnp.sum(c8, axis=0)
    padded = (rows8 + EXPERT_ROWS - 1) // EXPERT_ROWS * EXPERT_ROWS
    pends = jnp.cumsum(padded)
    pstarts = pends - padded
    base = jnp.cumsum(cnt_td, axis=0) - cnt_td
    grow = pstarts[None, :] + jnp.cumsum(c8, axis=0) - c8
    lo8 = jnp.cumsum(c8, axis=1) - c8
    nch = c8 // RUN_ROWS
    nct = jnp.sum(nch, axis=1)
    off_tiles = (lo8 - base).astype(F32).reshape(n_steps, N_EXPERTS, 1)
    tables = (grow.reshape(-1).astype(I32), lo8.reshape(-1).astype(I32),
              nch.reshape(-1).astype(I32), nct.astype(I32))
    n_blocks = -(-(n_tok * TOP_K + n_steps * N_EXPERTS * (RUN_ROWS - 1) + N_EXPERTS * (EXPERT_ROWS - 1))
                 // EXPERT_ROWS)
    n_used = (pends[-1] // EXPERT_ROWS).astype(I32).reshape(1)
    zero_tables = ((pstarts + rows8).astype(I32), ((padded - rows8) // RUN_ROWS).astype(I32), n_used)

    h2_pf, h2_sf = h2_p.reshape(n_prompt, D_MODEL), h2_s.reshape(n_sample, D_MODEL)
    xs, slot_tiles = _dispatch_call(
        tables, zero_tables, h2_pf, h2_sf, moe_tiles(eidx_p, eidx_s), moe_tiles(rank_p, rank_s), off_tiles,
        moe_tiles(wts_p, wts_s), n_blocks * EXPERT_ROWS)
    ys = _experts_call((pstarts // EXPERT_ROWS).astype(I32), (padded // EXPERT_ROWS).astype(I32), n_used,
                       xs, w_gate[0], w_up[0], w_down[0])
    gate_groups = jnp.concatenate(
        [jnp.repeat(mod[:bsz, 5], seq // GATE_GROUP, axis=0),
         jnp.repeat(mod[bsz:, 5], dseq // GATE_GROUP, axis=0)], axis=0)
    y_p, y_s = _combine_call(
        tables, x1_p.reshape(n_prompt, D_MODEL), x1_s.reshape(n_sample, D_MODEL), h2_pf, h2_sf,
        gate_groups, gains, slot_tiles, ys,
        ws_gate[0].astype(BF16), ws_up[0].astype(BF16), ws_down[0].astype(BF16))

    n_hist = state_pool.shape[2]
    kv_shape = (1, -1, WINDOW, N_KV_HEADS, HEAD_DIM)
    return (y_p.reshape(bsz, seq, D_MODEL), y_s.reshape(dbsz, dseq, D_MODEL),
            pool_p[None, :, POOL_HIST_PAD - n_hist:], nk_p.reshape(kv_shape), nv_p.reshape(kv_shape),
            pool_s[None, :, POOL_HIST_PAD - n_hist:], nk_s.reshape(kv_shape), nv_s.reshape(kv_shape))
```

```python
import functools

import jax
import jax.numpy as jnp
import numpy as np
from jax import lax
from jax.experimental import pallas as pl
from jax.experimental.pallas import tpu as pltpu

F32 = jnp.float32
BF16 = jnp.bfloat16
I32 = jnp.int32

D_MODEL = 1024
CHUNK = 64
POOL_WIDTH = 512
POOL_WINDOWS = (2, 4, 8, 16)
POOL_GROUP_W = 128
POOL_HIST_PAD = 16
HEAD_DIM = 64
N_HEADS = 8
N_KV_HEADS = 2
GQ = N_HEADS // N_KV_HEADS
ATTN_WIDTH = N_HEADS * HEAD_DIM
KV_WIDTH = N_KV_HEADS * HEAD_DIM
IN_WIDTH = POOL_WIDTH + ATTN_WIDTH + 2 * KV_WIDTH
WINDOW = 128
N_EXPERTS = 64
TOP_K = 6
N_EXPERT_GROUPS = 8
GROUP_SIZE = N_EXPERTS // N_EXPERT_GROUPS
TOPK_GROUPS = 4
D_EXPERT = 256
D_SHARED = 256
ROUTED_SCALE = 2.5
NORM_EPS = 1e-6
NEG_BIG = -1e30

LANES = 128
SUBLANES = 8
ATTN_KEYS = 2 * LANES
ROUTE_ROWS = 8
EXPERT_ROWS = 256
MOE_TILE = 256
GATE_GROUP = MOE_TILE // SUBLANES
RUN_ROWS = 2 * SUBLANES
LOCAL_ROWS = 2560
MAX_CHUNKS = LOCAL_ROWS // RUN_ROWS
ROW_W = D_MODEL + LANES
VMEM_LIMIT = 56 * 1024 * 1024

assert TOP_K * MOE_TILE + N_EXPERTS * (RUN_ROWS - 1) <= LOCAL_ROWS

_NT = (((1,), (1,)), ((), ()))


def _rms(v):
    return v * lax.rsqrt(jnp.mean(v * v, axis=-1, keepdims=True) + NORM_EPS)


def _sigmoid(v):
    return 1.0 / (1.0 + jnp.exp(-v))


def _split3(v):
    hi = v.astype(BF16)
    r1 = v - hi.astype(F32)
    mid = r1.astype(BF16)
    lo = (r1 - mid.astype(F32)).astype(BF16)
    return hi, mid, lo


def _to_sublanes(rows):
    n = rows.shape[1]
    hi, mid, lo = _split3(rows)
    eye = (lax.broadcasted_iota(I32, (n, n), 0) == lax.broadcasted_iota(I32, (n, n), 1)).astype(BF16)
    return (lax.dot_general(eye, hi, _NT, preferred_element_type=F32)
            + lax.dot_general(eye, mid, _NT, preferred_element_type=F32)
            + lax.dot_general(eye, lo, _NT, preferred_element_type=F32))


def _ada_body(c_ref, w_ref, b_ref, o_ref):
    c = c_ref[...]
    s = c * _sigmoid(c)
    o_ref[...] = jnp.dot(s, w_ref[...], preferred_element_type=F32) + b_ref[...]


def _ada_call(c_all, w_ada, b_ada):
    nb = c_all.shape[0]
    n_out = w_ada.shape[1]
    tile = D_MODEL
    return pl.pallas_call(
        _ada_body,
        grid=(n_out // tile,),
        in_specs=[
            pl.BlockSpec((nb, D_MODEL), lambda j: (0, 0)),
            pl.BlockSpec((D_MODEL, tile), lambda j: (0, j)),
            pl.BlockSpec((1, tile), lambda j: (0, j)),
        ],
        out_specs=pl.BlockSpec((nb, tile), lambda j: (0, j)),
        out_shape=jax.ShapeDtypeStruct((nb, n_out), F32),
        compiler_params=pltpu.CompilerParams(dimension_semantics=("arbitrary",)),
        name="ada",
    )(c_all, w_ada, b_ada.reshape(1, n_out))


def _front_body(tile, has_cache,
                sinks_ref, x_ref, mod_ref, gains_ref, hp_ref, hk_ref, hv_ref, cnt0_ref,
                win_ref, wpool_ref, pscale_ref, wout_ref, wrt_ref, rbias_ref,
                x1_ref, h2_ref, npool_ref, nk_ref, nv_ref, eidx_ref, wts_ref, rank_ref, tcnt_ref,
                ubuf, khist, vhist, run):
    b = pl.program_id(0)
    i = pl.program_id(1)
    sub_q = min(tile, LANES)
    n_sub = tile // sub_q
    hist_keys = ATTN_KEYS - sub_q
    n_keys = hist_keys + tile

    @pl.when(i == 0)
    def _():
        ubuf[0:POOL_HIST_PAD, :] = hp_ref[0]
        khist[...] = hk_ref[0]
        vhist[...] = hv_ref[0]

    @pl.when((b == 0) & (i == 0))
    def _():
        run[...] = cnt0_ref[...]

    mod = mod_ref[0]
    gains = gains_ref[...]
    xt = x_ref[0]

    h = _rms(xt) * gains[0:1] * (1.0 + mod[1:2]) + mod[0:1]
    proj = jnp.dot(h.astype(BF16), win_ref[...], preferred_element_type=F32)
    u = proj[:, :POOL_WIDTH]
    o_k = POOL_WIDTH + ATTN_WIDTH
    k_new = proj[:, o_k:o_k + KV_WIDTH]
    v_new = proj[:, o_k + KV_WIDTH:]

    kw = jnp.concatenate([khist[...], k_new], axis=0)
    vw = jnp.concatenate([vhist[...], v_new], axis=0)
    kwb = kw.astype(BF16)
    vwb = vw.astype(BF16)
    qrow = lax.broadcasted_iota(I32, (sub_q, ATTN_KEYS), 0)
    kpos = lax.broadcasted_iota(I32, (sub_q, ATTN_KEYS), 1) - hist_keys
    qchunk = jnp.right_shift(qrow, 6)
    vis_band = (kpos >= CHUNK * (qchunk - 2)) & (kpos < CHUNK * (qchunk + 1))
    side0 = lax.broadcasted_iota(I32, (sub_q, LANES), 1) < HEAD_DIM
    attn_rows = []
    for r in range(n_sub):
        vis = vis_band
        if not has_cache:
            vis = vis & (kpos + (i * tile + r * sub_q) >= 0)
        kb = kwb[r * sub_q:r * sub_q + ATTN_KEYS]
        vb = vwb[r * sub_q:r * sub_q + ATTN_KEYS]
        blocks = []
        for j in range(GQ):
            qp = proj[r * sub_q:(r + 1) * sub_q, POOL_WIDTH + LANES * j:POOL_WIDTH + LANES * (j + 1)]
            qp = qp * (HEAD_DIM ** -0.5)
            outs = []
            for s in range(N_KV_HEADS):
                keep = side0 if s == 0 else jnp.logical_not(side0)
                qm = jnp.where(keep, qp, 0.0).astype(BF16)
                sc = lax.dot_general(qm, kb, _NT, preferred_element_type=F32)
                sc = jnp.where(vis, sc, NEG_BIG)
                sink = sinks_ref[j + GQ * s]
                m = jnp.maximum(jnp.max(sc, axis=-1, keepdims=True), sink)
                p = jnp.exp(sc - m)
                den = jnp.sum(p, axis=-1, keepdims=True) + jnp.exp(sink - m)
                p = (p / den).astype(BF16)
                outs.append(jnp.dot(p, vb, preferred_element_type=F32))
            blocks.append(jnp.where(side0, outs[0], outs[1]))
        attn_rows.append(jnp.concatenate(blocks, axis=-1))
    attn = attn_rows[0] if n_sub == 1 else jnp.concatenate(attn_rows, axis=0)

    ubuf[POOL_HIST_PAD:POOL_HIST_PAD + tile, :] = u
    if has_cache:
        seen = None
    else:
        seen = (lax.broadcasted_iota(I32, (tile, 1), 0) + i * tile + 1).astype(F32)
    pool_blocks = []
    for g, w in enumerate(POOL_WINDOWS):
        cols = slice(POOL_GROUP_W * g, POOL_GROUP_W * (g + 1))
        acc = u[:, cols]
        for s in range(1, w):
            acc = acc + ubuf[POOL_HIST_PAD - s:POOL_HIST_PAD - s + tile, cols]
        cnt = float(w) if seen is None else jnp.minimum(seen, float(w))
        dlt = acc / cnt - u[:, cols]
        pool_blocks.append(jnp.dot(dlt.astype(BF16), wpool_ref[g], preferred_element_type=F32))
    pool = jnp.concatenate(pool_blocks, axis=-1) * pscale_ref[...]

    mixin = jnp.concatenate([pool, attn], axis=-1).astype(BF16)
    mix = jnp.dot(mixin, wout_ref[...], preferred_element_type=F32)
    x1 = xt + mod[2:3] * (_rms(mix) * gains[1:2])
    x1_ref[0] = x1

    new_hist = ubuf[tile:tile + POOL_HIST_PAD, :]
    ubuf[0:POOL_HIST_PAD, :] = new_hist
    npool_ref[0] = new_hist
    khist[...] = kw[tile:, :]
    vhist[...] = vw[tile:, :]
    nk_ref[0] = kw[n_keys - WINDOW:, :]
    nv_ref[0] = vw[n_keys - WINDOW:, :]

    h2f = _rms(x1) * gains[2:3] * (1.0 + mod[4:5]) + mod[3:4]
    h2hi = h2f.astype(BF16)
    h2_ref[0] = h2hi
    h2lo = (h2f - h2hi.astype(F32)).astype(BF16)
    wrt = wrt_ref[...]
    part = lax.dot_general(wrt, h2hi, _NT, preferred_element_type=F32)
    logits = (part[:N_EXPERTS] + part[N_EXPERTS:]
              + lax.dot_general(wrt[:N_EXPERTS], h2lo, _NT, preferred_element_type=F32))
    scores = _sigmoid(logits)
    sel = scores + rbias_ref[...]

    sub_g = lax.broadcasted_iota(I32, (GROUP_SIZE, tile), 0).astype(F32)
    gscore = jnp.zeros((N_EXPERT_GROUPS, tile), F32)
    for gi in range(N_EXPERT_GROUPS):
        blk = sel[GROUP_SIZE * gi:GROUP_SIZE * (gi + 1), :]
        m1 = jnp.max(blk, axis=0, keepdims=True)
        i1 = jnp.min(jnp.where(blk == m1, sub_g, float(GROUP_SIZE)), axis=0, keepdims=True)
        m2 = jnp.max(jnp.where(sub_g == i1, -jnp.inf, blk), axis=0, keepdims=True)
        gscore = jnp.where(sub_g == gi, m1 + m2, gscore)
    chosen = jnp.zeros((N_EXPERT_GROUPS, tile), F32)
    for _ in range(TOPK_GROUPS):
        m = jnp.max(gscore, axis=0, keepdims=True)
        idx = jnp.min(jnp.where(gscore == m, sub_g, float(N_EXPERT_GROUPS)), axis=0, keepdims=True)
        pick = sub_g == idx
        chosen = jnp.where(pick, 1.0, chosen)
        gscore = jnp.where(pick, -jnp.inf, gscore)
    emask = jnp.concatenate(
        [jnp.broadcast_to(chosen[gi:gi + 1, :], (GROUP_SIZE, tile)) for gi in range(N_EXPERT_GROUPS)], axis=0)
    selm = jnp.where(emask > 0.0, sel, -jnp.inf)

    sub_e = lax.broadcasted_iota(I32, (N_EXPERTS, tile), 0).astype(F32)
    picks, idxs, raw_w = [], [], []
    for _ in range(TOP_K):
        m = jnp.max(selm, axis=0, keepdims=True)
        idx = jnp.min(jnp.where(selm == m, sub_e, float(N_EXPERTS)), axis=0, keepdims=True)
        pick = sub_e == idx
        raw_w.append(jnp.sum(jnp.where(pick, scores, 0.0), axis=0, keepdims=True))
        selm = jnp.where(pick, -jnp.inf, selm)
        picks.append(pick)
        idxs.append(idx)
    wsum = raw_w[0]
    for kk in range(1, TOP_K):
        wsum = wsum + raw_w[kk]

    onehot = jnp.zeros((N_EXPERTS, tile), F32)
    for kk in range(TOP_K):
        onehot = jnp.where(picks[kk], 1.0, onehot)
    onehot_b = onehot.astype(BF16)
    tri = (lax.broadcasted_iota(I32, (tile, tile), 0) < lax.broadcasted_iota(I32, (tile, tile), 1)).astype(BF16)
    before = jnp.dot(onehot_b, tri, preferred_element_type=F32) + run[:, 0:1]
    sub_r = lax.broadcasted_iota(I32, (ROUTE_ROWS, tile), 0)
    eidx_o = jnp.zeros((ROUTE_ROWS, tile), I32)
    wts_o = jnp.zeros((ROUTE_ROWS, tile), F32)
    rank_o = jnp.zeros((ROUTE_ROWS, tile), I32)
    for kk in range(TOP_K):
        rk = jnp.sum(jnp.where(picks[kk], before, 0.0), axis=0, keepdims=True).astype(I32)
        eidx_o = jnp.where(sub_r == kk, idxs[kk].astype(I32), eidx_o)
        wts_o = jnp.where(sub_r == kk, raw_w[kk] / wsum * ROUTED_SCALE, wts_o)
        rank_o = jnp.where(sub_r == kk, rk, rank_o)
    eidx_ref[0] = eidx_o
    wts_ref[0] = wts_o
    rank_ref[0] = rank_o
    tcnt_ref[0] = lax.dot_general(jnp.ones((ROUTE_ROWS, tile), BF16), onehot_b, _NT,
                                  preferred_element_type=F32)
    run[...] = run[...] + jnp.sum(onehot, axis=1, keepdims=True)


def _front_call(x, mod, gains, hist_pool, hist_k, hist_v, cnt0, sinks,
                w_in, w_pool, pool_scale, w_out, wr_t, rbias, *, tile, has_cache):
    bsz, seq, _ = x.shape
    n_tiles = seq // tile
    hist_keys = hist_k.shape[1]
    assert hist_keys == ATTN_KEYS - min(tile, LANES)
    body = functools.partial(_front_body, tile, has_cache)
    whole = lambda shape: pl.BlockSpec(shape, lambda b, i: (0,) * len(shape))
    per_b = lambda shape: pl.BlockSpec((1,) + shape, lambda b, i: (b,) + (0,) * len(shape))
    route = pl.BlockSpec((1, ROUTE_ROWS, tile), lambda b, i: (b * n_tiles + i, 0, 0))
    out_shape = [
        jax.ShapeDtypeStruct((bsz, seq, D_MODEL), F32),
        jax.ShapeDtypeStruct((bsz, seq, D_MODEL), BF16),
        jax.ShapeDtypeStruct((bsz, POOL_HIST_PAD, POOL_WIDTH), F32),
        jax.ShapeDtypeStruct((bsz, WINDOW, KV_WIDTH), F32),
        jax.ShapeDtypeStruct((bsz, WINDOW, KV_WIDTH), F32),
        jax.ShapeDtypeStruct((bsz * n_tiles, ROUTE_ROWS, tile), I32),
        jax.ShapeDtypeStruct((bsz * n_tiles, ROUTE_ROWS, tile), F32),
        jax.ShapeDtypeStruct((bsz * n_tiles, ROUTE_ROWS, tile), I32),
        jax.ShapeDtypeStruct((bsz * n_tiles, ROUTE_ROWS, N_EXPERTS), F32),
    ]
    return pl.pallas_call(
        body,
        grid=(bsz, n_tiles),
        in_specs=[
            pl.BlockSpec(memory_space=pltpu.SMEM),
            pl.BlockSpec((1, tile, D_MODEL), lambda b, i: (b, i, 0)),
            per_b((6, D_MODEL)),
            whole((4, D_MODEL)),
            per_b((POOL_HIST_PAD, POOL_WIDTH)),
            per_b((hist_keys, KV_WIDTH)),
            per_b((hist_keys, KV_WIDTH)),
            whole((N_EXPERTS, LANES)),
            whole((D_MODEL, IN_WIDTH)),
            whole((len(POOL_WINDOWS), POOL_GROUP_W, POOL_GROUP_W)),
            whole((1, POOL_WIDTH)),
            whole((D_MODEL, D_MODEL)),
            whole((2 * N_EXPERTS, D_MODEL)),
            whole((N_EXPERTS, 1)),
        ],
        out_specs=[
            pl.BlockSpec((1, tile, D_MODEL), lambda b, i: (b, i, 0)),
            pl.BlockSpec((1, tile, D_MODEL), lambda b, i: (b, i, 0)),
            per_b((POOL_HIST_PAD, POOL_WIDTH)),
            per_b((WINDOW, KV_WIDTH)),
            per_b((WINDOW, KV_WIDTH)),
            route, route, route,
            pl.BlockSpec((1, ROUTE_ROWS, N_EXPERTS), lambda b, i: (b * n_tiles + i, 0, 0)),
        ],
        out_shape=out_shape,
        scratch_shapes=[
            pltpu.VMEM((tile + POOL_HIST_PAD, POOL_WIDTH), F32),
            pltpu.VMEM((hist_keys, KV_WIDTH), F32),
            pltpu.VMEM((hist_keys, KV_WIDTH), F32),
            pltpu.VMEM((N_EXPERTS, LANES), F32),
        ],
        compiler_params=pltpu.CompilerParams(
            dimension_semantics=("arbitrary", "arbitrary"), vmem_limit_bytes=VMEM_LIMIT),
        name="front_cached" if has_cache else "front_prompt",
    )(sinks, x, mod, gains, hist_pool, hist_k, hist_v, cnt0,
      w_in, w_pool, pool_scale, w_out, wr_t, rbias)


def _run_copy(src, dst, s_row, d_row, sem):
    return pltpu.make_async_copy(src.at[pl.ds(s_row, RUN_ROWS)], dst.at[pl.ds(d_row, RUN_ROWS)], sem)


def _for_each_run_chunk(step, lrow_ref, grow_ref, nct_ref, fn):
    def per_chunk(j, carry):
        idx = step * MAX_CHUNKS + j
        fn(pl.multiple_of(lrow_ref[idx], RUN_ROWS), pl.multiple_of(grow_ref[idx], RUN_ROWS))
        return carry

    lax.fori_loop(0, nct_ref[step], per_chunk, 0)


def _dispatch_body(n_prompt_steps, lrow_ref, grow_ref, nct_ref, zrow_ref, znch_ref, nused_ref,
                   h2p_ref, h2s_ref, eidx_ref, rank_ref, off_ref, wts_ref, xs_out, slot_out, loc, zrows, sem):
    i = pl.program_id(0)
    n_steps = pl.num_programs(0)
    par = lax.rem(i, 2)
    h2 = jnp.where(i < n_prompt_steps, h2p_ref[...], h2s_ref[...])
    eidx = eidx_ref[0]
    rank = rank_ref[0]
    wts = wts_ref[0]
    off = off_ref[0]
    expert_id = lax.broadcasted_iota(I32, (N_EXPERTS, MOE_TILE), 0)
    sub_r = lax.broadcasted_iota(I32, (ROUTE_ROWS, MOE_TILE), 0)
    slots = []
    slot_o = jnp.zeros((ROUTE_ROWS, MOE_TILE), I32)
    for kk in range(TOP_K):
        mine = jnp.sum(jnp.where(expert_id == eidx[kk:kk + 1, :], off, 0.0), axis=0, keepdims=True)
        slots.append(rank[kk:kk + 1, :] + mine.astype(I32))
        slot_o = jnp.where(sub_r == kk, slots[kk], slot_o)
    slot_out[0] = slot_o
    row_id = lax.broadcasted_iota(I32, (LOCAL_ROWS, MOE_TILE), 0)
    sel = jnp.zeros((LOCAL_ROWS, MOE_TILE), F32)
    selw = jnp.zeros((LOCAL_ROWS, MOE_TILE), F32)
    for kk in range(TOP_K):
        hit = row_id == slots[kk]
        sel = jnp.where(hit, 1.0, sel)
        selw = jnp.where(hit, wts[kk:kk + 1, :], selw)
    sorted_rows = jnp.dot(sel.astype(BF16), h2, preferred_element_type=F32)
    w_slot = jnp.sum(selw, axis=1, keepdims=True)
    buf = loc.at[par]
    buf[:, :D_MODEL] = sorted_rows.astype(BF16)
    w_hi, w_mid, w_lo = _split3(w_slot)
    lane = lax.broadcasted_iota(I32, (LOCAL_ROWS, LANES), 1)
    w_lanes = jnp.where(lane == 0, w_hi.astype(F32),
                        jnp.where(lane == 1, w_mid.astype(F32), jnp.where(lane == 2, w_lo.astype(F32), 0.0)))
    buf[:, D_MODEL:] = w_lanes.astype(BF16)

    def drain(n_chunks):
        def one(c, carry):
            _run_copy(buf, xs_out, 0, 0, sem).wait()
            return carry
        lax.fori_loop(0, n_chunks, one, 0)

    @pl.when(i > 0)
    def _():
        drain(nct_ref[i - 1])

    _for_each_run_chunk(i, lrow_ref, grow_ref, nct_ref,
                        lambda lrow, grow: _run_copy(buf, xs_out, lrow, grow, sem).start())

    @pl.when(i == n_steps - 1)
    def _():
        zrows[...] = jnp.zeros_like(zrows)

        def per_expert(e, total):
            def per_chunk(c, carry):
                _run_copy(zrows, xs_out, 0, pl.multiple_of(zrow_ref[e] + RUN_ROWS * c, RUN_ROWS), sem).start()
                return carry
            lax.fori_loop(0, znch_ref[e], per_chunk, 0)
            return total + znch_ref[e]

        n_zero = lax.fori_loop(0, N_EXPERTS, per_expert, 0)
        drain(nct_ref[i] + n_zero)

        def block_copy(blk):
            return pltpu.make_async_copy(
                zrows, xs_out.at[pl.ds(pl.multiple_of(blk * EXPERT_ROWS, EXPERT_ROWS), EXPERT_ROWS)], sem)

        n_blocks = xs_out.shape[0] // EXPERT_ROWS

        def start_block(blk, carry):
            block_copy(blk).start()
            return carry

        def wait_block(blk, carry):
            block_copy(blk).wait()
            return carry

        lax.fori_loop(nused_ref[0], n_blocks, start_block, 0)
        lax.fori_loop(nused_ref[0], n_blocks, wait_block, 0)


def _dispatch_call(tables, zero_tables, h2_p, h2_s, eidx_tiles, rank_tiles, off_tiles, wts_tiles, n_rows):
    n_prompt_steps = h2_p.shape[0] // MOE_TILE
    n_steps = n_prompt_steps + h2_s.shape[0] // MOE_TILE
    route = lambda: pl.BlockSpec((1, ROUTE_ROWS, MOE_TILE), lambda i, *_: (i, 0, 0))
    grid_spec = pltpu.PrefetchScalarGridSpec(
        num_scalar_prefetch=6,
        grid=(n_steps,),
        in_specs=[
            pl.BlockSpec((MOE_TILE, D_MODEL), lambda i, *_: (jnp.minimum(i, n_prompt_steps - 1), 0)),
            pl.BlockSpec((MOE_TILE, D_MODEL), lambda i, *_: (jnp.maximum(i - n_prompt_steps, 0), 0)),
            route(), route(),
            pl.BlockSpec((1, N_EXPERTS, 1), lambda i, *_: (i, 0, 0)),
            route(),
        ],
        out_specs=[pl.BlockSpec(memory_space=pl.ANY), route()],
        scratch_shapes=[
            pltpu.VMEM((2, LOCAL_ROWS, ROW_W), BF16),
            pltpu.VMEM((EXPERT_ROWS, ROW_W), BF16),
            pltpu.SemaphoreType.DMA,
        ],
    )
    return pl.pallas_call(
        functools.partial(_dispatch_body, n_prompt_steps),
        grid_spec=grid_spec,
        out_shape=[jax.ShapeDtypeStruct((n_rows, ROW_W), BF16),
                   jax.ShapeDtypeStruct((n_steps, ROUTE_ROWS, MOE_TILE), I32)],
        compiler_params=pltpu.CompilerParams(
            dimension_semantics=("arbitrary",), vmem_limit_bytes=VMEM_LIMIT),
        name="dispatch",
    )(*tables, *zero_tables, h2_p, h2_s, eidx_tiles, rank_tiles, off_tiles, wts_tiles)


def _block_rows(ref, blk):
    return ref.at[pl.ds(pl.multiple_of(blk * EXPERT_ROWS, EXPERT_ROWS), EXPERT_ROWS)]


def _experts_body(first_ref, nblk_ref, nused_ref, xs_hbm, wg_ref, wu_ref, wd_ref, ys_hbm,
                  xbuf, ybuf, wgb, wub, wdb, isem, osem):
    e = pl.program_id(0)
    first = first_ref[e]
    n_blk = nblk_ref[e]

    def fetch(block, slot):
        return pltpu.make_async_copy(_block_rows(xs_hbm, block), xbuf.at[slot], isem.at[slot])

    def in_copy(blk, slot):
        return fetch(first + blk, slot)

    def out_copy(blk, slot):
        return pltpu.make_async_copy(ybuf.at[slot], _block_rows(ys_hbm, first + blk), osem.at[slot])

    @pl.when((e == 0) & (n_blk > 0))
    def _():
        in_copy(0, 0).start()

    @pl.when(n_blk > 0)
    def _():
        wgb[...] = wg_ref[0].astype(BF16)
        wub[...] = wu_ref[0].astype(BF16)
        wdb[...] = wd_ref[0].astype(BF16)

        def one_block(blk, carry):
            slot = lax.rem(blk, 2)

            @pl.when(blk + 1 < n_blk)
            def _():
                in_copy(blk + 1, 1 - slot).start()

            in_copy(blk, slot).wait()

            @pl.when(blk >= 2)
            def _():
                out_copy(blk - 2, slot).wait()

            xrow = xbuf[slot]
            x = xrow[:, :D_MODEL]
            w_parts = xrow[:, D_MODEL:].astype(F32)
            w_row = w_parts[:, 0:1] + w_parts[:, 1:2] + w_parts[:, 2:3]
            g = jnp.dot(x, wgb[...], preferred_element_type=F32)
            u = jnp.dot(x, wub[...], preferred_element_type=F32)
            a = (g * _sigmoid(g) * u).astype(BF16)
            y = jnp.dot(a, wdb[...], preferred_element_type=F32) * w_row
            ybuf[slot] = y.astype(BF16)
            out_copy(blk, slot).start()
            return carry

        lax.fori_loop(0, n_blk, one_block, 0)

        @pl.when(n_blk >= 2)
        def _():
            out_copy(n_blk - 2, lax.rem(n_blk, 2)).wait()
        out_copy(n_blk - 1, lax.rem(n_blk - 1, 2)).wait()

    nxt = jnp.minimum(e + 1, pl.num_programs(0) - 1)

    @pl.when((e + 1 < pl.num_programs(0)) & (nblk_ref[nxt] > 0))
    def _():
        fetch(first_ref[nxt], 0).start()

    @pl.when(e == pl.num_programs(0) - 1)
    def _():
        ybuf[0] = jnp.zeros((EXPERT_ROWS, D_MODEL), BF16)
        n_blocks = ys_hbm.shape[0] // EXPERT_ROWS

        def tail_copy(blk):
            return pltpu.make_async_copy(ybuf.at[0], _block_rows(ys_hbm, blk), osem.at[0])

        def start_block(blk, carry):
            tail_copy(blk).start()
            return carry

        def wait_block(blk, carry):
            tail_copy(blk).wait()
            return carry

        lax.fori_loop(nused_ref[0], n_blocks, start_block, 0)
        lax.fori_loop(nused_ref[0], n_blocks, wait_block, 0)


def _experts_call(first_block, n_expert_blocks, n_used, xs, w_gate, w_up, w_down):
    grid_spec = pltpu.PrefetchScalarGridSpec(
        num_scalar_prefetch=3,
        grid=(N_EXPERTS,),
        in_specs=[
            pl.BlockSpec(memory_space=pl.ANY),
            pl.BlockSpec((1, D_MODEL, D_EXPERT), lambda e, *_: (e, 0, 0)),
            pl.BlockSpec((1, D_MODEL, D_EXPERT), lambda e, *_: (e, 0, 0)),
            pl.BlockSpec((1, D_EXPERT, D_MODEL), lambda e, *_: (e, 0, 0)),
        ],
        out_specs=pl.BlockSpec(memory_space=pl.ANY),
        scratch_shapes=[
            pltpu.VMEM((2, EXPERT_ROWS, ROW_W), BF16),
            pltpu.VMEM((2, EXPERT_ROWS, D_MODEL), BF16),
            pltpu.VMEM((D_MODEL, D_EXPERT), BF16),
            pltpu.VMEM((D_MODEL, D_EXPERT), BF16),
            pltpu.VMEM((D_EXPERT, D_MODEL), BF16),
            pltpu.SemaphoreType.DMA((2,)),
            pltpu.SemaphoreType.DMA((2,)),
        ],
    )
    return pl.pallas_call(
        _experts_body,
        grid_spec=grid_spec,
        out_shape=jax.ShapeDtypeStruct((xs.shape[0], D_MODEL), BF16),
        compiler_params=pltpu.CompilerParams(
            dimension_semantics=("arbitrary",), vmem_limit_bytes=VMEM_LIMIT),
        name="experts",
    )(first_block, n_expert_blocks, n_used, xs, w_gate, w_up, w_down)


def _combine_body(n_prompt_steps, lrow_ref, grow_ref, nct_ref,
                  x1p_ref, x1s_ref, h2p_ref, h2s_ref, gate_ref, gains_ref, slot_ref, ys_hbm,
                  wsg_ref, wsu_ref, wsd_ref, outp_ref, outs_ref, gath, sem):
    i = pl.program_id(0)
    n_steps = pl.num_programs(0)
    par = lax.rem(i, 2)

    def fetch(step, slot):
        buf = gath.at[slot]
        _for_each_run_chunk(step, lrow_ref, grow_ref, nct_ref,
                            lambda lrow, grow: _run_copy(ys_hbm, buf, grow, lrow, sem.at[slot]).start())

    @pl.when(i == 0)
    def _():
        gath[...] = jnp.zeros_like(gath)
        fetch(0, 0)

    @pl.when(i + 1 < n_steps)
    def _():
        fetch(i + 1, 1 - par)

    is_prompt = i < n_prompt_steps
    h2 = jnp.where(is_prompt, h2p_ref[...], h2s_ref[...])
    g = jnp.dot(h2, wsg_ref[...], preferred_element_type=F32)
    u = jnp.dot(h2, wsu_ref[...], preferred_element_type=F32)
    a = (g * _sigmoid(g) * u).astype(BF16)
    ffn = jnp.dot(a, wsd_ref[...], preferred_element_type=F32)

    slot_cols = _to_sublanes(slot_ref[0].astype(F32))
    col_id = lax.broadcasted_iota(I32, (MOE_TILE, LOCAL_ROWS), 1).astype(F32)
    take = jnp.zeros((MOE_TILE, LOCAL_ROWS), F32)
    for kk in range(TOP_K):
        take = jnp.where(col_id == slot_cols[:, kk:kk + 1], 1.0, take)

    buf = gath.at[par]

    def one(c, carry):
        _run_copy(ys_hbm, buf, 0, 0, sem.at[par]).wait()
        return carry

    lax.fori_loop(0, nct_ref[i], one, 0)

    ffn = ffn + jnp.dot(take.astype(BF16), buf[...], preferred_element_type=F32)

    quarter = MOE_TILE // gate_ref.shape[0]
    gate = jnp.concatenate(
        [jnp.broadcast_to(gate_ref[q:q + 1, :], (quarter, D_MODEL)) for q in range(gate_ref.shape[0])], axis=0)
    x1 = jnp.where(is_prompt, x1p_ref[...], x1s_ref[...])
    out = x1 + gate * (_rms(ffn) * gains_ref[3:4, :])

    @pl.when(is_prompt)
    def _():
        outp_ref[...] = out

    @pl.when(jnp.logical_not(is_prompt))
    def _():
        outs_ref[...] = out


def _combine_call(tables, x1_p, x1_s, h2_p, h2_s, gate_groups, gains, slot_tiles, ys,
                  ws_gate, ws_up, ws_down):
    n_prompt, n_sample = x1_p.shape[0], x1_s.shape[0]
    n_prompt_steps = n_prompt // MOE_TILE
    n_steps = n_prompt_steps + n_sample // MOE_TILE
    groups_per_tile = gate_groups.shape[0] // n_steps
    tok_p = lambda: pl.BlockSpec((MOE_TILE, D_MODEL), lambda i, *_: (jnp.minimum(i, n_prompt_steps - 1), 0))
    tok_s = lambda: pl.BlockSpec((MOE_TILE, D_MODEL), lambda i, *_: (jnp.maximum(i - n_prompt_steps, 0), 0))
    whole = lambda shape: pl.BlockSpec(shape, lambda i, *_: (0,) * len(shape))
    grid_spec = pltpu.PrefetchScalarGridSpec(
        num_scalar_prefetch=3,
        grid=(n_steps,),
        in_specs=[
            tok_p(), tok_s(), tok_p(), tok_s(),
            pl.BlockSpec((groups_per_tile, D_MODEL), lambda i, *_: (i, 0)),
            whole((4, D_MODEL)),
            pl.BlockSpec((1, ROUTE_ROWS, MOE_TILE), lambda i, *_: (i, 0, 0)),
            pl.BlockSpec(memory_space=pl.ANY),
            whole((D_MODEL, D_SHARED)), whole((D_MODEL, D_SHARED)), whole((D_SHARED, D_MODEL)),
        ],
        out_specs=[tok_p(), tok_s()],
        scratch_shapes=[
            pltpu.VMEM((2, LOCAL_ROWS, D_MODEL), BF16),
            pltpu.SemaphoreType.DMA((2,)),
        ],
    )
    return pl.pallas_call(
        functools.partial(_combine_body, n_prompt_steps),
        grid_spec=grid_spec,
        out_shape=[jax.ShapeDtypeStruct((n_prompt, D_MODEL), F32),
                   jax.ShapeDtypeStruct((n_sample, D_MODEL), F32)],
        compiler_params=pltpu.CompilerParams(
            dimension_semantics=("arbitrary",), vmem_limit_bytes=VMEM_LIMIT),
        name="combine",
    )(*tables, x1_p, x1_s, h2_p, h2_s, gate_groups, gains, slot_tiles, ys, ws_gate, ws_up, ws_down)


def kernel(x_prompt, x_sample, c_prompt, c_sample, state_pool, cache_k, cache_v, w_ada, b_ada, norm_gains,
           w_in, w_pool, pool_scale, attn_sinks, w_out, w_router, router_bias, w_gate, w_up, w_down,
           ws_gate, ws_up, ws_down):
    assert w_ada.shape[0] == 1, "single-layer kernel"
    bsz, seq, _ = x_prompt.shape
    dbsz, dseq, _ = x_sample.shape
    n_prompt, n_sample = bsz * seq, dbsz * dseq
    n_tok = n_prompt + n_sample
    assert dseq == CHUNK and seq % MOE_TILE == 0 and n_sample % MOE_TILE == 0

    w_in0, w_out0 = w_in[0], w_out[0]
    wq = w_in0[:, POOL_WIDTH:POOL_WIDTH + ATTN_WIDTH].reshape(D_MODEL, N_KV_HEADS, GQ, HEAD_DIM)
    wq = jnp.transpose(wq, (0, 2, 1, 3)).reshape(D_MODEL, ATTN_WIDTH)
    w_in_p = jnp.concatenate(
        [w_in0[:, :POOL_WIDTH], wq, w_in0[:, POOL_WIDTH + ATTN_WIDTH:]], axis=1).astype(BF16)
    wo = w_out0[POOL_WIDTH:].reshape(N_KV_HEADS, GQ, HEAD_DIM, D_MODEL)
    wo = jnp.transpose(wo, (1, 0, 2, 3)).reshape(ATTN_WIDTH, D_MODEL)
    w_out_p = jnp.concatenate([w_out0[:POOL_WIDTH], wo], axis=0).astype(BF16)
    w_pool_b = w_pool[0].astype(BF16)
    pscale = pool_scale[0].reshape(1, POOL_WIDTH)
    wr_t = w_router[0].T
    wr_hi = wr_t.astype(BF16)
    wr_lo = (wr_t - wr_hi.astype(F32)).astype(BF16)
    wr_split = jnp.concatenate([wr_hi, wr_lo], axis=0)
    rbias = router_bias[0].reshape(N_EXPERTS, 1)
    gains = norm_gains[0]
    sinks = attn_sinks[0]

    mod = _ada_call(jnp.concatenate([c_prompt, c_sample], axis=0), w_ada[0], b_ada[0])
    mod = mod.reshape(bsz + dbsz, 6, D_MODEL)

    zeros_pool = jnp.zeros((bsz, POOL_HIST_PAD, POOL_WIDTH), F32)
    zeros_kv = jnp.zeros((bsz, WINDOW, KV_WIDTH), F32)
    cnt0 = jnp.zeros((N_EXPERTS, LANES), F32)
    (x1_p, h2_p, pool_p, nk_p, nv_p, eidx_p, wts_p, rank_p, tcnt_p) = _front_call(
        x_prompt, mod[:bsz], gains, zeros_pool, zeros_kv, zeros_kv, cnt0, sinks,
        w_in_p, w_pool_b, pscale, w_out_p, wr_split, rbias, tile=MOE_TILE, has_cache=False)
    cnt_prompt = jnp.broadcast_to(jnp.sum(tcnt_p[:, 0, :], axis=0)[:, None], (N_EXPERTS, LANES))

    hist_pad = ATTN_KEYS - CHUNK - WINDOW
    pool_s0 = jnp.pad(state_pool[0], ((0, 0), (POOL_HIST_PAD - state_pool.shape[2], 0), (0, 0)))
    ck = jnp.pad(cache_k[0].reshape(dbsz, WINDOW, KV_WIDTH), ((0, 0), (hist_pad, 0), (0, 0)))
    cv = jnp.pad(cache_v[0].reshape(dbsz, WINDOW, KV_WIDTH), ((0, 0), (hist_pad, 0), (0, 0)))
    (x1_s, h2_s, pool_s, nk_s, nv_s, eidx_s, wts_s, rank_s, tcnt_s) = _front_call(
        x_sample, mod[bsz:], gains, pool_s0, ck, cv, cnt_prompt, sinks,
        w_in_p, w_pool_b, pscale, w_out_p, wr_split, rbias, tile=CHUNK, has_cache=True)

    def moe_tiles(route_p, route_s):
        per = MOE_TILE // CHUNK
        regrouped = jnp.transpose(route_s.reshape(-1, per, ROUTE_ROWS, CHUNK), (0, 2, 1, 3))
        return jnp.concatenate([route_p, regrouped.reshape(-1, ROUTE_ROWS, MOE_TILE)], axis=0)

    n_steps = n_tok // MOE_TILE
    cnt_td = jnp.concatenate(
        [tcnt_p[:, 0, :], tcnt_s[:, 0, :].reshape(-1, MOE_TILE // CHUNK, N_EXPERTS).sum(axis=1)],
        axis=0).astype(I32)
    c8 = (cnt_td + RUN_ROWS - 1) // RUN_ROWS * RUN_ROWS
    rows8 = jnp.sum(c8, axis=0)
    padded = (rows8 + EXPERT_ROWS - 1) // EXPERT_ROWS * EXPERT_ROWS
    pends = jnp.cumsum(padded)
    pstarts = pends - padded
    base = jnp.cumsum(cnt_td, axis=0) - cnt_td
    grow = pstarts[None, :] + jnp.cumsum(c8, axis=0) - c8
    lo8 = jnp.cumsum(c8, axis=1) - c8
    nch = c8 // RUN_ROWS
    nct = jnp.sum(nch, axis=1)
    off_tiles = (lo8 - base).astype(F32).reshape(n_steps, N_EXPERTS, 1)
    cum = jnp.cumsum(nch, axis=1)
    j = jnp.arange(MAX_CHUNKS, dtype=I32)
    owner = jnp.sum(j[None, :, None] >= cum[:, None, :], axis=-1)
    owns = owner[:, :, None] == jnp.arange(N_EXPERTS, dtype=I32)[None, None, :]
    of_owner = lambda tab: jnp.sum(jnp.where(owns, tab[:, None, :], 0), axis=-1)
    within = RUN_ROWS * (j[None, :] - of_owner(cum - nch))
    tables = ((of_owner(lo8) + within).reshape(-1).astype(I32),
              (of_owner(grow) + within).reshape(-1).astype(I32), nct.astype(I32))
    n_blocks = -(-(n_tok * TOP_K + n_steps * N_EXPERTS * (RUN_ROWS - 1) + N_EXPERTS * (EXPERT_ROWS - 1))
                 // EXPERT_ROWS)
    n_used = (pends[-1] // EXPERT_ROWS).astype(I32).reshape(1)
    zero_tables = ((pstarts + rows8).astype(I32), ((padded - rows8) // RUN_ROWS).astype(I32), n_used)

    h2_pf, h2_sf = h2_p.reshape(n_prompt, D_MODEL), h2_s.reshape(n_sample, D_MODEL)
    xs, slot_tiles = _dispatch_call(
        tables, zero_tables, h2_pf, h2_sf, moe_tiles(eidx_p, eidx_s), moe_tiles(rank_p, rank_s), off_tiles,
        moe_tiles(wts_p, wts_s), n_blocks * EXPERT_ROWS)
    ys = _experts_call((pstarts // EXPERT_ROWS).astype(I32), (padded // EXPERT_ROWS).astype(I32), n_used,
                       xs, w_gate[0], w_up[0], w_down[0])
    gate_groups = jnp.concatenate(
        [jnp.repeat(mod[:bsz, 5], seq // GATE_GROUP, axis=0),
         jnp.repeat(mod[bsz:, 5], dseq // GATE_GROUP, axis=0)], axis=0)
    y_p, y_s = _combine_call(
        tables, x1_p.reshape(n_prompt, D_MODEL), x1_s.reshape(n_sample, D_MODEL), h2_pf, h2_sf,
        gate_groups, gains, slot_tiles, ys,
        ws_gate[0].astype(BF16), ws_up[0].astype(BF16), ws_down[0].astype(BF16))

    n_hist = state_pool.shape[2]
    kv_shape = (1, -1, WINDOW, N_KV_HEADS, HEAD_DIM)
    return (y_p.reshape(bsz, seq, D_MODEL), y_s.reshape(dbsz, dseq, D_MODEL),
            pool_p[None, :, POOL_HIST_PAD - n_hist:], nk_p.reshape(kv_shape), nv_p.reshape(kv_shape),
            pool_s[None, :, POOL_HIST_PAD - n_hist:], nk_s.reshape(kv_shape), nv_s.reshape(kv_shape))
```

```python
import functools

import jax
import jax.numpy as jnp
import numpy as np
from jax import lax
from jax.experimental import pallas as pl
from jax.experimental.pallas import tpu as pltpu

F32 = jnp.float32
BF16 = jnp.bfloat16
I32 = jnp.int32

D_MODEL = 1024
CHUNK = 64
POOL_WIDTH = 512
POOL_WINDOWS = (2, 4, 8, 16)
POOL_GROUP_W = 128
POOL_HIST_PAD = 16
HEAD_DIM = 64
N_HEADS = 8
N_KV_HEADS = 2
GQ = N_HEADS // N_KV_HEADS
ATTN_WIDTH = N_HEADS * HEAD_DIM
KV_WIDTH = N_KV_HEADS * HEAD_DIM
IN_WIDTH = POOL_WIDTH + ATTN_WIDTH + 2 * KV_WIDTH
WINDOW = 128
N_EXPERTS = 64
TOP_K = 6
N_EXPERT_GROUPS = 8
GROUP_SIZE = N_EXPERTS // N_EXPERT_GROUPS
TOPK_GROUPS = 4
D_EXPERT = 256
D_SHARED = 256
ROUTED_SCALE = 2.5
NORM_EPS = 1e-6
NEG_BIG = -1e30

LANES = 128
SUBLANES = 8
ATTN_KEYS = 2 * LANES
ROUTE_ROWS = 8
EXPERT_ROWS = 256
BIG_BLOCKS = 4
BIG_ROWS = BIG_BLOCKS * EXPERT_ROWS
MOE_TILE = 256
GATE_GROUP = MOE_TILE // SUBLANES
RUN_ROWS = 2 * SUBLANES
LOCAL_ROWS = 2560
MAX_CHUNKS = LOCAL_ROWS // RUN_ROWS
ROW_W = D_MODEL + LANES
VMEM_LIMIT = 56 * 1024 * 1024

assert TOP_K * MOE_TILE + N_EXPERTS * (RUN_ROWS - 1) <= LOCAL_ROWS

_NT = (((1,), (1,)), ((), ()))


def _rms(v):
    return v * lax.rsqrt(jnp.mean(v * v, axis=-1, keepdims=True) + NORM_EPS)


def _sigmoid(v):
    return 1.0 / (1.0 + jnp.exp(-v))


def _split3(v):
    hi = v.astype(BF16)
    r1 = v - hi.astype(F32)
    mid = r1.astype(BF16)
    lo = (r1 - mid.astype(F32)).astype(BF16)
    return hi, mid, lo


def _to_sublanes(rows):
    n = rows.shape[1]
    hi, mid, lo = _split3(rows)
    eye = (lax.broadcasted_iota(I32, (n, n), 0) == lax.broadcasted_iota(I32, (n, n), 1)).astype(BF16)
    return (lax.dot_general(eye, hi, _NT, preferred_element_type=F32)
            + lax.dot_general(eye, mid, _NT, preferred_element_type=F32)
            + lax.dot_general(eye, lo, _NT, preferred_element_type=F32))


def _ada_body(c_ref, w_ref, b_ref, o_ref):
    c = c_ref[...]
    s = c * _sigmoid(c)
    o_ref[...] = jnp.dot(s, w_ref[...], preferred_element_type=F32) + b_ref[...]


def _ada_call(c_all, w_ada, b_ada):
    nb = c_all.shape[0]
    n_out = w_ada.shape[1]
    tile = D_MODEL
    return pl.pallas_call(
        _ada_body,
        grid=(n_out // tile,),
        in_specs=[
            pl.BlockSpec((nb, D_MODEL), lambda j: (0, 0)),
            pl.BlockSpec((D_MODEL, tile), lambda j: (0, j)),
            pl.BlockSpec((1, tile), lambda j: (0, j)),
        ],
        out_specs=pl.BlockSpec((nb, tile), lambda j: (0, j)),
        out_shape=jax.ShapeDtypeStruct((nb, n_out), F32),
        compiler_params=pltpu.CompilerParams(dimension_semantics=("arbitrary",)),
        name="ada",
    )(c_all, w_ada, b_ada.reshape(1, n_out))


def _front_body(tile, has_cache,
                sinks_ref, x_ref, mod_ref, gains_ref, hp_ref, hk_ref, hv_ref, cnt0_ref,
                win_ref, wpool_ref, pscale_ref, wout_ref, wrt_ref, rbias_ref,
                x1_ref, h2_ref, npool_ref, nk_ref, nv_ref, eidx_ref, wts_ref, rank_ref, tcnt_ref,
                ubuf, khist, vhist, run):
    b = pl.program_id(0)
    i = pl.program_id(1)
    sub_q = min(tile, LANES)
    n_sub = tile // sub_q
    hist_keys = ATTN_KEYS - sub_q
    n_keys = hist_keys + tile

    @pl.when(i == 0)
    def _():
        ubuf[0:POOL_HIST_PAD, :] = hp_ref[0]
        khist[...] = hk_ref[0]
        vhist[...] = hv_ref[0]

    @pl.when((b == 0) & (i == 0))
    def _():
        run[...] = cnt0_ref[...]

    mod = mod_ref[0]
    gains = gains_ref[...]
    xt = x_ref[0]

    h = _rms(xt) * gains[0:1] * (1.0 + mod[1:2]) + mod[0:1]
    proj = jnp.dot(h.astype(BF16), win_ref[...], preferred_element_type=F32)
    u = proj[:, :POOL_WIDTH]
    o_k = POOL_WIDTH + ATTN_WIDTH
    k_new = proj[:, o_k:o_k + KV_WIDTH]
    v_new = proj[:, o_k + KV_WIDTH:]

    kw = jnp.concatenate([khist[...], k_new], axis=0)
    vw = jnp.concatenate([vhist[...], v_new], axis=0)
    kwb = kw.astype(BF16)
    vwb = vw.astype(BF16)
    qrow = lax.broadcasted_iota(I32, (sub_q, ATTN_KEYS), 0)
    kpos = lax.broadcasted_iota(I32, (sub_q, ATTN_KEYS), 1) - hist_keys
    qchunk = jnp.right_shift(qrow, 6)
    vis_band = (kpos >= CHUNK * (qchunk - 2)) & (kpos < CHUNK * (qchunk + 1))
    side0 = lax.broadcasted_iota(I32, (sub_q, LANES), 1) < HEAD_DIM
    attn_rows = []
    for r in range(n_sub):
        vis = vis_band
        if not has_cache:
            vis = vis & (kpos + (i * tile + r * sub_q) >= 0)
        kb = kwb[r * sub_q:r * sub_q + ATTN_KEYS]
        vb = vwb[r * sub_q:r * sub_q + ATTN_KEYS]
        blocks = []
        for j in range(GQ):
            qp = proj[r * sub_q:(r + 1) * sub_q, POOL_WIDTH + LANES * j:POOL_WIDTH + LANES * (j + 1)]
            qp = qp * (HEAD_DIM ** -0.5)
            outs = []
            for s in range(N_KV_HEADS):
                keep = side0 if s == 0 else jnp.logical_not(side0)
                qm = jnp.where(keep, qp, 0.0).astype(BF16)
                sc = lax.dot_general(qm, kb, _NT, preferred_element_type=F32)
                sc = jnp.where(vis, sc, NEG_BIG)
                sink = sinks_ref[j + GQ * s]
                m = jnp.maximum(jnp.max(sc, axis=-1, keepdims=True), sink)
                p = jnp.exp(sc - m)
                den = jnp.sum(p, axis=-1, keepdims=True) + jnp.exp(sink - m)
                p = (p / den).astype(BF16)
                outs.append(jnp.dot(p, vb, preferred_element_type=F32))
            blocks.append(jnp.where(side0, outs[0], outs[1]))
        attn_rows.append(jnp.concatenate(blocks, axis=-1))
    attn = attn_rows[0] if n_sub == 1 else jnp.concatenate(attn_rows, axis=0)

    ubuf[POOL_HIST_PAD:POOL_HIST_PAD + tile, :] = u
    if has_cache:
        seen = None
    else:
        seen = (lax.broadcasted_iota(I32, (tile, 1), 0) + i * tile + 1).astype(F32)
    pool_blocks = []
    for g, w in enumerate(POOL_WINDOWS):
        cols = slice(POOL_GROUP_W * g, POOL_GROUP_W * (g + 1))
        acc = u[:, cols]
        for s in range(1, w):
            acc = acc + ubuf[POOL_HIST_PAD - s:POOL_HIST_PAD - s + tile, cols]
        cnt = float(w) if seen is None else jnp.minimum(seen, float(w))
        dlt = acc / cnt - u[:, cols]
        pool_blocks.append(jnp.dot(dlt.astype(BF16), wpool_ref[g], preferred_element_type=F32))
    pool = jnp.concatenate(pool_blocks, axis=-1) * pscale_ref[...]

    mixin = jnp.concatenate([pool, attn], axis=-1).astype(BF16)
    mix = jnp.dot(mixin, wout_ref[...], preferred_element_type=F32)
    x1 = xt + mod[2:3] * (_rms(mix) * gains[1:2])
    x1_ref[0] = x1

    new_hist = ubuf[tile:tile + POOL_HIST_PAD, :]
    ubuf[0:POOL_HIST_PAD, :] = new_hist
    npool_ref[0] = new_hist
    khist[...] = kw[tile:, :]
    vhist[...] = vw[tile:, :]
    nk_ref[0] = kw[n_keys - WINDOW:, :]
    nv_ref[0] = vw[n_keys - WINDOW:, :]

    h2f = _rms(x1) * gains[2:3] * (1.0 + mod[4:5]) + mod[3:4]
    h2hi = h2f.astype(BF16)
    h2_ref[0] = h2hi
    h2lo = (h2f - h2hi.astype(F32)).astype(BF16)
    wrt = wrt_ref[...]
    part = lax.dot_general(wrt, h2hi, _NT, preferred_element_type=F32)
    logits = (part[:N_EXPERTS] + part[N_EXPERTS:]
              + lax.dot_general(wrt[:N_EXPERTS], h2lo, _NT, preferred_element_type=F32))
    scores = _sigmoid(logits)
    sel = scores + rbias_ref[...]

    sub_g = lax.broadcasted_iota(I32, (GROUP_SIZE, tile), 0).astype(F32)
    gscore = jnp.zeros((N_EXPERT_GROUPS, tile), F32)
    for gi in range(N_EXPERT_GROUPS):
        blk = sel[GROUP_SIZE * gi:GROUP_SIZE * (gi + 1), :]
        m1 = jnp.max(blk, axis=0, keepdims=True)
        i1 = jnp.min(jnp.where(blk == m1, sub_g, float(GROUP_SIZE)), axis=0, keepdims=True)
        m2 = jnp.max(jnp.where(sub_g == i1, -jnp.inf, blk), axis=0, keepdims=True)
        gscore = jnp.where(sub_g == gi, m1 + m2, gscore)
    chosen = jnp.zeros((N_EXPERT_GROUPS, tile), F32)
    for _ in range(TOPK_GROUPS):
        m = jnp.max(gscore, axis=0, keepdims=True)
        idx = jnp.min(jnp.where(gscore == m, sub_g, float(N_EXPERT_GROUPS)), axis=0, keepdims=True)
        pick = sub_g == idx
        chosen = jnp.where(pick, 1.0, chosen)
        gscore = jnp.where(pick, -jnp.inf, gscore)
    emask = jnp.concatenate(
        [jnp.broadcast_to(chosen[gi:gi + 1, :], (GROUP_SIZE, tile)) for gi in range(N_EXPERT_GROUPS)], axis=0)
    selm = jnp.where(emask > 0.0, sel, -jnp.inf)

    sub_e = lax.broadcasted_iota(I32, (N_EXPERTS, tile), 0).astype(F32)
    picks, idxs, raw_w = [], [], []
    for _ in range(TOP_K):
        m = jnp.max(selm, axis=0, keepdims=True)
        idx = jnp.min(jnp.where(selm == m, sub_e, float(N_EXPERTS)), axis=0, keepdims=True)
        pick = sub_e == idx
        raw_w.append(jnp.sum(jnp.where(pick, scores, 0.0), axis=0, keepdims=True))
        selm = jnp.where(pick, -jnp.inf, selm)
        picks.append(pick)
        idxs.append(idx)
    wsum = raw_w[0]
    for kk in range(1, TOP_K):
        wsum = wsum + raw_w[kk]

    onehot = jnp.zeros((N_EXPERTS, tile), F32)
    for kk in range(TOP_K):
        onehot = jnp.where(picks[kk], 1.0, onehot)
    onehot_b = onehot.astype(BF16)
    tri = (lax.broadcasted_iota(I32, (tile, tile), 0) < lax.broadcasted_iota(I32, (tile, tile), 1)).astype(BF16)
    before = jnp.dot(onehot_b, tri, preferred_element_type=F32) + run[:, 0:1]
    sub_r = lax.broadcasted_iota(I32, (ROUTE_ROWS, tile), 0)
    eidx_o = jnp.zeros((ROUTE_ROWS, tile), I32)
    wts_o = jnp.zeros((ROUTE_ROWS, tile), F32)
    rank_o = jnp.zeros((ROUTE_ROWS, tile), I32)
    for kk in range(TOP_K):
        rk = jnp.sum(jnp.where(picks[kk], before, 0.0), axis=0, keepdims=True).astype(I32)
        eidx_o = jnp.where(sub_r == kk, idxs[kk].astype(I32), eidx_o)
        wts_o = jnp.where(sub_r == kk, raw_w[kk] / wsum * ROUTED_SCALE, wts_o)
        rank_o = jnp.where(sub_r == kk, rk, rank_o)
    eidx_ref[0] = eidx_o
    wts_ref[0] = wts_o
    rank_ref[0] = rank_o
    tcnt_ref[0] = lax.dot_general(jnp.ones((ROUTE_ROWS, tile), BF16), onehot_b, _NT,
                                  preferred_element_type=F32)
    run[...] = run[...] + jnp.sum(onehot, axis=1, keepdims=True)


def _front_call(x, mod, gains, hist_pool, hist_k, hist_v, cnt0, sinks,
                w_in, w_pool, pool_scale, w_out, wr_t, rbias, *, tile, has_cache):
    bsz, seq, _ = x.shape
    n_tiles = seq // tile
    hist_keys = hist_k.shape[1]
    assert hist_keys == ATTN_KEYS - min(tile, LANES)
    body = functools.partial(_front_body, tile, has_cache)
    whole = lambda shape: pl.BlockSpec(shape, lambda b, i: (0,) * len(shape))
    per_b = lambda shape: pl.BlockSpec((1,) + shape, lambda b, i: (b,) + (0,) * len(shape))
    route = pl.BlockSpec((1, ROUTE_ROWS, tile), lambda b, i: (b * n_tiles + i, 0, 0))
    out_shape = [
        jax.ShapeDtypeStruct((bsz, seq, D_MODEL), F32),
        jax.ShapeDtypeStruct((bsz, seq, D_MODEL), BF16),
        jax.ShapeDtypeStruct((bsz, POOL_HIST_PAD, POOL_WIDTH), F32),
        jax.ShapeDtypeStruct((bsz, WINDOW, KV_WIDTH), F32),
        jax.ShapeDtypeStruct((bsz, WINDOW, KV_WIDTH), F32),
        jax.ShapeDtypeStruct((bsz * n_tiles, ROUTE_ROWS, tile), I32),
        jax.ShapeDtypeStruct((bsz * n_tiles, ROUTE_ROWS, tile), F32),
        jax.ShapeDtypeStruct((bsz * n_tiles, ROUTE_ROWS, tile), I32),
        jax.ShapeDtypeStruct((bsz * n_tiles, ROUTE_ROWS, N_EXPERTS), F32),
    ]
    return pl.pallas_call(
        body,
        grid=(bsz, n_tiles),
        in_specs=[
            pl.BlockSpec(memory_space=pltpu.SMEM),
            pl.BlockSpec((1, tile, D_MODEL), lambda b, i: (b, i, 0)),
            per_b((6, D_MODEL)),
            whole((4, D_MODEL)),
            per_b((POOL_HIST_PAD, POOL_WIDTH)),
            per_b((hist_keys, KV_WIDTH)),
            per_b((hist_keys, KV_WIDTH)),
            whole((N_EXPERTS, LANES)),
            whole((D_MODEL, IN_WIDTH)),
            whole((len(POOL_WINDOWS), POOL_GROUP_W, POOL_GROUP_W)),
            whole((1, POOL_WIDTH)),
            whole((D_MODEL, D_MODEL)),
            whole((2 * N_EXPERTS, D_MODEL)),
            whole((N_EXPERTS, 1)),
        ],
        out_specs=[
            pl.BlockSpec((1, tile, D_MODEL), lambda b, i: (b, i, 0)),
            pl.BlockSpec((1, tile, D_MODEL), lambda b, i: (b, i, 0)),
            per_b((POOL_HIST_PAD, POOL_WIDTH)),
            per_b((WINDOW, KV_WIDTH)),
            per_b((WINDOW, KV_WIDTH)),
            route, route, route,
            pl.BlockSpec((1, ROUTE_ROWS, N_EXPERTS), lambda b, i: (b * n_tiles + i, 0, 0)),
        ],
        out_shape=out_shape,
        scratch_shapes=[
            pltpu.VMEM((tile + POOL_HIST_PAD, POOL_WIDTH), F32),
            pltpu.VMEM((hist_keys, KV_WIDTH), F32),
            pltpu.VMEM((hist_keys, KV_WIDTH), F32),
            pltpu.VMEM((N_EXPERTS, LANES), F32),
        ],
        compiler_params=pltpu.CompilerParams(
            dimension_semantics=("arbitrary", "arbitrary"), vmem_limit_bytes=VMEM_LIMIT),
        name="front_cached" if has_cache else "front_prompt",
    )(sinks, x, mod, gains, hist_pool, hist_k, hist_v, cnt0,
      w_in, w_pool, pool_scale, w_out, wr_t, rbias)


def _run_copy(src, dst, s_row, d_row, sem):
    return pltpu.make_async_copy(src.at[pl.ds(s_row, RUN_ROWS)], dst.at[pl.ds(d_row, RUN_ROWS)], sem)


def _for_each_run_chunk(step, lrow_ref, grow_ref, nct_ref, fn):
    def per_chunk(j, carry):
        idx = step * MAX_CHUNKS + j
        fn(pl.multiple_of(lrow_ref[idx], RUN_ROWS), pl.multiple_of(grow_ref[idx], RUN_ROWS))
        return carry

    lax.fori_loop(0, nct_ref[step], per_chunk, 0)


def _dispatch_body(n_prompt_steps, lrow_ref, grow_ref, nct_ref, zrow_ref, znch_ref, nused_ref,
                   h2p_ref, h2s_ref, eidx_ref, rank_ref, off_ref, wts_ref, xs_out, slot_out, loc, zrows, sem):
    i = pl.program_id(0)
    n_steps = pl.num_programs(0)
    par = lax.rem(i, 2)
    h2 = jnp.where(i < n_prompt_steps, h2p_ref[...], h2s_ref[...])
    eidx = eidx_ref[0]
    rank = rank_ref[0]
    wts = wts_ref[0]
    off = off_ref[0]
    expert_id = lax.broadcasted_iota(I32, (N_EXPERTS, MOE_TILE), 0)
    sub_r = lax.broadcasted_iota(I32, (ROUTE_ROWS, MOE_TILE), 0)
    slots = []
    slot_o = jnp.zeros((ROUTE_ROWS, MOE_TILE), I32)
    for kk in range(TOP_K):
        mine = jnp.sum(jnp.where(expert_id == eidx[kk:kk + 1, :], off, 0.0), axis=0, keepdims=True)
        slots.append(rank[kk:kk + 1, :] + mine.astype(I32))
        slot_o = jnp.where(sub_r == kk, slots[kk], slot_o)
    slot_out[0] = slot_o
    row_id = lax.broadcasted_iota(I32, (LOCAL_ROWS, MOE_TILE), 0)
    sel = jnp.zeros((LOCAL_ROWS, MOE_TILE), F32)
    selw = jnp.zeros((LOCAL_ROWS, MOE_TILE), F32)
    for kk in range(TOP_K):
        hit = row_id == slots[kk]
        sel = jnp.where(hit, 1.0, sel)
        selw = jnp.where(hit, wts[kk:kk + 1, :], selw)
    sorted_rows = jnp.dot(sel.astype(BF16), h2, preferred_element_type=F32)
    w_slot = jnp.sum(selw, axis=1, keepdims=True)
    buf = loc.at[par]
    buf[:, :D_MODEL] = sorted_rows.astype(BF16)
    w_hi, w_mid, w_lo = _split3(w_slot)
    lane = lax.broadcasted_iota(I32, (LOCAL_ROWS, LANES), 1)
    w_lanes = jnp.where(lane == 0, w_hi.astype(F32),
                        jnp.where(lane == 1, w_mid.astype(F32), jnp.where(lane == 2, w_lo.astype(F32), 0.0)))
    buf[:, D_MODEL:] = w_lanes.astype(BF16)

    def drain(n_chunks):
        def one(c, carry):
            _run_copy(buf, xs_out, 0, 0, sem).wait()
            return carry
        lax.fori_loop(0, n_chunks, one, 0)

    @pl.when(i > 0)
    def _():
        drain(nct_ref[i - 1])

    _for_each_run_chunk(i, lrow_ref, grow_ref, nct_ref,
                        lambda lrow, grow: _run_copy(buf, xs_out, lrow, grow, sem).start())

    @pl.when(i == n_steps - 1)
    def _():
        zrows[...] = jnp.zeros_like(zrows)

        def per_expert(e, total):
            def per_chunk(c, carry):
                _run_copy(zrows, xs_out, 0, pl.multiple_of(zrow_ref[e] + RUN_ROWS * c, RUN_ROWS), sem).start()
                return carry
            lax.fori_loop(0, znch_ref[e], per_chunk, 0)
            return total + znch_ref[e]

        n_zero = lax.fori_loop(0, N_EXPERTS, per_expert, 0)
        drain(nct_ref[i] + n_zero)

        def block_copy(blk):
            return pltpu.make_async_copy(
                zrows, xs_out.at[pl.ds(pl.multiple_of(blk * EXPERT_ROWS, EXPERT_ROWS), EXPERT_ROWS)], sem)

        n_blocks = xs_out.shape[0] // EXPERT_ROWS

        def start_block(blk, carry):
            block_copy(blk).start()
            return carry

        def wait_block(blk, carry):
            block_copy(blk).wait()
            return carry

        lax.fori_loop(nused_ref[0], n_blocks, start_block, 0)
        lax.fori_loop(nused_ref[0], n_blocks, wait_block, 0)


def _dispatch_call(tables, zero_tables, h2_p, h2_s, eidx_tiles, rank_tiles, off_tiles, wts_tiles, n_rows):
    n_prompt_steps = h2_p.shape[0] // MOE_TILE
    n_steps = n_prompt_steps + h2_s.shape[0] // MOE_TILE
    route = lambda: pl.BlockSpec((1, ROUTE_ROWS, MOE_TILE), lambda i, *_: (i, 0, 0))
    grid_spec = pltpu.PrefetchScalarGridSpec(
        num_scalar_prefetch=6,
        grid=(n_steps,),
        in_specs=[
            pl.BlockSpec((MOE_TILE, D_MODEL), lambda i, *_: (jnp.minimum(i, n_prompt_steps - 1), 0)),
            pl.BlockSpec((MOE_TILE, D_MODEL), lambda i, *_: (jnp.maximum(i - n_prompt_steps, 0), 0)),
            route(), route(),
            pl.BlockSpec((1, N_EXPERTS, 1), lambda i, *_: (i, 0, 0)),
            route(),
        ],
        out_specs=[pl.BlockSpec(memory_space=pl.ANY), route()],
        scratch_shapes=[
            pltpu.VMEM((2, LOCAL_ROWS, ROW_W), BF16),
            pltpu.VMEM((EXPERT_ROWS, ROW_W), BF16),
            pltpu.SemaphoreType.DMA,
        ],
    )
    return pl.pallas_call(
        functools.partial(_dispatch_body, n_prompt_steps),
        grid_spec=grid_spec,
        out_shape=[jax.ShapeDtypeStruct((n_rows, ROW_W), BF16),
                   jax.ShapeDtypeStruct((n_steps, ROUTE_ROWS, MOE_TILE), I32)],
        compiler_params=pltpu.CompilerParams(
            dimension_semantics=("arbitrary",), vmem_limit_bytes=VMEM_LIMIT),
        name="dispatch",
    )(*tables, *zero_tables, h2_p, h2_s, eidx_tiles, rank_tiles, off_tiles, wts_tiles)


def _block_rows(ref, blk):
    return ref.at[pl.ds(pl.multiple_of(blk * EXPERT_ROWS, EXPERT_ROWS), EXPERT_ROWS)]


def _experts_body(first_ref, nblk_ref, nused_ref, xs_hbm, wg_ref, wu_ref, wd_ref, ys_hbm,
                  xbuf, ybuf, wgb, wub, wdb, isem, osem):
    e = pl.program_id(0)
    n_exp = pl.num_programs(0)

    def items_of(expert):
        n_blk = nblk_ref[expert]
        n_big = n_blk // BIG_BLOCKS
        return first_ref[expert], n_big, n_big + n_blk - n_big * BIG_BLOCKS

    def item_copy(first, n_big, t, slot, rows_static, outward):
        if rows_static == BIG_ROWS:
            row0 = first * EXPERT_ROWS + t * BIG_ROWS
        else:
            row0 = first * EXPERT_ROWS + n_big * BIG_ROWS + (t - n_big) * EXPERT_ROWS
        row0 = pl.multiple_of(row0, EXPERT_ROWS)
        if outward:
            return pltpu.make_async_copy(ybuf.at[slot, pl.ds(0, rows_static)],
                                         ys_hbm.at[pl.ds(row0, rows_static)], osem.at[slot])
        return pltpu.make_async_copy(xs_hbm.at[pl.ds(row0, rows_static)],
                                     xbuf.at[slot, pl.ds(0, rows_static)], isem.at[slot])

    def for_item(first, n_big, t, slot, outward, action):
        @pl.when(t < n_big)
        def _():
            action(item_copy(first, n_big, t, slot, BIG_ROWS, outward))

        @pl.when(t >= n_big)
        def _():
            action(item_copy(first, n_big, t, slot, EXPERT_ROWS, outward))

    start = lambda c: c.start()
    wait = lambda c: c.wait()
    first, n_big, n_items = items_of(e)

    @pl.when((e == 0) & (n_items > 0))
    def _():
        for_item(first, n_big, 0, 0, False, start)

    def compute(slot, rows_static):
        xrow = xbuf[slot, 0:rows_static, :]
        x = xrow[:, :D_MODEL]
        w_parts = xrow[:, D_MODEL:].astype(F32)
        w_row = w_parts[:, 0:1] + w_parts[:, 1:2] + w_parts[:, 2:3]
        g = jnp.dot(x, wgb[...], preferred_element_type=F32)
        u = jnp.dot(x, wub[...], preferred_element_type=F32)
        a = (g * _sigmoid(g) * u).astype(BF16)
        y = jnp.dot(a, wdb[...], preferred_element_type=F32) * w_row
        ybuf[slot, 0:rows_static, :] = y.astype(BF16)

    @pl.when(n_items > 0)
    def _():
        wgb[...] = wg_ref[0].astype(BF16)
        wub[...] = wu_ref[0].astype(BF16)
        wdb[...] = wd_ref[0].astype(BF16)

        def one_item(t, carry):
            slot = lax.rem(t, 2)

            @pl.when(t + 1 < n_items)
            def _():
                for_item(first, n_big, t + 1, 1 - slot, False, start)

            for_item(first, n_big, t, slot, False, wait)

            @pl.when(t < n_big)
            def _():
                compute(slot, BIG_ROWS)

            @pl.when(t >= n_big)
            def _():
                compute(slot, EXPERT_ROWS)

            @pl.when(t >= 1)
            def _():
                for_item(first, n_big, t - 1, 1 - slot, True, wait)

            for_item(first, n_big, t, slot, True, start)
            return carry

        lax.fori_loop(0, n_items, one_item, 0)
        for_item(first, n_big, n_items - 1, lax.rem(n_items - 1, 2), True, wait)

    nxt = jnp.minimum(e + 1, n_exp - 1)
    nxt_first, nxt_big, nxt_items = items_of(nxt)

    @pl.when((e + 1 < n_exp) & (nxt_items > 0))
    def _():
        for_item(nxt_first, nxt_big, 0, 0, False, start)

    @pl.when(e == pl.num_programs(0) - 1)
    def _():
        ybuf[0, 0:EXPERT_ROWS, :] = jnp.zeros((EXPERT_ROWS, D_MODEL), BF16)
        n_blocks = ys_hbm.shape[0] // EXPERT_ROWS

        def tail_copy(blk):
            return pltpu.make_async_copy(ybuf.at[0, pl.ds(0, EXPERT_ROWS)], _block_rows(ys_hbm, blk), osem.at[0])

        def start_block(blk, carry):
            tail_copy(blk).start()
            return carry

        def wait_block(blk, carry):
            tail_copy(blk).wait()
            return carry

        lax.fori_loop(nused_ref[0], n_blocks, start_block, 0)
        lax.fori_loop(nused_ref[0], n_blocks, wait_block, 0)


def _experts_call(first_block, n_expert_blocks, n_used, xs, w_gate, w_up, w_down):
    grid_spec = pltpu.PrefetchScalarGridSpec(
        num_scalar_prefetch=3,
        grid=(N_EXPERTS,),
        in_specs=[
            pl.BlockSpec(memory_space=pl.ANY),
            pl.BlockSpec((1, D_MODEL, D_EXPERT), lambda e, *_: (e, 0, 0)),
            pl.BlockSpec((1, D_MODEL, D_EXPERT), lambda e, *_: (e, 0, 0)),
            pl.BlockSpec((1, D_EXPERT, D_MODEL), lambda e, *_: (e, 0, 0)),
        ],
        out_specs=pl.BlockSpec(memory_space=pl.ANY),
        scratch_shapes=[
            pltpu.VMEM((2, BIG_ROWS, ROW_W), BF16),
            pltpu.VMEM((2, BIG_ROWS, D_MODEL), BF16),
            pltpu.VMEM((D_MODEL, D_EXPERT), BF16),
            pltpu.VMEM((D_MODEL, D_EXPERT), BF16),
            pltpu.VMEM((D_EXPERT, D_MODEL), BF16),
            pltpu.SemaphoreType.DMA((2,)),
            pltpu.SemaphoreType.DMA((2,)),
        ],
    )
    return pl.pallas_call(
        _experts_body,
        grid_spec=grid_spec,
        out_shape=jax.ShapeDtypeStruct((xs.shape[0], D_MODEL), BF16),
        compiler_params=pltpu.CompilerParams(
            dimension_semantics=("arbitrary",), vmem_limit_bytes=VMEM_LIMIT),
        name="experts",
    )(first_block, n_expert_blocks, n_used, xs, w_gate, w_up, w_down)


def _combine_body(n_prompt_steps, lrow_ref, grow_ref, nct_ref,
                  x1p_ref, x1s_ref, h2p_ref, h2s_ref, gate_ref, gains_ref, slot_ref, ys_hbm,
                  wsg_ref, wsu_ref, wsd_ref, outp_ref, outs_ref, gath, sem):
    i = pl.program_id(0)
    n_steps = pl.num_programs(0)
    par = lax.rem(i, 2)

    def fetch(step, slot):
        buf = gath.at[slot]
        _for_each_run_chunk(step, lrow_ref, grow_ref, nct_ref,
                            lambda lrow, grow: _run_copy(ys_hbm, buf, grow, lrow, sem.at[slot]).start())

    @pl.when(i == 0)
    def _():
        gath[...] = jnp.zeros_like(gath)
        fetch(0, 0)

    @pl.when(i + 1 < n_steps)
    def _():
        fetch(i + 1, 1 - par)

    is_prompt = i < n_prompt_steps
    h2 = jnp.where(is_prompt, h2p_ref[...], h2s_ref[...])
    g = jnp.dot(h2, wsg_ref[...], preferred_element_type=F32)
    u = jnp.dot(h2, wsu_ref[...], preferred_element_type=F32)
    a = (g * _sigmoid(g) * u).astype(BF16)
    ffn = jnp.dot(a, wsd_ref[...], preferred_element_type=F32)

    slot_cols = _to_sublanes(slot_ref[0].astype(F32))
    col_id = lax.broadcasted_iota(I32, (MOE_TILE, LOCAL_ROWS), 1).astype(F32)
    take = jnp.zeros((MOE_TILE, LOCAL_ROWS), F32)
    for kk in range(TOP_K):
        take = jnp.where(col_id == slot_cols[:, kk:kk + 1], 1.0, take)

    buf = gath.at[par]

    def one(c, carry):
        _run_copy(ys_hbm, buf, 0, 0, sem.at[par]).wait()
        return carry

    lax.fori_loop(0, nct_ref[i], one, 0)

    ffn = ffn + jnp.dot(take.astype(BF16), buf[...], preferred_element_type=F32)

    quarter = MOE_TILE // gate_ref.shape[0]
    gate = jnp.concatenate(
        [jnp.broadcast_to(gate_ref[q:q + 1, :], (quarter, D_MODEL)) for q in range(gate_ref.shape[0])], axis=0)
    x1 = jnp.where(is_prompt, x1p_ref[...], x1s_ref[...])
    out = x1 + gate * (_rms(ffn) * gains_ref[3:4, :])

    @pl.when(is_prompt)
    def _():
        outp_ref[...] = out

    @pl.when(jnp.logical_not(is_prompt))
    def _():
        outs_ref[...] = out


def _combine_call(tables, x1_p, x1_s, h2_p, h2_s, gate_groups, gains, slot_tiles, ys,
                  ws_gate, ws_up, ws_down):
    n_prompt, n_sample = x1_p.shape[0], x1_s.shape[0]
    n_prompt_steps = n_prompt // MOE_TILE
    n_steps = n_prompt_steps + n_sample // MOE_TILE
    groups_per_tile = gate_groups.shape[0] // n_steps
    tok_p = lambda: pl.BlockSpec((MOE_TILE, D_MODEL), lambda i, *_: (jnp.minimum(i, n_prompt_steps - 1), 0))
    tok_s = lambda: pl.BlockSpec((MOE_TILE, D_MODEL), lambda i, *_: (jnp.maximum(i - n_prompt_steps, 0), 0))
    whole = lambda shape: pl.BlockSpec(shape, lambda i, *_: (0,) * len(shape))
    grid_spec = pltpu.PrefetchScalarGridSpec(
        num_scalar_prefetch=3,
        grid=(n_steps,),
        in_specs=[
            tok_p(), tok_s(), tok_p(), tok_s(),
            pl.BlockSpec((groups_per_tile, D_MODEL), lambda i, *_: (i, 0)),
            whole((4, D_MODEL)),
            pl.BlockSpec((1, ROUTE_ROWS, MOE_TILE), lambda i, *_: (i, 0, 0)),
            pl.BlockSpec(memory_space=pl.ANY),
            whole((D_MODEL, D_SHARED)), whole((D_MODEL, D_SHARED)), whole((D_SHARED, D_MODEL)),
        ],
        out_specs=[tok_p(), tok_s()],
        scratch_shapes=[
            pltpu.VMEM((2, LOCAL_ROWS, D_MODEL), BF16),
            pltpu.SemaphoreType.DMA((2,)),
        ],
    )
    return pl.pallas_call(
        functools.partial(_combine_body, n_prompt_steps),
        grid_spec=grid_spec,
        out_shape=[jax.ShapeDtypeStruct((n_prompt, D_MODEL), F32),
                   jax.ShapeDtypeStruct((n_sample, D_MODEL), F32)],
        compiler_params=pltpu.CompilerParams(
            dimension_semantics=("arbitrary",), vmem_limit_bytes=VMEM_LIMIT),
        name="combine",
    )(*tables, x1_p, x1_s, h2_p, h2_s, gate_groups, gains, slot_tiles, ys, ws_gate, ws_up, ws_down)


def kernel(x_prompt, x_sample, c_prompt, c_sample, state_pool, cache_k, cache_v, w_ada, b_ada, norm_gains,
           w_in, w_pool, pool_scale, attn_sinks, w_out, w_router, router_bias, w_gate, w_up, w_down,
           ws_gate, ws_up, ws_down):
    assert w_ada.shape[0] == 1, "single-layer kernel"
    bsz, seq, _ = x_prompt.shape
    dbsz, dseq, _ = x_sample.shape
    n_prompt, n_sample = bsz * seq, dbsz * dseq
    n_tok = n_prompt + n_sample
    assert dseq == CHUNK and seq % MOE_TILE == 0 and n_sample % MOE_TILE == 0

    w_in0, w_out0 = w_in[0], w_out[0]
    wq = w_in0[:, POOL_WIDTH:POOL_WIDTH + ATTN_WIDTH].reshape(D_MODEL, N_KV_HEADS, GQ, HEAD_DIM)
    wq = jnp.transpose(wq, (0, 2, 1, 3)).reshape(D_MODEL, ATTN_WIDTH)
    w_in_p = jnp.concatenate(
        [w_in0[:, :POOL_WIDTH], wq, w_in0[:, POOL_WIDTH + ATTN_WIDTH:]], axis=1).astype(BF16)
    wo = w_out0[POOL_WIDTH:].reshape(N_KV_HEADS, GQ, HEAD_DIM, D_MODEL)
    wo = jnp.transpose(wo, (1, 0, 2, 3)).reshape(ATTN_WIDTH, D_MODEL)
    w_out_p = jnp.concatenate([w_out0[:POOL_WIDTH], wo], axis=0).astype(BF16)
    w_pool_b = w_pool[0].astype(BF16)
    pscale = pool_scale[0].reshape(1, POOL_WIDTH)
    wr_t = w_router[0].T
    wr_hi = wr_t.astype(BF16)
    wr_lo = (wr_t - wr_hi.astype(F32)).astype(BF16)
    wr_split = jnp.concatenate([wr_hi, wr_lo], axis=0)
    rbias = router_bias[0].reshape(N_EXPERTS, 1)
    gains = norm_gains[0]
    sinks = attn_sinks[0]

    mod = _ada_call(jnp.concatenate([c_prompt, c_sample], axis=0), w_ada[0], b_ada[0])
    mod = mod.reshape(bsz + dbsz, 6, D_MODEL)

    zeros_pool = jnp.zeros((bsz, POOL_HIST_PAD, POOL_WIDTH), F32)
    zeros_kv = jnp.zeros((bsz, WINDOW, KV_WIDTH), F32)
    cnt0 = jnp.zeros((N_EXPERTS, LANES), F32)
    (x1_p, h2_p, pool_p, nk_p, nv_p, eidx_p, wts_p, rank_p, tcnt_p) = _front_call(
        x_prompt, mod[:bsz], gains, zeros_pool, zeros_kv, zeros_kv, cnt0, sinks,
        w_in_p, w_pool_b, pscale, w_out_p, wr_split, rbias, tile=MOE_TILE, has_cache=False)
    cnt_prompt = jnp.broadcast_to(jnp.sum(tcnt_p[:, 0, :], axis=0)[:, None], (N_EXPERTS, LANES))

    hist_pad = ATTN_KEYS - CHUNK - WINDOW
    pool_s0 = jnp.pad(state_pool[0], ((0, 0), (POOL_HIST_PAD - state_pool.shape[2], 0), (0, 0)))
    ck = jnp.pad(cache_k[0].reshape(dbsz, WINDOW, KV_WIDTH), ((0, 0), (hist_pad, 0), (0, 0)))
    cv = jnp.pad(cache_v[0].reshape(dbsz, WINDOW, KV_WIDTH), ((0, 0), (hist_pad, 0), (0, 0)))
    (x1_s, h2_s, pool_s, nk_s, nv_s, eidx_s, wts_s, rank_s, tcnt_s) = _front_call(
        x_sample, mod[bsz:], gains, pool_s0, ck, cv, cnt_prompt, sinks,
        w_in_p, w_pool_b, pscale, w_out_p, wr_split, rbias, tile=CHUNK, has_cache=True)

    def moe_tiles(route_p, route_s):
        per = MOE_TILE // CHUNK
        regrouped = jnp.transpose(route_s.reshape(-1, per, ROUTE_ROWS, CHUNK), (0, 2, 1, 3))
        return jnp.concatenate([route_p, regrouped.reshape(-1, ROUTE_ROWS, MOE_TILE)], axis=0)

    n_steps = n_tok // MOE_TILE
    cnt_td = jnp.concatenate(
        [tcnt_p[:, 0, :], tcnt_s[:, 0, :].reshape(-1, MOE_TILE // CHUNK, N_EXPERTS).sum(axis=1)],
        axis=0).astype(I32)
    c8 = (cnt_td + RUN_ROWS - 1) // RUN_ROWS * RUN_ROWS
    rows8 = jnp.sum(c8, axis=0)
    padded = (rows8 + EXPERT_ROWS - 1) // EXPERT_ROWS * EXPERT_ROWS
    pends = jnp.cumsum(padded)
    pstarts = pends - padded
    base = jnp.cumsum(cnt_td, axis=0) - cnt_td
    grow = pstarts[None, :] + jnp.cumsum(c8, axis=0) - c8
    lo8 = jnp.cumsum(c8, axis=1) - c8
    nch = c8 // RUN_ROWS
    nct = jnp.sum(nch, axis=1)
    off_tiles = (lo8 - base).astype(F32).reshape(n_steps, N_EXPERTS, 1)
    cum = jnp.cumsum(nch, axis=1)
    j = jnp.arange(MAX_CHUNKS, dtype=I32)
    owner = jnp.sum(j[None, :, None] >= cum[:, None, :], axis=-1)
    owns = owner[:, :, None] == jnp.arange(N_EXPERTS, dtype=I32)[None, None, :]
    of_owner = lambda tab: jnp.sum(jnp.where(owns, tab[:, None, :], 0), axis=-1)
    within = RUN_ROWS * (j[None, :] - of_owner(cum - nch))
    tables = ((of_owner(lo8) + within).reshape(-1).astype(I32),
              (of_owner(grow) + within).reshape(-1).astype(I32), nct.astype(I32))
    n_blocks = -(-(n_tok * TOP_K + n_steps * N_EXPERTS * (RUN_ROWS - 1) + N_EXPERTS * (EXPERT_ROWS - 1))
                 // EXPERT_ROWS)
    n_used = (pends[-1] // EXPERT_ROWS).astype(I32).reshape(1)
    zero_tables = ((pstarts + rows8).astype(I32), ((padded - rows8) // RUN_ROWS).astype(I32), n_used)

    h2_pf, h2_sf = h2_p.reshape(n_prompt, D_MODEL), h2_s.reshape(n_sample, D_MODEL)
    xs, slot_tiles = _dispatch_call(
        tables, zero_tables, h2_pf, h2_sf, moe_tiles(eidx_p, eidx_s), moe_tiles(rank_p, rank_s), off_tiles,
        moe_tiles(wts_p, wts_s), n_blocks * EXPERT_ROWS)
    ys = _experts_call((pstarts // EXPERT_ROWS).astype(I32), (padded // EXPERT_ROWS).astype(I32), n_used,
                       xs, w_gate[0], w_up[0], w_down[0])
    gate_groups = jnp.concatenate(
        [jnp.repeat(mod[:bsz, 5], seq // GATE_GROUP, axis=0),
         jnp.repeat(mod[bsz:, 5], dseq // GATE_GROUP, axis=0)], axis=0)
    y_p, y_s = _combine_call(
        tables, x1_p.reshape(n_prompt, D_MODEL), x1_s.reshape(n_sample, D_MODEL), h2_pf, h2_sf,
        gate_groups, gains, slot_tiles, ys,
        ws_gate[0].astype(BF16), ws_up[0].astype(BF16), ws_down[0].astype(BF16))

    n_hist = state_pool.shape[2]
    kv_shape = (1, -1, WINDOW, N_KV_HEADS, HEAD_DIM)
    return (y_p.reshape(bsz, seq, D_MODEL), y_s.reshape(dbsz, dseq, D_MODEL),
            pool_p[None, :, POOL_HIST_PAD - n_hist:], nk_p.reshape(kv_shape), nv_p.reshape(kv_shape),
            pool_s[None, :, POOL_HIST_PAD - n_hist:], nk_s.reshape(kv_shape), nv_s.reshape(kv_shape))
```

```python
import functools

import jax
import jax.numpy as jnp
import numpy as np
from jax import lax
from jax.experimental import pallas as pl
from jax.experimental.pallas import tpu as pltpu

F32 = jnp.float32
BF16 = jnp.bfloat16
I32 = jnp.int32

D_MODEL = 1024
CHUNK = 64
POOL_WIDTH = 512
POOL_WINDOWS = (2, 4, 8, 16)
POOL_GROUP_W = 128
POOL_HIST_PAD = 16
HEAD_DIM = 64
N_HEADS = 8
N_KV_HEADS = 2
GQ = N_HEADS // N_KV_HEADS
ATTN_WIDTH = N_HEADS * HEAD_DIM
KV_WIDTH = N_KV_HEADS * HEAD_DIM
IN_WIDTH = POOL_WIDTH + ATTN_WIDTH + 2 * KV_WIDTH
WINDOW = 128
N_EXPERTS = 64
TOP_K = 6
N_EXPERT_GROUPS = 8
GROUP_SIZE = N_EXPERTS // N_EXPERT_GROUPS
TOPK_GROUPS = 4
D_EXPERT = 256
D_SHARED = 256
ROUTED_SCALE = 2.5
NORM_EPS = 1e-6
NEG_BIG = -1e30

LANES = 128
SUBLANES = 8
ATTN_KEYS = 2 * LANES
ROUTE_ROWS = 8
EXPERT_ROWS = 256
BIG_BLOCKS = 4
BIG_ROWS = BIG_BLOCKS * EXPERT_ROWS
MOE_TILE = 256
GATE_GROUP = MOE_TILE // SUBLANES
RUN_ROWS = 2 * SUBLANES
LOCAL_ROWS = 2560
LOCAL_ROW_OPTIONS = (2048, 2176, LOCAL_ROWS)
MAX_CHUNKS = LOCAL_ROWS // RUN_ROWS
ROW_W = D_MODEL + LANES
VMEM_LIMIT = 56 * 1024 * 1024

assert TOP_K * MOE_TILE + N_EXPERTS * (RUN_ROWS - 1) <= LOCAL_ROWS

_NT = (((1,), (1,)), ((), ()))


def _rms(v):
    return v * lax.rsqrt(jnp.mean(v * v, axis=-1, keepdims=True) + NORM_EPS)


def _sigmoid(v):
    return 1.0 / (1.0 + jnp.exp(-v))


def _split3(v):
    hi = v.astype(BF16)
    r1 = v - hi.astype(F32)
    mid = r1.astype(BF16)
    lo = (r1 - mid.astype(F32)).astype(BF16)
    return hi, mid, lo


def _to_sublanes(rows):
    n = rows.shape[1]
    hi, mid, lo = _split3(rows)
    eye = (lax.broadcasted_iota(I32, (n, n), 0) == lax.broadcasted_iota(I32, (n, n), 1)).astype(BF16)
    return (lax.dot_general(eye, hi, _NT, preferred_element_type=F32)
            + lax.dot_general(eye, mid, _NT, preferred_element_type=F32)
            + lax.dot_general(eye, lo, _NT, preferred_element_type=F32))


def _ada_body(c_ref, w_ref, b_ref, o_ref):
    c = c_ref[...]
    s = c * _sigmoid(c)
    o_ref[...] = jnp.dot(s, w_ref[...], preferred_element_type=F32) + b_ref[...]


def _ada_call(c_all, w_ada, b_ada):
    nb = c_all.shape[0]
    n_out = w_ada.shape[1]
    tile = D_MODEL
    return pl.pallas_call(
        _ada_body,
        grid=(n_out // tile,),
        in_specs=[
            pl.BlockSpec((nb, D_MODEL), lambda j: (0, 0)),
            pl.BlockSpec((D_MODEL, tile), lambda j: (0, j)),
            pl.BlockSpec((1, tile), lambda j: (0, j)),
        ],
        out_specs=pl.BlockSpec((nb, tile), lambda j: (0, j)),
        out_shape=jax.ShapeDtypeStruct((nb, n_out), F32),
        compiler_params=pltpu.CompilerParams(dimension_semantics=("arbitrary",)),
        name="ada",
    )(c_all, w_ada, b_ada.reshape(1, n_out))


def _front_body(tile, has_cache,
                sinks_ref, x_ref, mod_ref, gains_ref, hp_ref, hk_ref, hv_ref, cnt0_ref,
                win_ref, wpool_ref, pscale_ref, wout_ref, wrt_ref, rbias_ref,
                x1_ref, h2_ref, npool_ref, nk_ref, nv_ref, eidx_ref, wts_ref, rank_ref, tcnt_ref,
                ubuf, khist, vhist, run):
    b = pl.program_id(0)
    i = pl.program_id(1)
    sub_q = min(tile, LANES)
    n_sub = tile // sub_q
    hist_keys = ATTN_KEYS - sub_q
    n_keys = hist_keys + tile

    @pl.when(i == 0)
    def _():
        ubuf[0:POOL_HIST_PAD, :] = hp_ref[0]
        khist[...] = hk_ref[0]
        vhist[...] = hv_ref[0]

    @pl.when((b == 0) & (i == 0))
    def _():
        run[...] = cnt0_ref[...]

    mod = mod_ref[0]
    gains = gains_ref[...]
    xt = x_ref[0]

    h = _rms(xt) * gains[0:1] * (1.0 + mod[1:2]) + mod[0:1]
    proj = jnp.dot(h.astype(BF16), win_ref[...], preferred_element_type=F32)
    u = proj[:, :POOL_WIDTH]
    o_k = POOL_WIDTH + ATTN_WIDTH
    k_new = proj[:, o_k:o_k + KV_WIDTH]
    v_new = proj[:, o_k + KV_WIDTH:]

    kw = jnp.concatenate([khist[...], k_new], axis=0)
    vw = jnp.concatenate([vhist[...], v_new], axis=0)
    kwb = kw.astype(BF16)
    vwb = vw.astype(BF16)
    qrow = lax.broadcasted_iota(I32, (sub_q, ATTN_KEYS), 0)
    kpos = lax.broadcasted_iota(I32, (sub_q, ATTN_KEYS), 1) - hist_keys
    qchunk = jnp.right_shift(qrow, 6)
    vis_band = (kpos >= CHUNK * (qchunk - 2)) & (kpos < CHUNK * (qchunk + 1))
    side0 = lax.broadcasted_iota(I32, (sub_q, LANES), 1) < HEAD_DIM
    attn_rows = []
    for r in range(n_sub):
        vis = vis_band
        if not has_cache:
            vis = vis & (kpos + (i * tile + r * sub_q) >= 0)
        kb = kwb[r * sub_q:r * sub_q + ATTN_KEYS]
        vb = vwb[r * sub_q:r * sub_q + ATTN_KEYS]
        blocks = []
        for j in range(GQ):
            qp = proj[r * sub_q:(r + 1) * sub_q, POOL_WIDTH + LANES * j:POOL_WIDTH + LANES * (j + 1)]
            qp = qp * (HEAD_DIM ** -0.5)
            outs = []
            for s in range(N_KV_HEADS):
                keep = side0 if s == 0 else jnp.logical_not(side0)
                qm = jnp.where(keep, qp, 0.0).astype(BF16)
                sc = lax.dot_general(qm, kb, _NT, preferred_element_type=F32)
                sc = jnp.where(vis, sc, NEG_BIG)
                sink = sinks_ref[j + GQ * s]
                m = jnp.maximum(jnp.max(sc, axis=-1, keepdims=True), sink)
                p = jnp.exp(sc - m)
                den = jnp.sum(p, axis=-1, keepdims=True) + jnp.exp(sink - m)
                p = (p / den).astype(BF16)
                outs.append(jnp.dot(p, vb, preferred_element_type=F32))
            blocks.append(jnp.where(side0, outs[0], outs[1]))
        attn_rows.append(jnp.concatenate(blocks, axis=-1))
    attn = attn_rows[0] if n_sub == 1 else jnp.concatenate(attn_rows, axis=0)

    ubuf[POOL_HIST_PAD:POOL_HIST_PAD + tile, :] = u
    if has_cache:
        seen = None
    else:
        seen = (lax.broadcasted_iota(I32, (tile, 1), 0) + i * tile + 1).astype(F32)
    pool_blocks = []
    for g, w in enumerate(POOL_WINDOWS):
        cols = slice(POOL_GROUP_W * g, POOL_GROUP_W * (g + 1))
        acc = u[:, cols]
        for s in range(1, w):
            acc = acc + ubuf[POOL_HIST_PAD - s:POOL_HIST_PAD - s + tile, cols]
        cnt = float(w) if seen is None else jnp.minimum(seen, float(w))
        dlt = acc / cnt - u[:, cols]
        pool_blocks.append(jnp.dot(dlt.astype(BF16), wpool_ref[g], preferred_element_type=F32))
    pool = jnp.concatenate(pool_blocks, axis=-1) * pscale_ref[...]

    mixin = jnp.concatenate([pool, attn], axis=-1).astype(BF16)
    mix = jnp.dot(mixin, wout_ref[...], preferred_element_type=F32)
    x1 = xt + mod[2:3] * (_rms(mix) * gains[1:2])
    x1_ref[0] = x1

    new_hist = ubuf[tile:tile + POOL_HIST_PAD, :]
    ubuf[0:POOL_HIST_PAD, :] = new_hist
    npool_ref[0] = new_hist
    khist[...] = kw[tile:, :]
    vhist[...] = vw[tile:, :]
    nk_ref[0] = kw[n_keys - WINDOW:, :]
    nv_ref[0] = vw[n_keys - WINDOW:, :]

    h2f = _rms(x1) * gains[2:3] * (1.0 + mod[4:5]) + mod[3:4]
    h2hi = h2f.astype(BF16)
    h2_ref[0] = h2hi
    h2lo = (h2f - h2hi.astype(F32)).astype(BF16)
    wrt = wrt_ref[...]
    part = lax.dot_general(wrt, h2hi, _NT, preferred_element_type=F32)
    logits = (part[:N_EXPERTS] + part[N_EXPERTS:]
              + lax.dot_general(wrt[:N_EXPERTS], h2lo, _NT, preferred_element_type=F32))
    scores = _sigmoid(logits)
    sel = scores + rbias_ref[...]

    sub_g = lax.broadcasted_iota(I32, (GROUP_SIZE, tile), 0).astype(F32)
    gscore = jnp.zeros((N_EXPERT_GROUPS, tile), F32)
    for gi in range(N_EXPERT_GROUPS):
        blk = sel[GROUP_SIZE * gi:GROUP_SIZE * (gi + 1), :]
        m1 = jnp.max(blk, axis=0, keepdims=True)
        i1 = jnp.min(jnp.where(blk == m1, sub_g, float(GROUP_SIZE)), axis=0, keepdims=True)
        m2 = jnp.max(jnp.where(sub_g == i1, -jnp.inf, blk), axis=0, keepdims=True)
        gscore = jnp.where(sub_g == gi, m1 + m2, gscore)
    chosen = jnp.zeros((N_EXPERT_GROUPS, tile), F32)
    for _ in range(TOPK_GROUPS):
        m = jnp.max(gscore, axis=0, keepdims=True)
        idx = jnp.min(jnp.where(gscore == m, sub_g, float(N_EXPERT_GROUPS)), axis=0, keepdims=True)
        pick = sub_g == idx
        chosen = jnp.where(pick, 1.0, chosen)
        gscore = jnp.where(pick, -jnp.inf, gscore)
    emask = jnp.concatenate(
        [jnp.broadcast_to(chosen[gi:gi + 1, :], (GROUP_SIZE, tile)) for gi in range(N_EXPERT_GROUPS)], axis=0)
    selm = jnp.where(emask > 0.0, sel, -jnp.inf)

    sub_e = lax.broadcasted_iota(I32, (N_EXPERTS, tile), 0).astype(F32)
    picks, idxs, raw_w = [], [], []
    for _ in range(TOP_K):
        m = jnp.max(selm, axis=0, keepdims=True)
        idx = jnp.min(jnp.where(selm == m, sub_e, float(N_EXPERTS)), axis=0, keepdims=True)
        pick = sub_e == idx
        raw_w.append(jnp.sum(jnp.where(pick, scores, 0.0), axis=0, keepdims=True))
        selm = jnp.where(pick, -jnp.inf, selm)
        picks.append(pick)
        idxs.append(idx)
    wsum = raw_w[0]
    for kk in range(1, TOP_K):
        wsum = wsum + raw_w[kk]

    onehot = jnp.zeros((N_EXPERTS, tile), F32)
    for kk in range(TOP_K):
        onehot = jnp.where(picks[kk], 1.0, onehot)
    onehot_b = onehot.astype(BF16)
    tri = (lax.broadcasted_iota(I32, (tile, tile), 0) < lax.broadcasted_iota(I32, (tile, tile), 1)).astype(BF16)
    before = jnp.dot(onehot_b, tri, preferred_element_type=F32) + run[:, 0:1]
    sub_r = lax.broadcasted_iota(I32, (ROUTE_ROWS, tile), 0)
    eidx_o = jnp.zeros((ROUTE_ROWS, tile), I32)
    wts_o = jnp.zeros((ROUTE_ROWS, tile), F32)
    rank_o = jnp.zeros((ROUTE_ROWS, tile), I32)
    for kk in range(TOP_K):
        rk = jnp.sum(jnp.where(picks[kk], before, 0.0), axis=0, keepdims=True).astype(I32)
        eidx_o = jnp.where(sub_r == kk, idxs[kk].astype(I32), eidx_o)
        wts_o = jnp.where(sub_r == kk, raw_w[kk] / wsum * ROUTED_SCALE, wts_o)
        rank_o = jnp.where(sub_r == kk, rk, rank_o)
    eidx_ref[0] = eidx_o
    wts_ref[0] = wts_o
    rank_ref[0] = rank_o
    tcnt_ref[0] = lax.dot_general(jnp.ones((ROUTE_ROWS, tile), BF16), onehot_b, _NT,
                                  preferred_element_type=F32)
    run[...] = run[...] + jnp.sum(onehot, axis=1, keepdims=True)


def _front_call(x, mod, gains, hist_pool, hist_k, hist_v, cnt0, sinks,
                w_in, w_pool, pool_scale, w_out, wr_t, rbias, *, tile, has_cache):
    bsz, seq, _ = x.shape
    n_tiles = seq // tile
    hist_keys = hist_k.shape[1]
    assert hist_keys == ATTN_KEYS - min(tile, LANES)
    body = functools.partial(_front_body, tile, has_cache)
    whole = lambda shape: pl.BlockSpec(shape, lambda b, i: (0,) * len(shape))
    per_b = lambda shape: pl.BlockSpec((1,) + shape, lambda b, i: (b,) + (0,) * len(shape))
    route = pl.BlockSpec((1, ROUTE_ROWS, tile), lambda b, i: (b * n_tiles + i, 0, 0))
    out_shape = [
        jax.ShapeDtypeStruct((bsz, seq, D_MODEL), F32),
        jax.ShapeDtypeStruct((bsz, seq, D_MODEL), BF16),
        jax.ShapeDtypeStruct((bsz, POOL_HIST_PAD, POOL_WIDTH), F32),
        jax.ShapeDtypeStruct((bsz, WINDOW, KV_WIDTH), F32),
        jax.ShapeDtypeStruct((bsz, WINDOW, KV_WIDTH), F32),
        jax.ShapeDtypeStruct((bsz * n_tiles, ROUTE_ROWS, tile), I32),
        jax.ShapeDtypeStruct((bsz * n_tiles, ROUTE_ROWS, tile), F32),
        jax.ShapeDtypeStruct((bsz * n_tiles, ROUTE_ROWS, tile), I32),
        jax.ShapeDtypeStruct((bsz * n_tiles, ROUTE_ROWS, N_EXPERTS), F32),
    ]
    return pl.pallas_call(
        body,
        grid=(bsz, n_tiles),
        in_specs=[
            pl.BlockSpec(memory_space=pltpu.SMEM),
            pl.BlockSpec((1, tile, D_MODEL), lambda b, i: (b, i, 0)),
            per_b((6, D_MODEL)),
            whole((4, D_MODEL)),
            per_b((POOL_HIST_PAD, POOL_WIDTH)),
            per_b((hist_keys, KV_WIDTH)),
            per_b((hist_keys, KV_WIDTH)),
            whole((N_EXPERTS, LANES)),
            whole((D_MODEL, IN_WIDTH)),
            whole((len(POOL_WINDOWS), POOL_GROUP_W, POOL_GROUP_W)),
            whole((1, POOL_WIDTH)),
            whole((D_MODEL, D_MODEL)),
            whole((2 * N_EXPERTS, D_MODEL)),
            whole((N_EXPERTS, 1)),
        ],
        out_specs=[
            pl.BlockSpec((1, tile, D_MODEL), lambda b, i: (b, i, 0)),
            pl.BlockSpec((1, tile, D_MODEL), lambda b, i: (b, i, 0)),
            per_b((POOL_HIST_PAD, POOL_WIDTH)),
            per_b((WINDOW, KV_WIDTH)),
            per_b((WINDOW, KV_WIDTH)),
            route, route, route,
            pl.BlockSpec((1, ROUTE_ROWS, N_EXPERTS), lambda b, i: (b * n_tiles + i, 0, 0)),
        ],
        out_shape=out_shape,
        scratch_shapes=[
            pltpu.VMEM((tile + POOL_HIST_PAD, POOL_WIDTH), F32),
            pltpu.VMEM((hist_keys, KV_WIDTH), F32),
            pltpu.VMEM((hist_keys, KV_WIDTH), F32),
            pltpu.VMEM((N_EXPERTS, LANES), F32),
        ],
        compiler_params=pltpu.CompilerParams(
            dimension_semantics=("arbitrary", "arbitrary"), vmem_limit_bytes=VMEM_LIMIT),
        name="front_cached" if has_cache else "front_prompt",
    )(sinks, x, mod, gains, hist_pool, hist_k, hist_v, cnt0,
      w_in, w_pool, pool_scale, w_out, wr_t, rbias)


def _run_copy(src, dst, s_row, d_row, sem):
    return pltpu.make_async_copy(src.at[pl.ds(s_row, RUN_ROWS)], dst.at[pl.ds(d_row, RUN_ROWS)], sem)


def _for_each_run_chunk(step, lrow_ref, grow_ref, nct_ref, fn):
    def per_chunk(j, carry):
        idx = step * MAX_CHUNKS + j
        fn(pl.multiple_of(lrow_ref[idx], RUN_ROWS), pl.multiple_of(grow_ref[idx], RUN_ROWS))
        return carry

    lax.fori_loop(0, nct_ref[step], per_chunk, 0)


def _for_row_option(used_rows, fn):
    lower = 0
    for n_rows in LOCAL_ROW_OPTIONS:
        @pl.when((used_rows > lower) & (used_rows <= n_rows))
        def _(n_rows=n_rows):
            fn(n_rows)
        lower = n_rows


def _dispatch_body(n_prompt_steps, lrow_ref, grow_ref, nct_ref, zrow_ref, znch_ref, nused_ref,
                   h2p_ref, h2s_ref, eidx_ref, rank_ref, off_ref, wts_ref, xs_out, slot_out, loc, zrows, sem):
    i = pl.program_id(0)
    n_steps = pl.num_programs(0)
    par = lax.rem(i, 2)
    h2 = jnp.where(i < n_prompt_steps, h2p_ref[...], h2s_ref[...])
    eidx = eidx_ref[0]
    rank = rank_ref[0]
    wts = wts_ref[0]
    off = off_ref[0]
    expert_id = lax.broadcasted_iota(I32, (N_EXPERTS, MOE_TILE), 0)
    sub_r = lax.broadcasted_iota(I32, (ROUTE_ROWS, MOE_TILE), 0)
    slots = []
    slot_o = jnp.zeros((ROUTE_ROWS, MOE_TILE), I32)
    for kk in range(TOP_K):
        mine = jnp.sum(jnp.where(expert_id == eidx[kk:kk + 1, :], off, 0.0), axis=0, keepdims=True)
        slots.append(rank[kk:kk + 1, :] + mine.astype(I32))
        slot_o = jnp.where(sub_r == kk, slots[kk], slot_o)
    slot_out[0] = slot_o
    buf = loc.at[par]

    def sort_tile(n_rows):
        row_id = lax.broadcasted_iota(I32, (n_rows, MOE_TILE), 0)
        sel = jnp.zeros((n_rows, MOE_TILE), F32)
        selw = jnp.zeros((n_rows, MOE_TILE), F32)
        for kk in range(TOP_K):
            hit = row_id == slots[kk]
            sel = jnp.where(hit, 1.0, sel)
            selw = jnp.where(hit, wts[kk:kk + 1, :], selw)
        sorted_rows = jnp.dot(sel.astype(BF16), h2, preferred_element_type=F32)
        w_slot = jnp.sum(selw, axis=1, keepdims=True)
        buf[0:n_rows, :D_MODEL] = sorted_rows.astype(BF16)
        w_hi, w_mid, w_lo = _split3(w_slot)
        lane = lax.broadcasted_iota(I32, (n_rows, LANES), 1)
        w_lanes = jnp.where(lane == 0, w_hi.astype(F32),
                            jnp.where(lane == 1, w_mid.astype(F32), jnp.where(lane == 2, w_lo.astype(F32), 0.0)))
        buf[0:n_rows, D_MODEL:] = w_lanes.astype(BF16)

    _for_row_option(nct_ref[i] * RUN_ROWS, sort_tile)

    def drain(n_chunks):
        def one(c, carry):
            _run_copy(buf, xs_out, 0, 0, sem).wait()
            return carry
        lax.fori_loop(0, n_chunks, one, 0)

    @pl.when(i > 0)
    def _():
        drain(nct_ref[i - 1])

    _for_each_run_chunk(i, lrow_ref, grow_ref, nct_ref,
                        lambda lrow, grow: _run_copy(buf, xs_out, lrow, grow, sem).start())

    @pl.when(i == n_steps - 1)
    def _():
        zrows[...] = jnp.zeros_like(zrows)

        def per_expert(e, total):
            def per_chunk(c, carry):
                _run_copy(zrows, xs_out, 0, pl.multiple_of(zrow_ref[e] + RUN_ROWS * c, RUN_ROWS), sem).start()
                return carry
            lax.fori_loop(0, znch_ref[e], per_chunk, 0)
            return total + znch_ref[e]

        n_zero = lax.fori_loop(0, N_EXPERTS, per_expert, 0)
        drain(nct_ref[i] + n_zero)

        def block_copy(blk):
            return pltpu.make_async_copy(
                zrows, xs_out.at[pl.ds(pl.multiple_of(blk * EXPERT_ROWS, EXPERT_ROWS), EXPERT_ROWS)], sem)

        n_blocks = xs_out.shape[0] // EXPERT_ROWS

        def start_block(blk, carry):
            block_copy(blk).start()
            return carry

        def wait_block(blk, carry):
            block_copy(blk).wait()
            return carry

        lax.fori_loop(nused_ref[0], n_blocks, start_block, 0)
        lax.fori_loop(nused_ref[0], n_blocks, wait_block, 0)


def _dispatch_call(tables, zero_tables, h2_p, h2_s, eidx_tiles, rank_tiles, off_tiles, wts_tiles, n_rows):
    n_prompt_steps = h2_p.shape[0] // MOE_TILE
    n_steps = n_prompt_steps + h2_s.shape[0] // MOE_TILE
    route = lambda: pl.BlockSpec((1, ROUTE_ROWS, MOE_TILE), lambda i, *_: (i, 0, 0))
    grid_spec = pltpu.PrefetchScalarGridSpec(
        num_scalar_prefetch=6,
        grid=(n_steps,),
        in_specs=[
            pl.BlockSpec((MOE_TILE, D_MODEL), lambda i, *_: (jnp.minimum(i, n_prompt_steps - 1), 0)),
            pl.BlockSpec((MOE_TILE, D_MODEL), lambda i, *_: (jnp.maximum(i - n_prompt_steps, 0), 0)),
            route(), route(),
            pl.BlockSpec((1, N_EXPERTS, 1), lambda i, *_: (i, 0, 0)),
            route(),
        ],
        out_specs=[pl.BlockSpec(memory_space=pl.ANY), route()],
        scratch_shapes=[
            pltpu.VMEM((2, LOCAL_ROWS, ROW_W), BF16),
            pltpu.VMEM((EXPERT_ROWS, ROW_W), BF16),
            pltpu.SemaphoreType.DMA,
        ],
    )
    return pl.pallas_call(
        functools.partial(_dispatch_body, n_prompt_steps),
        grid_spec=grid_spec,
        out_shape=[jax.ShapeDtypeStruct((n_rows, ROW_W), BF16),
                   jax.ShapeDtypeStruct((n_steps, ROUTE_ROWS, MOE_TILE), I32)],
        compiler_params=pltpu.CompilerParams(
            dimension_semantics=("arbitrary",), vmem_limit_bytes=VMEM_LIMIT),
        name="dispatch",
    )(*tables, *zero_tables, h2_p, h2_s, eidx_tiles, rank_tiles, off_tiles, wts_tiles)


def _block_rows(ref, blk):
    return ref.at[pl.ds(pl.multiple_of(blk * EXPERT_ROWS, EXPERT_ROWS), EXPERT_ROWS)]


def _experts_body(first_ref, nblk_ref, nused_ref, xs_hbm, wg_ref, wu_ref, wd_ref, ys_hbm,
                  xbuf, ybuf, wgb, wub, wdb, isem, osem):
    e = pl.program_id(0)
    n_exp = pl.num_programs(0)

    def items_of(expert):
        n_blk = nblk_ref[expert]
        n_big = n_blk // BIG_BLOCKS
        return first_ref[expert], n_big, n_big + n_blk - n_big * BIG_BLOCKS

    def item_copy(first, n_big, t, slot, rows_static, outward):
        if rows_static == BIG_ROWS:
            row0 = first * EXPERT_ROWS + t * BIG_ROWS
        else:
            row0 = first * EXPERT_ROWS + n_big * BIG_ROWS + (t - n_big) * EXPERT_ROWS
        row0 = pl.multiple_of(row0, EXPERT_ROWS)
        if outward:
            return pltpu.make_async_copy(ybuf.at[slot, pl.ds(0, rows_static)],
                                         ys_hbm.at[pl.ds(row0, rows_static)], osem.at[slot])
        return pltpu.make_async_copy(xs_hbm.at[pl.ds(row0, rows_static)],
                                     xbuf.at[slot, pl.ds(0, rows_static)], isem.at[slot])

    def for_item(first, n_big, t, slot, outward, action):
        @pl.when(t < n_big)
        def _():
            action(item_copy(first, n_big, t, slot, BIG_ROWS, outward))

        @pl.when(t >= n_big)
        def _():
            action(item_copy(first, n_big, t, slot, EXPERT_ROWS, outward))

    start = lambda c: c.start()
    wait = lambda c: c.wait()
    first, n_big, n_items = items_of(e)

    @pl.when((e == 0) & (n_items > 0))
    def _():
        for_item(first, n_big, 0, 0, False, start)

    def compute(slot, rows_static):
        xrow = xbuf[slot, 0:rows_static, :]
        x = xrow[:, :D_MODEL]
        w_parts = xrow[:, D_MODEL:].astype(F32)
        w_row = w_parts[:, 0:1] + w_parts[:, 1:2] + w_parts[:, 2:3]
        g = jnp.dot(x, wgb[...], preferred_element_type=F32)
        u = jnp.dot(x, wub[...], preferred_element_type=F32)
        a = (g * _sigmoid(g) * u).astype(BF16)
        y = jnp.dot(a, wdb[...], preferred_element_type=F32) * w_row
        ybuf[slot, 0:rows_static, :] = y.astype(BF16)

    @pl.when(n_items > 0)
    def _():
        wgb[...] = wg_ref[0].astype(BF16)
        wub[...] = wu_ref[0].astype(BF16)
        wdb[...] = wd_ref[0].astype(BF16)

        def one_item(t, carry):
            slot = lax.rem(t, 2)

            @pl.when(t + 1 < n_items)
            def _():
                for_item(first, n_big, t + 1, 1 - slot, False, start)

            for_item(first, n_big, t, slot, False, wait)

            @pl.when(t < n_big)
            def _():
                compute(slot, BIG_ROWS)

            @pl.when(t >= n_big)
            def _():
                compute(slot, EXPERT_ROWS)

            @pl.when(t >= 1)
            def _():
                for_item(first, n_big, t - 1, 1 - slot, True, wait)

            for_item(first, n_big, t, slot, True, start)
            return carry

        lax.fori_loop(0, n_items, one_item, 0)
        for_item(first, n_big, n_items - 1, lax.rem(n_items - 1, 2), True, wait)

    nxt = jnp.minimum(e + 1, n_exp - 1)
    nxt_first, nxt_big, nxt_items = items_of(nxt)

    @pl.when((e + 1 < n_exp) & (nxt_items > 0))
    def _():
        for_item(nxt_first, nxt_big, 0, 0, False, start)

    @pl.when(e == pl.num_programs(0) - 1)
    def _():
        ybuf[0, 0:EXPERT_ROWS, :] = jnp.zeros((EXPERT_ROWS, D_MODEL), BF16)
        n_blocks = ys_hbm.shape[0] // EXPERT_ROWS

        def tail_copy(blk):
            return pltpu.make_async_copy(ybuf.at[0, pl.ds(0, EXPERT_ROWS)], _block_rows(ys_hbm, blk), osem.at[0])

        def start_block(blk, carry):
            tail_copy(blk).start()
            return carry

        def wait_block(blk, carry):
            tail_copy(blk).wait()
            return carry

        lax.fori_loop(nused_ref[0], n_blocks, start_block, 0)
        lax.fori_loop(nused_ref[0], n_blocks, wait_block, 0)


def _experts_call(first_block, n_expert_blocks, n_used, xs, w_gate, w_up, w_down):
    grid_spec = pltpu.PrefetchScalarGridSpec(
        num_scalar_prefetch=3,
        grid=(N_EXPERTS,),
        in_specs=[
            pl.BlockSpec(memory_space=pl.ANY),
            pl.BlockSpec((1, D_MODEL, D_EXPERT), lambda e, *_: (e, 0, 0)),
            pl.BlockSpec((1, D_MODEL, D_EXPERT), lambda e, *_: (e, 0, 0)),
            pl.BlockSpec((1, D_EXPERT, D_MODEL), lambda e, *_: (e, 0, 0)),
        ],
        out_specs=pl.BlockSpec(memory_space=pl.ANY),
        scratch_shapes=[
            pltpu.VMEM((2, BIG_ROWS, ROW_W), BF16),
            pltpu.VMEM((2, BIG_ROWS, D_MODEL), BF16),
            pltpu.VMEM((D_MODEL, D_EXPERT), BF16),
            pltpu.VMEM((D_MODEL, D_EXPERT), BF16),
            pltpu.VMEM((D_EXPERT, D_MODEL), BF16),
            pltpu.SemaphoreType.DMA((2,)),
            pltpu.SemaphoreType.DMA((2,)),
        ],
    )
    return pl.pallas_call(
        _experts_body,
        grid_spec=grid_spec,
        out_shape=jax.ShapeDtypeStruct((xs.shape[0], D_MODEL), BF16),
        compiler_params=pltpu.CompilerParams(
            dimension_semantics=("arbitrary",), vmem_limit_bytes=VMEM_LIMIT),
        name="experts",
    )(first_block, n_expert_blocks, n_used, xs, w_gate, w_up, w_down)


def _combine_body(n_prompt_steps, lrow_ref, grow_ref, nct_ref,
                  x1p_ref, x1s_ref, h2p_ref, h2s_ref, gate_ref, gains_ref, slot_ref, ys_hbm,
                  wsg_ref, wsu_ref, wsd_ref, outp_ref, outs_ref, gath, routed, sem):
    i = pl.program_id(0)
    n_steps = pl.num_programs(0)
    par = lax.rem(i, 2)

    def fetch(step, slot):
        buf = gath.at[slot]
        _for_each_run_chunk(step, lrow_ref, grow_ref, nct_ref,
                            lambda lrow, grow: _run_copy(ys_hbm, buf, grow, lrow, sem.at[slot]).start())

    @pl.when(i == 0)
    def _():
        gath[...] = jnp.zeros_like(gath)
        fetch(0, 0)

    @pl.when(i + 1 < n_steps)
    def _():
        fetch(i + 1, 1 - par)

    is_prompt = i < n_prompt_steps
    h2 = jnp.where(is_prompt, h2p_ref[...], h2s_ref[...])
    g = jnp.dot(h2, wsg_ref[...], preferred_element_type=F32)
    u = jnp.dot(h2, wsu_ref[...], preferred_element_type=F32)
    a = (g * _sigmoid(g) * u).astype(BF16)
    ffn = jnp.dot(a, wsd_ref[...], preferred_element_type=F32)

    slot_cols = _to_sublanes(slot_ref[0].astype(F32))
    buf = gath.at[par]

    def one(c, carry):
        _run_copy(ys_hbm, buf, 0, 0, sem.at[par]).wait()
        return carry

    lax.fori_loop(0, nct_ref[i], one, 0)

    def unsort(n_rows):
        col_id = lax.broadcasted_iota(I32, (MOE_TILE, n_rows), 1).astype(F32)
        take = jnp.zeros((MOE_TILE, n_rows), F32)
        for kk in range(TOP_K):
            take = jnp.where(col_id == slot_cols[:, kk:kk + 1], 1.0, take)
        routed[...] = jnp.dot(take.astype(BF16), buf[0:n_rows, :], preferred_element_type=F32)

    _for_row_option(nct_ref[i] * RUN_ROWS, unsort)
    ffn = ffn + routed[...]

    quarter = MOE_TILE // gate_ref.shape[0]
    gate = jnp.concatenate(
        [jnp.broadcast_to(gate_ref[q:q + 1, :], (quarter, D_MODEL)) for q in range(gate_ref.shape[0])], axis=0)
    x1 = jnp.where(is_prompt, x1p_ref[...], x1s_ref[...])
    out = x1 + gate * (_rms(ffn) * gains_ref[3:4, :])

    @pl.when(is_prompt)
    def _():
        outp_ref[...] = out

    @pl.when(jnp.logical_not(is_prompt))
    def _():
        outs_ref[...] = out


def _combine_call(tables, x1_p, x1_s, h2_p, h2_s, gate_groups, gains, slot_tiles, ys,
                  ws_gate, ws_up, ws_down):
    n_prompt, n_sample = x1_p.shape[0], x1_s.shape[0]
    n_prompt_steps = n_prompt // MOE_TILE
    n_steps = n_prompt_steps + n_sample // MOE_TILE
    groups_per_tile = gate_groups.shape[0] // n_steps
    tok_p = lambda: pl.BlockSpec((MOE_TILE, D_MODEL), lambda i, *_: (jnp.minimum(i, n_prompt_steps - 1), 0))
    tok_s = lambda: pl.BlockSpec((MOE_TILE, D_MODEL), lambda i, *_: (jnp.maximum(i - n_prompt_steps, 0), 0))
    whole = lambda shape: pl.BlockSpec(shape, lambda i, *_: (0,) * len(shape))
    grid_spec = pltpu.PrefetchScalarGridSpec(
        num_scalar_prefetch=3,
        grid=(n_steps,),
        in_specs=[
            tok_p(), tok_s(), tok_p(), tok_s(),
            pl.BlockSpec((groups_per_tile, D_MODEL), lambda i, *_: (i, 0)),
            whole((4, D_MODEL)),
            pl.BlockSpec((1, ROUTE_ROWS, MOE_TILE), lambda i, *_: (i, 0, 0)),
            pl.BlockSpec(memory_space=pl.ANY),
            whole((D_MODEL, D_SHARED)), whole((D_MODEL, D_SHARED)), whole((D_SHARED, D_MODEL)),
        ],
        out_specs=[tok_p(), tok_s()],
        scratch_shapes=[
            pltpu.VMEM((2, LOCAL_ROWS, D_MODEL), BF16),
            pltpu.VMEM((MOE_TILE, D_MODEL), F32),
            pltpu.SemaphoreType.DMA((2,)),
        ],
    )
    return pl.pallas_call(
        functools.partial(_combine_body, n_prompt_steps),
        grid_spec=grid_spec,
        out_shape=[jax.ShapeDtypeStruct((n_prompt, D_MODEL), F32),
                   jax.ShapeDtypeStruct((n_sample, D_MODEL), F32)],
        compiler_params=pltpu.CompilerParams(
            dimension_semantics=("arbitrary",), vmem_limit_bytes=VMEM_LIMIT),
        name="combine",
    )(*tables, x1_p, x1_s, h2_p, h2_s, gate_groups, gains, slot_tiles, ys, ws_gate, ws_up, ws_down)


def kernel(x_prompt, x_sample, c_prompt, c_sample, state_pool, cache_k, cache_v, w_ada, b_ada, norm_gains,
           w_in, w_pool, pool_scale, attn_sinks, w_out, w_router, router_bias, w_gate, w_up, w_down,
           ws_gate, ws_up, ws_down):
    assert w_ada.shape[0] == 1, "single-layer kernel"
    bsz, seq, _ = x_prompt.shape
    dbsz, dseq, _ = x_sample.shape
    n_prompt, n_sample = bsz * seq, dbsz * dseq
    n_tok = n_prompt + n_sample
    assert dseq == CHUNK and seq % MOE_TILE == 0 and n_sample % MOE_TILE == 0

    w_in0, w_out0 = w_in[0], w_out[0]
    wq = w_in0[:, POOL_WIDTH:POOL_WIDTH + ATTN_WIDTH].reshape(D_MODEL, N_KV_HEADS, GQ, HEAD_DIM)
    wq = jnp.transpose(wq, (0, 2, 1, 3)).reshape(D_MODEL, ATTN_WIDTH)
    w_in_p = jnp.concatenate(
        [w_in0[:, :POOL_WIDTH], wq, w_in0[:, POOL_WIDTH + ATTN_WIDTH:]], axis=1).astype(BF16)
    wo = w_out0[POOL_WIDTH:].reshape(N_KV_HEADS, GQ, HEAD_DIM, D_MODEL)
    wo = jnp.transpose(wo, (1, 0, 2, 3)).reshape(ATTN_WIDTH, D_MODEL)
    w_out_p = jnp.concatenate([w_out0[:POOL_WIDTH], wo], axis=0).astype(BF16)
    w_pool_b = w_pool[0].astype(BF16)
    pscale = pool_scale[0].reshape(1, POOL_WIDTH)
    wr_t = w_router[0].T
    wr_hi = wr_t.astype(BF16)
    wr_lo = (wr_t - wr_hi.astype(F32)).astype(BF16)
    wr_split = jnp.concatenate([wr_hi, wr_lo], axis=0)
    rbias = router_bias[0].reshape(N_EXPERTS, 1)
    gains = norm_gains[0]
    sinks = attn_sinks[0]

    mod = _ada_call(jnp.concatenate([c_prompt, c_sample], axis=0), w_ada[0], b_ada[0])
    mod = mod.reshape(bsz + dbsz, 6, D_MODEL)

    zeros_pool = jnp.zeros((bsz, POOL_HIST_PAD, POOL_WIDTH), F32)
    zeros_kv = jnp.zeros((bsz, WINDOW, KV_WIDTH), F32)
    cnt0 = jnp.zeros((N_EXPERTS, LANES), F32)
    (x1_p, h2_p, pool_p, nk_p, nv_p, eidx_p, wts_p, rank_p, tcnt_p) = _front_call(
        x_prompt, mod[:bsz], gains, zeros_pool, zeros_kv, zeros_kv, cnt0, sinks,
        w_in_p, w_pool_b, pscale, w_out_p, wr_split, rbias, tile=MOE_TILE, has_cache=False)
    cnt_prompt = jnp.broadcast_to(jnp.sum(tcnt_p[:, 0, :], axis=0)[:, None], (N_EXPERTS, LANES))

    hist_pad = ATTN_KEYS - CHUNK - WINDOW
    pool_s0 = jnp.pad(state_pool[0], ((0, 0), (POOL_HIST_PAD - state_pool.shape[2], 0), (0, 0)))
    ck = jnp.pad(cache_k[0].reshape(dbsz, WINDOW, KV_WIDTH), ((0, 0), (hist_pad, 0), (0, 0)))
    cv = jnp.pad(cache_v[0].reshape(dbsz, WINDOW, KV_WIDTH), ((0, 0), (hist_pad, 0), (0, 0)))
    (x1_s, h2_s, pool_s, nk_s, nv_s, eidx_s, wts_s, rank_s, tcnt_s) = _front_call(
        x_sample, mod[bsz:], gains, pool_s0, ck, cv, cnt_prompt, sinks,
        w_in_p, w_pool_b, pscale, w_out_p, wr_split, rbias, tile=CHUNK, has_cache=True)

    def moe_tiles(route_p, route_s):
        per = MOE_TILE // CHUNK
        regrouped = jnp.transpose(route_s.reshape(-1, per, ROUTE_ROWS, CHUNK), (0, 2, 1, 3))
        return jnp.concatenate([route_p, regrouped.reshape(-1, ROUTE_ROWS, MOE_TILE)], axis=0)

    n_steps = n_tok // MOE_TILE
    cnt_td = jnp.concatenate(
        [tcnt_p[:, 0, :], tcnt_s[:, 0, :].reshape(-1, MOE_TILE // CHUNK, N_EXPERTS).sum(axis=1)],
        axis=0).astype(I32)
    c8 = (cnt_td + RUN_ROWS - 1) // RUN_ROWS * RUN_ROWS
    rows8 = jnp.sum(c8, axis=0)
    padded = (rows8 + EXPERT_ROWS - 1) // EXPERT_ROWS * EXPERT_ROWS
    pends = jnp.cumsum(padded)
    pstarts = pends - padded
    base = jnp.cumsum(cnt_td, axis=0) - cnt_td
    grow = pstarts[None, :] + jnp.cumsum(c8, axis=0) - c8
    lo8 = jnp.cumsum(c8, axis=1) - c8
    nch = c8 // RUN_ROWS
    nct = jnp.sum(nch, axis=1)
    off_tiles = (lo8 - base).astype(F32).reshape(n_steps, N_EXPERTS, 1)
    cum = jnp.cumsum(nch, axis=1)
    j = jnp.arange(MAX_CHUNKS, dtype=I32)
    owner = jnp.sum(j[None, :, None] >= cum[:, None, :], axis=-1)
    owns = owner[:, :, None] == jnp.arange(N_EXPERTS, dtype=I32)[None, None, :]
    of_owner = lambda tab: jnp.sum(jnp.where(owns, tab[:, None, :], 0), axis=-1)
    within = RUN_ROWS * (j[None, :] - of_owner(cum - nch))
    tables = ((of_owner(lo8) + within).reshape(-1).astype(I32),
              (of_owner(grow) + within).reshape(-1).astype(I32), nct.astype(I32))
    n_blocks = -(-(n_tok * TOP_K + n_steps * N_EXPERTS * (RUN_ROWS - 1) + N_EXPERTS * (EXPERT_ROWS - 1))
                 // EXPERT_ROWS)
    n_used = (pends[-1] // EXPERT_ROWS).astype(I32).reshape(1)
    zero_tables = ((pstarts + rows8).astype(I32), ((padded - rows8) // RUN_ROWS).astype(I32), n_used)

    h2_pf, h2_sf = h2_p.reshape(n_prompt, D_MODEL), h2_s.reshape(n_sample, D_MODEL)
    xs, slot_tiles = _dispatch_call(
        tables, zero_tables, h2_pf, h2_sf, moe_tiles(eidx_p, eidx_s), moe_tiles(rank_p, rank_s), off_tiles,
        moe_tiles(wts_p, wts_s), n_blocks * EXPERT_ROWS)
    ys = _experts_call((pstarts // EXPERT_ROWS).astype(I32), (padded // EXPERT_ROWS).astype(I32), n_used,
                       xs, w_gate[0], w_up[0], w_down[0])
    gate_groups = jnp.concatenate(
        [jnp.repeat(mod[:bsz, 5], seq // GATE_GROUP, axis=0),
         jnp.repeat(mod[bsz:, 5], dseq // GATE_GROUP, axis=0)], axis=0)
    y_p, y_s = _combine_call(
        tables, x1_p.reshape(n_prompt, D_MODEL), x1_s.reshape(n_sample, D_MODEL), h2_pf, h2_sf,
        gate_groups, gains, slot_tiles, ys,
        ws_gate[0].astype(BF16), ws_up[0].astype(BF16), ws_down[0].astype(BF16))

    n_hist = state_pool.shape[2]
    kv_shape = (1, -1, WINDOW, N_KV_HEADS, HEAD_DIM)
    return (y_p.reshape(bsz, seq, D_MODEL), y_s.reshape(dbsz, dseq, D_MODEL),
            pool_p[None, :, POOL_HIST_PAD - n_hist:], nk_p.reshape(kv_shape), nv_p.reshape(kv_shape),
            pool_s[None, :, POOL_HIST_PAD - n_hist:], nk_s.reshape(kv_shape), nv_s.reshape(kv_shape))
```

```python
import functools

import jax
import jax.numpy as jnp
import numpy as np
from jax import lax
from jax.experimental import pallas as pl
from jax.experimental.pallas import tpu as pltpu

F32 = jnp.float32
BF16 = jnp.bfloat16
I32 = jnp.int32

D_MODEL = 1024
CHUNK = 64
POOL_WIDTH = 512
POOL_WINDOWS = (2, 4, 8, 16)
POOL_GROUP_W = 128
POOL_HIST_PAD = 16
HEAD_DIM = 64
N_HEADS = 8
N_KV_HEADS = 2
GQ = N_HEADS // N_KV_HEADS
ATTN_WIDTH = N_HEADS * HEAD_DIM
KV_WIDTH = N_KV_HEADS * HEAD_DIM
IN_WIDTH = POOL_WIDTH + ATTN_WIDTH + 2 * KV_WIDTH
WINDOW = 128
N_EXPERTS = 64
TOP_K = 6
N_EXPERT_GROUPS = 8
GROUP_SIZE = N_EXPERTS // N_EXPERT_GROUPS
TOPK_GROUPS = 4
D_EXPERT = 256
D_SHARED = 256
ROUTED_SCALE = 2.5
NORM_EPS = 1e-6
NEG_BIG = -1e30

LANES = 128
SUBLANES = 8
ATTN_KEYS = 2 * LANES
ROUTE_ROWS = 8
EXPERT_ROWS = 256
BIG_BLOCKS = 4
BIG_ROWS = BIG_BLOCKS * EXPERT_ROWS
MOE_TILE = 256
GATE_GROUP = MOE_TILE // SUBLANES
RUN_ROWS = 2 * SUBLANES
LOCAL_ROWS = 2560
LOCAL_ROW_OPTIONS = (2048, 2176, LOCAL_ROWS)
MAX_CHUNKS = LOCAL_ROWS // RUN_ROWS
ROW_W = D_MODEL
VMEM_LIMIT = 56 * 1024 * 1024

assert TOP_K * MOE_TILE + N_EXPERTS * (RUN_ROWS - 1) <= LOCAL_ROWS

_NT = (((1,), (1,)), ((), ()))


def _rms(v):
    return v * lax.rsqrt(jnp.mean(v * v, axis=-1, keepdims=True) + NORM_EPS)


def _sigmoid(v):
    return 1.0 / (1.0 + jnp.exp(-v))


def _split3(v):
    hi = v.astype(BF16)
    r1 = v - hi.astype(F32)
    mid = r1.astype(BF16)
    lo = (r1 - mid.astype(F32)).astype(BF16)
    return hi, mid, lo


def _to_sublanes(rows):
    n = rows.shape[1]
    hi, mid, lo = _split3(rows)
    eye = (lax.broadcasted_iota(I32, (n, n), 0) == lax.broadcasted_iota(I32, (n, n), 1)).astype(BF16)
    return (lax.dot_general(eye, hi, _NT, preferred_element_type=F32)
            + lax.dot_general(eye, mid, _NT, preferred_element_type=F32)
            + lax.dot_general(eye, lo, _NT, preferred_element_type=F32))


def _ada_body(c_ref, w_ref, b_ref, o_ref):
    c = c_ref[...]
    s = c * _sigmoid(c)
    o_ref[...] = jnp.dot(s, w_ref[...], preferred_element_type=F32) + b_ref[...]


def _ada_call(c_all, w_ada, b_ada):
    nb = c_all.shape[0]
    n_out = w_ada.shape[1]
    tile = D_MODEL
    return pl.pallas_call(
        _ada_body,
        grid=(n_out // tile,),
        in_specs=[
            pl.BlockSpec((nb, D_MODEL), lambda j: (0, 0)),
            pl.BlockSpec((D_MODEL, tile), lambda j: (0, j)),
            pl.BlockSpec((1, tile), lambda j: (0, j)),
        ],
        out_specs=pl.BlockSpec((nb, tile), lambda j: (0, j)),
        out_shape=jax.ShapeDtypeStruct((nb, n_out), F32),
        compiler_params=pltpu.CompilerParams(dimension_semantics=("arbitrary",)),
        name="ada",
    )(c_all, w_ada, b_ada.reshape(1, n_out))


def _front_body(tile, has_cache,
                sinks_ref, x_ref, mod_ref, gains_ref, hp_ref, hk_ref, hv_ref, cnt0_ref,
                win_ref, wpool_ref, pscale_ref, wout_ref, wrt_ref, rbias_ref,
                x1_ref, h2_ref, npool_ref, nk_ref, nv_ref, eidx_ref, wts_ref, rank_ref, tcnt_ref,
                ubuf, khist, vhist, run):
    b = pl.program_id(0)
    i = pl.program_id(1)
    sub_q = min(tile, LANES)
    n_sub = tile // sub_q
    hist_keys = ATTN_KEYS - sub_q
    n_keys = hist_keys + tile

    @pl.when(i == 0)
    def _():
        ubuf[0:POOL_HIST_PAD, :] = hp_ref[0]
        khist[...] = hk_ref[0]
        vhist[...] = hv_ref[0]

    @pl.when((b == 0) & (i == 0))
    def _():
        run[...] = cnt0_ref[...]

    mod = mod_ref[0]
    gains = gains_ref[...]
    xt = x_ref[0]

    h = _rms(xt) * gains[0:1] * (1.0 + mod[1:2]) + mod[0:1]
    proj = jnp.dot(h.astype(BF16), win_ref[...], preferred_element_type=F32)
    u = proj[:, :POOL_WIDTH]
    o_k = POOL_WIDTH + ATTN_WIDTH
    k_new = proj[:, o_k:o_k + KV_WIDTH]
    v_new = proj[:, o_k + KV_WIDTH:]

    kw = jnp.concatenate([khist[...], k_new], axis=0)
    vw = jnp.concatenate([vhist[...], v_new], axis=0)
    kwb = kw.astype(BF16)
    vwb = vw.astype(BF16)
    qrow = lax.broadcasted_iota(I32, (sub_q, ATTN_KEYS), 0)
    kpos = lax.broadcasted_iota(I32, (sub_q, ATTN_KEYS), 1) - hist_keys
    qchunk = jnp.right_shift(qrow, 6)
    vis_band = (kpos >= CHUNK * (qchunk - 2)) & (kpos < CHUNK * (qchunk + 1))
    side0 = lax.broadcasted_iota(I32, (sub_q, LANES), 1) < HEAD_DIM
    attn_rows = []
    for r in range(n_sub):
        vis = vis_band
        if not has_cache:
            vis = vis & (kpos + (i * tile + r * sub_q) >= 0)
        kb = kwb[r * sub_q:r * sub_q + ATTN_KEYS]
        vb = vwb[r * sub_q:r * sub_q + ATTN_KEYS]
        blocks = []
        for j in range(GQ):
            qp = proj[r * sub_q:(r + 1) * sub_q, POOL_WIDTH + LANES * j:POOL_WIDTH + LANES * (j + 1)]
            qp = qp * (HEAD_DIM ** -0.5)
            outs = []
            for s in range(N_KV_HEADS):
                keep = side0 if s == 0 else jnp.logical_not(side0)
                qm = jnp.where(keep, qp, 0.0).astype(BF16)
                sc = lax.dot_general(qm, kb, _NT, preferred_element_type=F32)
                sc = jnp.where(vis, sc, NEG_BIG)
                sink = sinks_ref[j + GQ * s]
                m = jnp.maximum(jnp.max(sc, axis=-1, keepdims=True), sink)
                p = jnp.exp(sc - m)
                den = jnp.sum(p, axis=-1, keepdims=True) + jnp.exp(sink - m)
                p = (p / den).astype(BF16)
                outs.append(jnp.dot(p, vb, preferred_element_type=F32))
            blocks.append(jnp.where(side0, outs[0], outs[1]))
        attn_rows.append(jnp.concatenate(blocks, axis=-1))
    attn = attn_rows[0] if n_sub == 1 else jnp.concatenate(attn_rows, axis=0)

    ubuf[POOL_HIST_PAD:POOL_HIST_PAD + tile, :] = u
    if has_cache:
        seen = None
    else:
        seen = (lax.broadcasted_iota(I32, (tile, 1), 0) + i * tile + 1).astype(F32)
    pool_blocks = []
    for g, w in enumerate(POOL_WINDOWS):
        cols = slice(POOL_GROUP_W * g, POOL_GROUP_W * (g + 1))
        acc = u[:, cols]
        for s in range(1, w):
            acc = acc + ubuf[POOL_HIST_PAD - s:POOL_HIST_PAD - s + tile, cols]
        cnt = float(w) if seen is None else jnp.minimum(seen, float(w))
        dlt = acc / cnt - u[:, cols]
        pool_blocks.append(jnp.dot(dlt.astype(BF16), wpool_ref[g], preferred_element_type=F32))
    pool = jnp.concatenate(pool_blocks, axis=-1) * pscale_ref[...]

    mixin = jnp.concatenate([pool, attn], axis=-1).astype(BF16)
    mix = jnp.dot(mixin, wout_ref[...], preferred_element_type=F32)
    x1 = xt + mod[2:3] * (_rms(mix) * gains[1:2])
    x1_ref[0] = x1

    new_hist = ubuf[tile:tile + POOL_HIST_PAD, :]
    ubuf[0:POOL_HIST_PAD, :] = new_hist
    npool_ref[0] = new_hist
    khist[...] = kw[tile:, :]
    vhist[...] = vw[tile:, :]
    nk_ref[0] = kw[n_keys - WINDOW:, :]
    nv_ref[0] = vw[n_keys - WINDOW:, :]

    h2f = _rms(x1) * gains[2:3] * (1.0 + mod[4:5]) + mod[3:4]
    h2hi = h2f.astype(BF16)
    h2_ref[0] = h2hi
    h2lo = (h2f - h2hi.astype(F32)).astype(BF16)
    wrt = wrt_ref[...]
    part = lax.dot_general(wrt, h2hi, _NT, preferred_element_type=F32)
    logits = (part[:N_EXPERTS] + part[N_EXPERTS:]
              + lax.dot_general(wrt[:N_EXPERTS], h2lo, _NT, preferred_element_type=F32))
    scores = _sigmoid(logits)
    sel = scores + rbias_ref[...]

    sub_g = lax.broadcasted_iota(I32, (GROUP_SIZE, tile), 0).astype(F32)
    gscore = jnp.zeros((N_EXPERT_GROUPS, tile), F32)
    for gi in range(N_EXPERT_GROUPS):
        blk = sel[GROUP_SIZE * gi:GROUP_SIZE * (gi + 1), :]
        m1 = jnp.max(blk, axis=0, keepdims=True)
        i1 = jnp.min(jnp.where(blk == m1, sub_g, float(GROUP_SIZE)), axis=0, keepdims=True)
        m2 = jnp.max(jnp.where(sub_g == i1, -jnp.inf, blk), axis=0, keepdims=True)
        gscore = jnp.where(sub_g == gi, m1 + m2, gscore)
    chosen = jnp.zeros((N_EXPERT_GROUPS, tile), F32)
    for _ in range(TOPK_GROUPS):
        m = jnp.max(gscore, axis=0, keepdims=True)
        idx = jnp.min(jnp.where(gscore == m, sub_g, float(N_EXPERT_GROUPS)), axis=0, keepdims=True)
        pick = sub_g == idx
        chosen = jnp.where(pick, 1.0, chosen)
        gscore = jnp.where(pick, -jnp.inf, gscore)
    emask = jnp.concatenate(
        [jnp.broadcast_to(chosen[gi:gi + 1, :], (GROUP_SIZE, tile)) for gi in range(N_EXPERT_GROUPS)], axis=0)
    selm = jnp.where(emask > 0.0, sel, -jnp.inf)

    sub_e = lax.broadcasted_iota(I32, (N_EXPERTS, tile), 0).astype(F32)
    picks, idxs, raw_w = [], [], []
    for _ in range(TOP_K):
        m = jnp.max(selm, axis=0, keepdims=True)
        idx = jnp.min(jnp.where(selm == m, sub_e, float(N_EXPERTS)), axis=0, keepdims=True)
        pick = sub_e == idx
        raw_w.append(jnp.sum(jnp.where(pick, scores, 0.0), axis=0, keepdims=True))
        selm = jnp.where(pick, -jnp.inf, selm)
        picks.append(pick)
        idxs.append(idx)
    wsum = raw_w[0]
    for kk in range(1, TOP_K):
        wsum = wsum + raw_w[kk]

    onehot = jnp.zeros((N_EXPERTS, tile), F32)
    for kk in range(TOP_K):
        onehot = jnp.where(picks[kk], 1.0, onehot)
    onehot_b = onehot.astype(BF16)
    tri = (lax.broadcasted_iota(I32, (tile, tile), 0) < lax.broadcasted_iota(I32, (tile, tile), 1)).astype(BF16)
    before = jnp.dot(onehot_b, tri, preferred_element_type=F32) + run[:, 0:1]
    sub_r = lax.broadcasted_iota(I32, (ROUTE_ROWS, tile), 0)
    eidx_o = jnp.zeros((ROUTE_ROWS, tile), I32)
    wts_o = jnp.zeros((ROUTE_ROWS, tile), F32)
    rank_o = jnp.zeros((ROUTE_ROWS, tile), I32)
    for kk in range(TOP_K):
        rk = jnp.sum(jnp.where(picks[kk], before, 0.0), axis=0, keepdims=True).astype(I32)
        eidx_o = jnp.where(sub_r == kk, idxs[kk].astype(I32), eidx_o)
        wts_o = jnp.where(sub_r == kk, raw_w[kk] / wsum * ROUTED_SCALE, wts_o)
        rank_o = jnp.where(sub_r == kk, rk, rank_o)
    eidx_ref[0] = eidx_o
    wts_ref[0] = wts_o
    rank_ref[0] = rank_o
    tcnt_ref[0] = lax.dot_general(jnp.ones((ROUTE_ROWS, tile), BF16), onehot_b, _NT,
                                  preferred_element_type=F32)
    run[...] = run[...] + jnp.sum(onehot, axis=1, keepdims=True)


def _front_call(x, mod, gains, hist_pool, hist_k, hist_v, cnt0, sinks,
                w_in, w_pool, pool_scale, w_out, wr_t, rbias, *, tile, has_cache):
    bsz, seq, _ = x.shape
    n_tiles = seq // tile
    hist_keys = hist_k.shape[1]
    assert hist_keys == ATTN_KEYS - min(tile, LANES)
    body = functools.partial(_front_body, tile, has_cache)
    whole = lambda shape: pl.BlockSpec(shape, lambda b, i: (0,) * len(shape))
    per_b = lambda shape: pl.BlockSpec((1,) + shape, lambda b, i: (b,) + (0,) * len(shape))
    route = pl.BlockSpec((1, ROUTE_ROWS, tile), lambda b, i: (b * n_tiles + i, 0, 0))
    out_shape = [
        jax.ShapeDtypeStruct((bsz, seq, D_MODEL), F32),
        jax.ShapeDtypeStruct((bsz, seq, D_MODEL), BF16),
        jax.ShapeDtypeStruct((bsz, POOL_HIST_PAD, POOL_WIDTH), F32),
        jax.ShapeDtypeStruct((bsz, WINDOW, KV_WIDTH), F32),
        jax.ShapeDtypeStruct((bsz, WINDOW, KV_WIDTH), F32),
        jax.ShapeDtypeStruct((bsz * n_tiles, ROUTE_ROWS, tile), I32),
        jax.ShapeDtypeStruct((bsz * n_tiles, ROUTE_ROWS, tile), F32),
        jax.ShapeDtypeStruct((bsz * n_tiles, ROUTE_ROWS, tile), I32),
        jax.ShapeDtypeStruct((bsz * n_tiles, ROUTE_ROWS, N_EXPERTS), F32),
    ]
    return pl.pallas_call(
        body,
        grid=(bsz, n_tiles),
        in_specs=[
            pl.BlockSpec(memory_space=pltpu.SMEM),
            pl.BlockSpec((1, tile, D_MODEL), lambda b, i: (b, i, 0)),
            per_b((6, D_MODEL)),
            whole((4, D_MODEL)),
            per_b((POOL_HIST_PAD, POOL_WIDTH)),
            per_b((hist_keys, KV_WIDTH)),
            per_b((hist_keys, KV_WIDTH)),
            whole((N_EXPERTS, LANES)),
            whole((D_MODEL, IN_WIDTH)),
            whole((len(POOL_WINDOWS), POOL_GROUP_W, POOL_GROUP_W)),
            whole((1, POOL_WIDTH)),
            whole((D_MODEL, D_MODEL)),
            whole((2 * N_EXPERTS, D_MODEL)),
            whole((N_EXPERTS, 1)),
        ],
        out_specs=[
            pl.BlockSpec((1, tile, D_MODEL), lambda b, i: (b, i, 0)),
            pl.BlockSpec((1, tile, D_MODEL), lambda b, i: (b, i, 0)),
            per_b((POOL_HIST_PAD, POOL_WIDTH)),
            per_b((WINDOW, KV_WIDTH)),
            per_b((WINDOW, KV_WIDTH)),
            route, route, route,
            pl.BlockSpec((1, ROUTE_ROWS, N_EXPERTS), lambda b, i: (b * n_tiles + i, 0, 0)),
        ],
        out_shape=out_shape,
        scratch_shapes=[
            pltpu.VMEM((tile + POOL_HIST_PAD, POOL_WIDTH), F32),
            pltpu.VMEM((hist_keys, KV_WIDTH), F32),
            pltpu.VMEM((hist_keys, KV_WIDTH), F32),
            pltpu.VMEM((N_EXPERTS, LANES), F32),
        ],
        compiler_params=pltpu.CompilerParams(
            dimension_semantics=("arbitrary", "arbitrary"), vmem_limit_bytes=VMEM_LIMIT),
        name="front_cached" if has_cache else "front_prompt",
    )(sinks, x, mod, gains, hist_pool, hist_k, hist_v, cnt0,
      w_in, w_pool, pool_scale, w_out, wr_t, rbias)


def _run_copy(src, dst, s_row, d_row, sem):
    return pltpu.make_async_copy(src.at[pl.ds(s_row, RUN_ROWS)], dst.at[pl.ds(d_row, RUN_ROWS)], sem)


def _for_each_run_chunk(step, lrow_ref, grow_ref, nct_ref, fn):
    def per_chunk(j, carry):
        idx = step * MAX_CHUNKS + j
        fn(pl.multiple_of(lrow_ref[idx], RUN_ROWS), pl.multiple_of(grow_ref[idx], RUN_ROWS))
        return carry

    lax.fori_loop(0, nct_ref[step], per_chunk, 0)


def _for_row_option(used_rows, fn):
    lower = 0
    for n_rows in LOCAL_ROW_OPTIONS:
        @pl.when((used_rows > lower) & (used_rows <= n_rows))
        def _(n_rows=n_rows):
            fn(n_rows)
        lower = n_rows


def _dispatch_body(n_prompt_steps, lrow_ref, grow_ref, nct_ref, zrow_ref, znch_ref, nused_ref,
                   h2p_ref, h2s_ref, eidx_ref, rank_ref, off_ref, xs_out, slot_out, loc, zrows, sem):
    i = pl.program_id(0)
    n_steps = pl.num_programs(0)
    par = lax.rem(i, 2)
    h2 = jnp.where(i < n_prompt_steps, h2p_ref[...], h2s_ref[...])
    eidx = eidx_ref[0]
    rank = rank_ref[0]
    off = off_ref[0]
    expert_id = lax.broadcasted_iota(I32, (N_EXPERTS, MOE_TILE), 0)
    sub_r = lax.broadcasted_iota(I32, (ROUTE_ROWS, MOE_TILE), 0)
    slots = []
    slot_o = jnp.zeros((ROUTE_ROWS, MOE_TILE), I32)
    for kk in range(TOP_K):
        mine = jnp.sum(jnp.where(expert_id == eidx[kk:kk + 1, :], off, 0.0), axis=0, keepdims=True)
        slots.append(rank[kk:kk + 1, :] + mine.astype(I32))
        slot_o = jnp.where(sub_r == kk, slots[kk], slot_o)
    slot_out[0] = slot_o
    buf = loc.at[par]

    def sort_tile(n_rows):
        row_id = lax.broadcasted_iota(I32, (n_rows, MOE_TILE), 0).astype(jnp.int16)
        sel = jnp.zeros((n_rows, MOE_TILE), BF16)
        for kk in range(TOP_K):
            sel = jnp.where(row_id == slots[kk].astype(jnp.int16), jnp.ones((), BF16), sel)
        buf[0:n_rows, :] = jnp.dot(sel, h2, preferred_element_type=F32).astype(BF16)

    _for_row_option(nct_ref[i] * RUN_ROWS, sort_tile)

    def drain(n_chunks):
        def one(c, carry):
            _run_copy(buf, xs_out, 0, 0, sem).wait()
            return carry
        lax.fori_loop(0, n_chunks, one, 0)

    @pl.when(i > 0)
    def _():
        drain(nct_ref[i - 1])

    _for_each_run_chunk(i, lrow_ref, grow_ref, nct_ref,
                        lambda lrow, grow: _run_copy(buf, xs_out, lrow, grow, sem).start())

    @pl.when(i == n_steps - 1)
    def _():
        zrows[...] = jnp.zeros_like(zrows)

        def per_expert(e, total):
            def per_chunk(c, carry):
                _run_copy(zrows, xs_out, 0, pl.multiple_of(zrow_ref[e] + RUN_ROWS * c, RUN_ROWS), sem).start()
                return carry
            lax.fori_loop(0, znch_ref[e], per_chunk, 0)
            return total + znch_ref[e]

        n_zero = lax.fori_loop(0, N_EXPERTS, per_expert, 0)
        drain(nct_ref[i] + n_zero)

        def block_copy(blk):
            return pltpu.make_async_copy(
                zrows, xs_out.at[pl.ds(pl.multiple_of(blk * EXPERT_ROWS, EXPERT_ROWS), EXPERT_ROWS)], sem)

        n_blocks = xs_out.shape[0] // EXPERT_ROWS

        def start_block(blk, carry):
            block_copy(blk).start()
            return carry

        def wait_block(blk, carry):
            block_copy(blk).wait()
            return carry

        lax.fori_loop(nused_ref[0], n_blocks, start_block, 0)
        lax.fori_loop(nused_ref[0], n_blocks, wait_block, 0)


def _dispatch_call(tables, zero_tables, h2_p, h2_s, eidx_tiles, rank_tiles, off_tiles, n_rows):
    n_prompt_steps = h2_p.shape[0] // MOE_TILE
    n_steps = n_prompt_steps + h2_s.shape[0] // MOE_TILE
    route = lambda: pl.BlockSpec((1, ROUTE_ROWS, MOE_TILE), lambda i, *_: (i, 0, 0))
    grid_spec = pltpu.PrefetchScalarGridSpec(
        num_scalar_prefetch=6,
        grid=(n_steps,),
        in_specs=[
            pl.BlockSpec((MOE_TILE, D_MODEL), lambda i, *_: (jnp.minimum(i, n_prompt_steps - 1), 0)),
            pl.BlockSpec((MOE_TILE, D_MODEL), lambda i, *_: (jnp.maximum(i - n_prompt_steps, 0), 0)),
            route(), route(),
            pl.BlockSpec((1, N_EXPERTS, 1), lambda i, *_: (i, 0, 0)),
        ],
        out_specs=[pl.BlockSpec(memory_space=pl.ANY), route()],
        scratch_shapes=[
            pltpu.VMEM((2, LOCAL_ROWS, ROW_W), BF16),
            pltpu.VMEM((EXPERT_ROWS, ROW_W), BF16),
            pltpu.SemaphoreType.DMA,
        ],
    )
    return pl.pallas_call(
        functools.partial(_dispatch_body, n_prompt_steps),
        grid_spec=grid_spec,
        out_shape=[jax.ShapeDtypeStruct((n_rows, ROW_W), BF16),
                   jax.ShapeDtypeStruct((n_steps, ROUTE_ROWS, MOE_TILE), I32)],
        compiler_params=pltpu.CompilerParams(
            dimension_semantics=("arbitrary",), vmem_limit_bytes=VMEM_LIMIT),
        name="dispatch",
    )(*tables, *zero_tables, h2_p, h2_s, eidx_tiles, rank_tiles, off_tiles)


def _block_rows(ref, blk):
    return ref.at[pl.ds(pl.multiple_of(blk * EXPERT_ROWS, EXPERT_ROWS), EXPERT_ROWS)]


def _experts_body(first_ref, nblk_ref, nused_ref, xs_hbm, wg_ref, wu_ref, wd_ref, ys_hbm,
                  xbuf, ybuf, wgb, wub, wdb, isem, osem):
    e = pl.program_id(0)
    n_exp = pl.num_programs(0)

    def items_of(expert):
        n_blk = nblk_ref[expert]
        n_big = n_blk // BIG_BLOCKS
        return first_ref[expert], n_big, n_big + n_blk - n_big * BIG_BLOCKS

    def item_copy(first, n_big, t, slot, rows_static, outward):
        if rows_static == BIG_ROWS:
            row0 = first * EXPERT_ROWS + t * BIG_ROWS
        else:
            row0 = first * EXPERT_ROWS + n_big * BIG_ROWS + (t - n_big) * EXPERT_ROWS
        row0 = pl.multiple_of(row0, EXPERT_ROWS)
        if outward:
            return pltpu.make_async_copy(ybuf.at[slot, pl.ds(0, rows_static)],
                                         ys_hbm.at[pl.ds(row0, rows_static)], osem.at[slot])
        return pltpu.make_async_copy(xs_hbm.at[pl.ds(row0, rows_static)],
                                     xbuf.at[slot, pl.ds(0, rows_static)], isem.at[slot])

    def for_item(first, n_big, t, slot, outward, action):
        @pl.when(t < n_big)
        def _():
            action(item_copy(first, n_big, t, slot, BIG_ROWS, outward))

        @pl.when(t >= n_big)
        def _():
            action(item_copy(first, n_big, t, slot, EXPERT_ROWS, outward))

    start = lambda c: c.start()
    wait = lambda c: c.wait()
    first, n_big, n_items = items_of(e)

    @pl.when((e == 0) & (n_items > 0))
    def _():
        for_item(first, n_big, 0, 0, False, start)

    def compute(slot, rows_static):
        x = xbuf[slot, 0:rows_static, :]
        g = jnp.dot(x, wgb[...], preferred_element_type=F32)
        u = jnp.dot(x, wub[...], preferred_element_type=F32)
        a = (g * _sigmoid(g) * u).astype(BF16)
        ybuf[slot, 0:rows_static, :] = jnp.dot(a, wdb[...], preferred_element_type=F32).astype(BF16)

    @pl.when(n_items > 0)
    def _():
        wgb[...] = wg_ref[0].astype(BF16)
        wub[...] = wu_ref[0].astype(BF16)
        wdb[...] = wd_ref[0].astype(BF16)

        def one_item(t, carry):
            slot = lax.rem(t, 2)

            @pl.when(t + 1 < n_items)
            def _():
                for_item(first, n_big, t + 1, 1 - slot, False, start)

            for_item(first, n_big, t, slot, False, wait)

            @pl.when(t < n_big)
            def _():
                compute(slot, BIG_ROWS)

            @pl.when(t >= n_big)
            def _():
                compute(slot, EXPERT_ROWS)

            @pl.when(t >= 1)
            def _():
                for_item(first, n_big, t - 1, 1 - slot, True, wait)

            for_item(first, n_big, t, slot, True, start)
            return carry

        lax.fori_loop(0, n_items, one_item, 0)
        for_item(first, n_big, n_items - 1, lax.rem(n_items - 1, 2), True, wait)

    nxt = jnp.minimum(e + 1, n_exp - 1)
    nxt_first, nxt_big, nxt_items = items_of(nxt)

    @pl.when((e + 1 < n_exp) & (nxt_items > 0))
    def _():
        for_item(nxt_first, nxt_big, 0, 0, False, start)

    @pl.when(e == pl.num_programs(0) - 1)
    def _():
        ybuf[0, 0:EXPERT_ROWS, :] = jnp.zeros((EXPERT_ROWS, D_MODEL), BF16)
        n_blocks = ys_hbm.shape[0] // EXPERT_ROWS

        def tail_copy(blk):
            return pltpu.make_async_copy(ybuf.at[0, pl.ds(0, EXPERT_ROWS)], _block_rows(ys_hbm, blk), osem.at[0])

        def start_block(blk, carry):
            tail_copy(blk).start()
            return carry

        def wait_block(blk, carry):
            tail_copy(blk).wait()
            return carry

        lax.fori_loop(nused_ref[0], n_blocks, start_block, 0)
        lax.fori_loop(nused_ref[0], n_blocks, wait_block, 0)


def _experts_call(first_block, n_expert_blocks, n_used, xs, w_gate, w_up, w_down):
    grid_spec = pltpu.PrefetchScalarGridSpec(
        num_scalar_prefetch=3,
        grid=(N_EXPERTS,),
        in_specs=[
            pl.BlockSpec(memory_space=pl.ANY),
            pl.BlockSpec((1, D_MODEL, D_EXPERT), lambda e, *_: (e, 0, 0)),
            pl.BlockSpec((1, D_MODEL, D_EXPERT), lambda e, *_: (e, 0, 0)),
            pl.BlockSpec((1, D_EXPERT, D_MODEL), lambda e, *_: (e, 0, 0)),
        ],
        out_specs=pl.BlockSpec(memory_space=pl.ANY),
        scratch_shapes=[
            pltpu.VMEM((2, BIG_ROWS, ROW_W), BF16),
            pltpu.VMEM((2, BIG_ROWS, D_MODEL), BF16),
            pltpu.VMEM((D_MODEL, D_EXPERT), BF16),
            pltpu.VMEM((D_MODEL, D_EXPERT), BF16),
            pltpu.VMEM((D_EXPERT, D_MODEL), BF16),
            pltpu.SemaphoreType.DMA((2,)),
            pltpu.SemaphoreType.DMA((2,)),
        ],
    )
    return pl.pallas_call(
        _experts_body,
        grid_spec=grid_spec,
        out_shape=jax.ShapeDtypeStruct((xs.shape[0], D_MODEL), BF16),
        compiler_params=pltpu.CompilerParams(
            dimension_semantics=("arbitrary",), vmem_limit_bytes=VMEM_LIMIT),
        name="experts",
    )(first_block, n_expert_blocks, n_used, xs, w_gate, w_up, w_down)


def _combine_body(n_prompt_steps, lrow_ref, grow_ref, nct_ref,
                  x1p_ref, x1s_ref, h2p_ref, h2s_ref, gate_ref, gains_ref, slot_ref, wts_ref, ys_hbm,
                  wsg_ref, wsu_ref, wsd_ref, outp_ref, outs_ref, gath, routed, sem):
    i = pl.program_id(0)
    n_steps = pl.num_programs(0)
    par = lax.rem(i, 2)

    def fetch(step, slot):
        buf = gath.at[slot]
        _for_each_run_chunk(step, lrow_ref, grow_ref, nct_ref,
                            lambda lrow, grow: _run_copy(ys_hbm, buf, grow, lrow, sem.at[slot]).start())

    @pl.when(i == 0)
    def _():
        gath[...] = jnp.zeros_like(gath)
        fetch(0, 0)

    @pl.when(i + 1 < n_steps)
    def _():
        fetch(i + 1, 1 - par)

    is_prompt = i < n_prompt_steps
    h2 = jnp.where(is_prompt, h2p_ref[...], h2s_ref[...])
    g = jnp.dot(h2, wsg_ref[...], preferred_element_type=F32)
    u = jnp.dot(h2, wsu_ref[...], preferred_element_type=F32)
    a = (g * _sigmoid(g) * u).astype(BF16)
    ffn = jnp.dot(a, wsd_ref[...], preferred_element_type=F32)

    slot_cols = _to_sublanes(slot_ref[0].astype(F32))
    slot16 = slot_cols.astype(I32).astype(jnp.int16)
    w16 = _to_sublanes(wts_ref[0]).astype(BF16)
    buf = gath.at[par]

    def one(c, carry):
        _run_copy(ys_hbm, buf, 0, 0, sem.at[par]).wait()
        return carry

    lax.fori_loop(0, nct_ref[i], one, 0)

    def unsort(n_rows):
        col_id = lax.broadcasted_iota(I32, (MOE_TILE, n_rows), 1).astype(jnp.int16)
        take = jnp.zeros((MOE_TILE, n_rows), BF16)
        for kk in range(TOP_K):
            take = jnp.where(col_id == slot16[:, kk:kk + 1], w16[:, kk:kk + 1], take)
        routed[...] = jnp.dot(take, buf[0:n_rows, :], preferred_element_type=F32)

    _for_row_option(nct_ref[i] * RUN_ROWS, unsort)
    ffn = ffn + routed[...]

    quarter = MOE_TILE // gate_ref.shape[0]
    gate = jnp.concatenate(
        [jnp.broadcast_to(gate_ref[q:q + 1, :], (quarter, D_MODEL)) for q in range(gate_ref.shape[0])], axis=0)
    x1 = jnp.where(is_prompt, x1p_ref[...], x1s_ref[...])
    out = x1 + gate * (_rms(ffn) * gains_ref[3:4, :])

    @pl.when(is_prompt)
    def _():
        outp_ref[...] = out

    @pl.when(jnp.logical_not(is_prompt))
    def _():
        outs_ref[...] = out


def _combine_call(tables, x1_p, x1_s, h2_p, h2_s, gate_groups, gains, slot_tiles, wts_tiles, ys,
                  ws_gate, ws_up, ws_down):
    n_prompt, n_sample = x1_p.shape[0], x1_s.shape[0]
    n_prompt_steps = n_prompt // MOE_TILE
    n_steps = n_prompt_steps + n_sample // MOE_TILE
    groups_per_tile = gate_groups.shape[0] // n_steps
    tok_p = lambda: pl.BlockSpec((MOE_TILE, D_MODEL), lambda i, *_: (jnp.minimum(i, n_prompt_steps - 1), 0))
    tok_s = lambda: pl.BlockSpec((MOE_TILE, D_MODEL), lambda i, *_: (jnp.maximum(i - n_prompt_steps, 0), 0))
    whole = lambda shape: pl.BlockSpec(shape, lambda i, *_: (0,) * len(shape))
    grid_spec = pltpu.PrefetchScalarGridSpec(
        num_scalar_prefetch=3,
        grid=(n_steps,),
        in_specs=[
            tok_p(), tok_s(), tok_p(), tok_s(),
            pl.BlockSpec((groups_per_tile, D_MODEL), lambda i, *_: (i, 0)),
            whole((4, D_MODEL)),
            pl.BlockSpec((1, ROUTE_ROWS, MOE_TILE), lambda i, *_: (i, 0, 0)),
            pl.BlockSpec((1, ROUTE_ROWS, MOE_TILE), lambda i, *_: (i, 0, 0)),
            pl.BlockSpec(memory_space=pl.ANY),
            whole((D_MODEL, D_SHARED)), whole((D_MODEL, D_SHARED)), whole((D_SHARED, D_MODEL)),
        ],
        out_specs=[tok_p(), tok_s()],
        scratch_shapes=[
            pltpu.VMEM((2, LOCAL_ROWS, D_MODEL), BF16),
            pltpu.VMEM((MOE_TILE, D_MODEL), F32),
            pltpu.SemaphoreType.DMA((2,)),
        ],
    )
    return pl.pallas_call(
        functools.partial(_combine_body, n_prompt_steps),
        grid_spec=grid_spec,
        out_shape=[jax.ShapeDtypeStruct((n_prompt, D_MODEL), F32),
                   jax.ShapeDtypeStruct((n_sample, D_MODEL), F32)],
        compiler_params=pltpu.CompilerParams(
            dimension_semantics=("arbitrary",), vmem_limit_bytes=VMEM_LIMIT),
        name="combine",
    )(*tables, x1_p, x1_s, h2_p, h2_s, gate_groups, gains, slot_tiles, wts_tiles, ys, ws_gate, ws_up, ws_down)


def kernel(x_prompt, x_sample, c_prompt, c_sample, state_pool, cache_k, cache_v, w_ada, b_ada, norm_gains,
           w_in, w_pool, pool_scale, attn_sinks, w_out, w_router, router_bias, w_gate, w_up, w_down,
           ws_gate, ws_up, ws_down):
    assert w_ada.shape[0] == 1, "single-layer kernel"
    bsz, seq, _ = x_prompt.shape
    dbsz, dseq, _ = x_sample.shape
    n_prompt, n_sample = bsz * seq, dbsz * dseq
    n_tok = n_prompt + n_sample
    assert dseq == CHUNK and seq % MOE_TILE == 0 and n_sample % MOE_TILE == 0

    w_in0, w_out0 = w_in[0], w_out[0]
    wq = w_in0[:, POOL_WIDTH:POOL_WIDTH + ATTN_WIDTH].reshape(D_MODEL, N_KV_HEADS, GQ, HEAD_DIM)
    wq = jnp.transpose(wq, (0, 2, 1, 3)).reshape(D_MODEL, ATTN_WIDTH)
    w_in_p = jnp.concatenate(
        [w_in0[:, :POOL_WIDTH], wq, w_in0[:, POOL_WIDTH + ATTN_WIDTH:]], axis=1).astype(BF16)
    wo = w_out0[POOL_WIDTH:].reshape(N_KV_HEADS, GQ, HEAD_DIM, D_MODEL)
    wo = jnp.transpose(wo, (1, 0, 2, 3)).reshape(ATTN_WIDTH, D_MODEL)
    w_out_p = jnp.concatenate([w_out0[:POOL_WIDTH], wo], axis=0).astype(BF16)
    w_pool_b = w_pool[0].astype(BF16)
    pscale = pool_scale[0].reshape(1, POOL_WIDTH)
    wr_t = w_router[0].T
    wr_hi = wr_t.astype(BF16)
    wr_lo = (wr_t - wr_hi.astype(F32)).astype(BF16)
    wr_split = jnp.concatenate([wr_hi, wr_lo], axis=0)
    rbias = router_bias[0].reshape(N_EXPERTS, 1)
    gains = norm_gains[0]
    sinks = attn_sinks[0]

    mod = _ada_call(jnp.concatenate([c_prompt, c_sample], axis=0), w_ada[0], b_ada[0])
    mod = mod.reshape(bsz + dbsz, 6, D_MODEL)

    zeros_pool = jnp.zeros((bsz, POOL_HIST_PAD, POOL_WIDTH), F32)
    zeros_kv = jnp.zeros((bsz, WINDOW, KV_WIDTH), F32)
    cnt0 = jnp.zeros((N_EXPERTS, LANES), F32)
    (x1_p, h2_p, pool_p, nk_p, nv_p, eidx_p, wts_p, rank_p, tcnt_p) = _front_call(
        x_prompt, mod[:bsz], gains, zeros_pool, zeros_kv, zeros_kv, cnt0, sinks,
        w_in_p, w_pool_b, pscale, w_out_p, wr_split, rbias, tile=MOE_TILE, has_cache=False)
    cnt_prompt = jnp.broadcast_to(jnp.sum(tcnt_p[:, 0, :], axis=0)[:, None], (N_EXPERTS, LANES))

    hist_pad = ATTN_KEYS - CHUNK - WINDOW
    pool_s0 = jnp.pad(state_pool[0], ((0, 0), (POOL_HIST_PAD - state_pool.shape[2], 0), (0, 0)))
    ck = jnp.pad(cache_k[0].reshape(dbsz, WINDOW, KV_WIDTH), ((0, 0), (hist_pad, 0), (0, 0)))
    cv = jnp.pad(cache_v[0].reshape(dbsz, WINDOW, KV_WIDTH), ((0, 0), (hist_pad, 0), (0, 0)))
    (x1_s, h2_s, pool_s, nk_s, nv_s, eidx_s, wts_s, rank_s, tcnt_s) = _front_call(
        x_sample, mod[bsz:], gains, pool_s0, ck, cv, cnt_prompt, sinks,
        w_in_p, w_pool_b, pscale, w_out_p, wr_split, rbias, tile=CHUNK, has_cache=True)

    def moe_tiles(route_p, route_s):
        per = MOE_TILE // CHUNK
        regrouped = jnp.transpose(route_s.reshape(-1, per, ROUTE_ROWS, CHUNK), (0, 2, 1, 3))
        return jnp.concatenate([route_p, regrouped.reshape(-1, ROUTE_ROWS, MOE_TILE)], axis=0)

    n_steps = n_tok // MOE_TILE
    cnt_td = jnp.concatenate(
        [tcnt_p[:, 0, :], tcnt_s[:, 0, :].reshape(-1, MOE_TILE // CHUNK, N_EXPERTS).sum(axis=1)],
        axis=0).astype(I32)
    c8 = (cnt_td + RUN_ROWS - 1) // RUN_ROWS * RUN_ROWS
    rows8 = jnp.sum(c8, axis=0)
    padded = (rows8 + EXPERT_ROWS - 1) // EXPERT_ROWS * EXPERT_ROWS
    pends = jnp.cumsum(padded)
    pstarts = pends - padded
    base = jnp.cumsum(cnt_td, axis=0) - cnt_td
    grow = pstarts[None, :] + jnp.cumsum(c8, axis=0) - c8
    lo8 = jnp.cumsum(c8, axis=1) - c8
    nch = c8 // RUN_ROWS
    nct = jnp.sum(nch, axis=1)
    off_tiles = (lo8 - base).astype(F32).reshape(n_steps, N_EXPERTS, 1)
    cum = jnp.cumsum(nch, axis=1)
    j = jnp.arange(MAX_CHUNKS, dtype=I32)
    owner = jnp.sum(j[None, :, None] >= cum[:, None, :], axis=-1)
    owns = owner[:, :, None] == jnp.arange(N_EXPERTS, dtype=I32)[None, None, :]
    of_owner = lambda tab: jnp.sum(jnp.where(owns, tab[:, None, :], 0), axis=-1)
    within = RUN_ROWS * (j[None, :] - of_owner(cum - nch))
    tables = ((of_owner(lo8) + within).reshape(-1).astype(I32),
              (of_owner(grow) + within).reshape(-1).astype(I32), nct.astype(I32))
    n_blocks = -(-(n_tok * TOP_K + n_steps * N_EXPERTS * (RUN_ROWS - 1) + N_EXPERTS * (EXPERT_ROWS - 1))
                 // EXPERT_ROWS)
    n_used = (pends[-1] // EXPERT_ROWS).astype(I32).reshape(1)
    zero_tables = ((pstarts + rows8).astype(I32), ((padded - rows8) // RUN_ROWS).astype(I32), n_used)

    h2_pf, h2_sf = h2_p.reshape(n_prompt, D_MODEL), h2_s.reshape(n_sample, D_MODEL)
    xs, slot_tiles = _dispatch_call(
        tables, zero_tables, h2_pf, h2_sf, moe_tiles(eidx_p, eidx_s), moe_tiles(rank_p, rank_s), off_tiles,
        n_blocks * EXPERT_ROWS)
    ys = _experts_call((pstarts // EXPERT_ROWS).astype(I32), (padded // EXPERT_ROWS).astype(I32), n_used,
                       xs, w_gate[0], w_up[0], w_down[0])
    gate_groups = jnp.concatenate(
        [jnp.repeat(mod[:bsz, 5], seq // GATE_GROUP, axis=0),
         jnp.repeat(mod[bsz:, 5], dseq // GATE_GROUP, axis=0)], axis=0)
    y_p, y_s = _combine_call(
        tables, x1_p.reshape(n_prompt, D_MODEL), x1_s.reshape(n_sample, D_MODEL), h2_pf, h2_sf,
        gate_groups, gains, slot_tiles, moe_tiles(wts_p, wts_s), ys,
        ws_gate[0].astype(BF16), ws_up[0].astype(BF16), ws_down[0].astype(BF16))

    n_hist = state_pool.shape[2]
    kv_shape = (1, -1, WINDOW, N_KV_HEADS, HEAD_DIM)
    return (y_p.reshape(bsz, seq, D_MODEL), y_s.reshape(dbsz, dseq, D_MODEL),
            pool_p[None, :, POOL_HIST_PAD - n_hist:], nk_p.reshape(kv_shape), nv_p.reshape(kv_shape),
            pool_s[None, :, POOL_HIST_PAD - n_hist:], nk_s.reshape(kv_shape), nv_s.reshape(kv_shape))
```

```python
import functools

import jax
import jax.numpy as jnp
import numpy as np
from jax import lax
from jax.experimental import pallas as pl
from jax.experimental.pallas import tpu as pltpu

F32 = jnp.float32
BF16 = jnp.bfloat16
I32 = jnp.int32

D_MODEL = 1024
CHUNK = 64
POOL_WIDTH = 512
POOL_WINDOWS = (2, 4, 8, 16)
POOL_GROUP_W = 128
POOL_HIST_PAD = 16
HEAD_DIM = 64
N_HEADS = 8
N_KV_HEADS = 2
GQ = N_HEADS // N_KV_HEADS
ATTN_WIDTH = N_HEADS * HEAD_DIM
KV_WIDTH = N_KV_HEADS * HEAD_DIM
IN_WIDTH = POOL_WIDTH + ATTN_WIDTH + 2 * KV_WIDTH
WINDOW = 128
N_EXPERTS = 64
TOP_K = 6
N_EXPERT_GROUPS = 8
GROUP_SIZE = N_EXPERTS // N_EXPERT_GROUPS
TOPK_GROUPS = 4
D_EXPERT = 256
D_SHARED = 256
ROUTED_SCALE = 2.5
NORM_EPS = 1e-6
NEG_BIG = -1e30

LANES = 128
SUBLANES = 8
ATTN_KEYS = 2 * LANES
ROUTE_ROWS = 8
EXPERT_ROWS = 256
BIG_BLOCKS = 4
BIG_ROWS = BIG_BLOCKS * EXPERT_ROWS
MOE_TILE = 256
GATE_GROUP = MOE_TILE // SUBLANES
RUN_ROWS = 2 * SUBLANES
LOCAL_ROWS = 2560
LOCAL_ROW_OPTIONS = (2048, 2176, LOCAL_ROWS)
MAX_DOUBLE_CHUNKS = LOCAL_ROWS // (2 * RUN_ROWS)
ROW_W = D_MODEL
VMEM_LIMIT = 56 * 1024 * 1024

assert TOP_K * MOE_TILE + N_EXPERTS * (RUN_ROWS - 1) <= LOCAL_ROWS

_NT = (((1,), (1,)), ((), ()))


def _rms(v):
    return v * lax.rsqrt(jnp.mean(v * v, axis=-1, keepdims=True) + NORM_EPS)


def _sigmoid(v):
    return 1.0 / (1.0 + jnp.exp(-v))


def _split3(v):
    hi = v.astype(BF16)
    r1 = v - hi.astype(F32)
    mid = r1.astype(BF16)
    lo = (r1 - mid.astype(F32)).astype(BF16)
    return hi, mid, lo


def _to_sublanes(rows):
    n = rows.shape[1]
    hi, mid, lo = _split3(rows)
    eye = (lax.broadcasted_iota(I32, (n, n), 0) == lax.broadcasted_iota(I32, (n, n), 1)).astype(BF16)
    return (lax.dot_general(eye, hi, _NT, preferred_element_type=F32)
            + lax.dot_general(eye, mid, _NT, preferred_element_type=F32)
            + lax.dot_general(eye, lo, _NT, preferred_element_type=F32))


def _ada_body(c_ref, w_ref, b_ref, o_ref):
    c = c_ref[...]
    s = c * _sigmoid(c)
    o_ref[...] = jnp.dot(s, w_ref[...], preferred_element_type=F32) + b_ref[...]


def _ada_call(c_all, w_ada, b_ada):
    nb = c_all.shape[0]
    n_out = w_ada.shape[1]
    tile = D_MODEL
    return pl.pallas_call(
        _ada_body,
        grid=(n_out // tile,),
        in_specs=[
            pl.BlockSpec((nb, D_MODEL), lambda j: (0, 0)),
            pl.BlockSpec((D_MODEL, tile), lambda j: (0, j)),
            pl.BlockSpec((1, tile), lambda j: (0, j)),
        ],
        out_specs=pl.BlockSpec((nb, tile), lambda j: (0, j)),
        out_shape=jax.ShapeDtypeStruct((nb, n_out), F32),
        compiler_params=pltpu.CompilerParams(dimension_semantics=("arbitrary",)),
        name="ada",
    )(c_all, w_ada, b_ada.reshape(1, n_out))


def _front_body(tile, has_cache,
                sinks_ref, x_ref, mod_ref, gains_ref, hp_ref, hk_ref, hv_ref, cnt0_ref,
                win_ref, wpool_ref, pscale_ref, wout_ref, wrt_ref, rbias_ref,
                x1_ref, h2_ref, npool_ref, nk_ref, nv_ref, eidx_ref, wts_ref, rank_ref, tcnt_ref,
                ubuf, khist, vhist, run):
    b = pl.program_id(0)
    i = pl.program_id(1)
    sub_q = min(tile, LANES)
    n_sub = tile // sub_q
    hist_keys = ATTN_KEYS - sub_q
    n_keys = hist_keys + tile

    @pl.when(i == 0)
    def _():
        ubuf[0:POOL_HIST_PAD, :] = hp_ref[0]
        khist[...] = hk_ref[0]
        vhist[...] = hv_ref[0]

    @pl.when((b == 0) & (i == 0))
    def _():
        run[...] = cnt0_ref[...]

    mod = mod_ref[0]
    gains = gains_ref[...]
    xt = x_ref[0]

    h = _rms(xt) * gains[0:1] * (1.0 + mod[1:2]) + mod[0:1]
    proj = jnp.dot(h.astype(BF16), win_ref[...], preferred_element_type=F32)
    u = proj[:, :POOL_WIDTH]
    o_k = POOL_WIDTH + ATTN_WIDTH
    k_new = proj[:, o_k:o_k + KV_WIDTH]
    v_new = proj[:, o_k + KV_WIDTH:]

    kw = jnp.concatenate([khist[...], k_new], axis=0)
    vw = jnp.concatenate([vhist[...], v_new], axis=0)
    kwb = kw.astype(BF16)
    vwb = vw.astype(BF16)
    qrow = lax.broadcasted_iota(I32, (sub_q, ATTN_KEYS), 0)
    kpos = lax.broadcasted_iota(I32, (sub_q, ATTN_KEYS), 1) - hist_keys
    qchunk = jnp.right_shift(qrow, 6)
    vis_band = (kpos >= CHUNK * (qchunk - 2)) & (kpos < CHUNK * (qchunk + 1))
    side0 = lax.broadcasted_iota(I32, (sub_q, LANES), 1) < HEAD_DIM
    attn_rows = []
    for r in range(n_sub):
        vis = vis_band
        if not has_cache:
            vis = vis & (kpos + (i * tile + r * sub_q) >= 0)
        kb = kwb[r * sub_q:r * sub_q + ATTN_KEYS]
        vb = vwb[r * sub_q:r * sub_q + ATTN_KEYS]
        blocks = []
        for j in range(GQ):
            qp = proj[r * sub_q:(r + 1) * sub_q, POOL_WIDTH + LANES * j:POOL_WIDTH + LANES * (j + 1)]
            qp = qp * (HEAD_DIM ** -0.5)
            outs = []
            for s in range(N_KV_HEADS):
                keep = side0 if s == 0 else jnp.logical_not(side0)
                qm = jnp.where(keep, qp, 0.0).astype(BF16)
                sc = lax.dot_general(qm, kb, _NT, preferred_element_type=F32)
                sc = jnp.where(vis, sc, NEG_BIG)
                sink = sinks_ref[j + GQ * s]
                m = jnp.maximum(jnp.max(sc, axis=-1, keepdims=True), sink)
                p = jnp.exp(sc - m)
                den = jnp.sum(p, axis=-1, keepdims=True) + jnp.exp(sink - m)
                p = (p / den).astype(BF16)
                outs.append(jnp.dot(p, vb, preferred_element_type=F32))
            blocks.append(jnp.where(side0, outs[0], outs[1]))
        attn_rows.append(jnp.concatenate(blocks, axis=-1))
    attn = attn_rows[0] if n_sub == 1 else jnp.concatenate(attn_rows, axis=0)

    ubuf[POOL_HIST_PAD:POOL_HIST_PAD + tile, :] = u
    if has_cache:
        seen = None
    else:
        seen = (lax.broadcasted_iota(I32, (tile, 1), 0) + i * tile + 1).astype(F32)
    pool_blocks = []
    for g, w in enumerate(POOL_WINDOWS):
        cols = slice(POOL_GROUP_W * g, POOL_GROUP_W * (g + 1))
        acc = u[:, cols]
        for s in range(1, w):
            acc = acc + ubuf[POOL_HIST_PAD - s:POOL_HIST_PAD - s + tile, cols]
        cnt = float(w) if seen is None else jnp.minimum(seen, float(w))
        dlt = acc / cnt - u[:, cols]
        pool_blocks.append(jnp.dot(dlt.astype(BF16), wpool_ref[g], preferred_element_type=F32))
    pool = jnp.concatenate(pool_blocks, axis=-1) * pscale_ref[...]

    mixin = jnp.concatenate([pool, attn], axis=-1).astype(BF16)
    mix = jnp.dot(mixin, wout_ref[...], preferred_element_type=F32)
    x1 = xt + mod[2:3] * (_rms(mix) * gains[1:2])
    x1_ref[0] = x1

    new_hist = ubuf[tile:tile + POOL_HIST_PAD, :]
    ubuf[0:POOL_HIST_PAD, :] = new_hist
    npool_ref[0] = new_hist
    khist[...] = kw[tile:, :]
    vhist[...] = vw[tile:, :]
    nk_ref[0] = kw[n_keys - WINDOW:, :]
    nv_ref[0] = vw[n_keys - WINDOW:, :]

    h2f = _rms(x1) * gains[2:3] * (1.0 + mod[4:5]) + mod[3:4]
    h2hi = h2f.astype(BF16)
    h2_ref[0] = h2hi
    h2lo = (h2f - h2hi.astype(F32)).astype(BF16)
    wrt = wrt_ref[...]
    part = lax.dot_general(wrt, h2hi, _NT, preferred_element_type=F32)
    logits = (part[:N_EXPERTS] + part[N_EXPERTS:]
              + lax.dot_general(wrt[:N_EXPERTS], h2lo, _NT, preferred_element_type=F32))
    scores = _sigmoid(logits)
    sel = scores + rbias_ref[...]

    sub_g = lax.broadcasted_iota(I32, (GROUP_SIZE, tile), 0).astype(F32)
    gscore = jnp.zeros((N_EXPERT_GROUPS, tile), F32)
    for gi in range(N_EXPERT_GROUPS):
        blk = sel[GROUP_SIZE * gi:GROUP_SIZE * (gi + 1), :]
        m1 = jnp.max(blk, axis=0, keepdims=True)
        i1 = jnp.min(jnp.where(blk == m1, sub_g, float(GROUP_SIZE)), axis=0, keepdims=True)
        m2 = jnp.max(jnp.where(sub_g == i1, -jnp.inf, blk), axis=0, keepdims=True)
        gscore = jnp.where(sub_g == gi, m1 + m2, gscore)
    chosen = jnp.zeros((N_EXPERT_GROUPS, tile), F32)
    for _ in range(TOPK_GROUPS):
        m = jnp.max(gscore, axis=0, keepdims=True)
        idx = jnp.min(jnp.where(gscore == m, sub_g, float(N_EXPERT_GROUPS)), axis=0, keepdims=True)
        pick = sub_g == idx
        chosen = jnp.where(pick, 1.0, chosen)
        gscore = jnp.where(pick, -jnp.inf, gscore)
    emask = jnp.concatenate(
        [jnp.broadcast_to(chosen[gi:gi + 1, :], (GROUP_SIZE, tile)) for gi in range(N_EXPERT_GROUPS)], axis=0)
    selm = jnp.where(emask > 0.0, sel, -jnp.inf)

    sub_e = lax.broadcasted_iota(I32, (N_EXPERTS, tile), 0).astype(F32)
    picks, idxs, raw_w = [], [], []
    for _ in range(TOP_K):
        m = jnp.max(selm, axis=0, keepdims=True)
        idx = jnp.min(jnp.where(selm == m, sub_e, float(N_EXPERTS)), axis=0, keepdims=True)
        pick = sub_e == idx
        raw_w.append(jnp.sum(jnp.where(pick, scores, 0.0), axis=0, keepdims=True))
        selm = jnp.where(pick, -jnp.inf, selm)
        picks.append(pick)
        idxs.append(idx)
    wsum = raw_w[0]
    for kk in range(1, TOP_K):
        wsum = wsum + raw_w[kk]

    onehot = jnp.zeros((N_EXPERTS, tile), F32)
    for kk in range(TOP_K):
        onehot = jnp.where(picks[kk], 1.0, onehot)
    onehot_b = onehot.astype(BF16)
    tri = (lax.broadcasted_iota(I32, (tile, tile), 0) < lax.broadcasted_iota(I32, (tile, tile), 1)).astype(BF16)
    before = jnp.dot(onehot_b, tri, preferred_element_type=F32) + run[:, 0:1]
    sub_r = lax.broadcasted_iota(I32, (ROUTE_ROWS, tile), 0)
    eidx_o = jnp.zeros((ROUTE_ROWS, tile), I32)
    wts_o = jnp.zeros((ROUTE_ROWS, tile), F32)
    rank_o = jnp.zeros((ROUTE_ROWS, tile), I32)
    for kk in range(TOP_K):
        rk = jnp.sum(jnp.where(picks[kk], before, 0.0), axis=0, keepdims=True).astype(I32)
        eidx_o = jnp.where(sub_r == kk, idxs[kk].astype(I32), eidx_o)
        wts_o = jnp.where(sub_r == kk, raw_w[kk] / wsum * ROUTED_SCALE, wts_o)
        rank_o = jnp.where(sub_r == kk, rk, rank_o)
    eidx_ref[0] = eidx_o
    wts_ref[0] = wts_o
    rank_ref[0] = rank_o
    tcnt_ref[0] = lax.dot_general(jnp.ones((ROUTE_ROWS, tile), BF16), onehot_b, _NT,
                                  preferred_element_type=F32)
    run[...] = run[...] + jnp.sum(onehot, axis=1, keepdims=True)


def _front_call(x, mod, gains, hist_pool, hist_k, hist_v, cnt0, sinks,
                w_in, w_pool, pool_scale, w_out, wr_t, rbias, *, tile, has_cache):
    bsz, seq, _ = x.shape
    n_tiles = seq // tile
    hist_keys = hist_k.shape[1]
    assert hist_keys == ATTN_KEYS - min(tile, LANES)
    body = functools.partial(_front_body, tile, has_cache)
    whole = lambda shape: pl.BlockSpec(shape, lambda b, i: (0,) * len(shape))
    per_b = lambda shape: pl.BlockSpec((1,) + shape, lambda b, i: (b,) + (0,) * len(shape))
    route = pl.BlockSpec((1, ROUTE_ROWS, tile), lambda b, i: (b * n_tiles + i, 0, 0))
    out_shape = [
        jax.ShapeDtypeStruct((bsz, seq, D_MODEL), F32),
        jax.ShapeDtypeStruct((bsz, seq, D_MODEL), BF16),
        jax.ShapeDtypeStruct((bsz, POOL_HIST_PAD, POOL_WIDTH), F32),
        jax.ShapeDtypeStruct((bsz, WINDOW, KV_WIDTH), F32),
        jax.ShapeDtypeStruct((bsz, WINDOW, KV_WIDTH), F32),
        jax.ShapeDtypeStruct((bsz * n_tiles, ROUTE_ROWS, tile), I32),
        jax.ShapeDtypeStruct((bsz * n_tiles, ROUTE_ROWS, tile), F32),
        jax.ShapeDtypeStruct((bsz * n_tiles, ROUTE_ROWS, tile), I32),
        jax.ShapeDtypeStruct((bsz * n_tiles, ROUTE_ROWS, N_EXPERTS), F32),
    ]
    return pl.pallas_call(
        body,
        grid=(bsz, n_tiles),
        in_specs=[
            pl.BlockSpec(memory_space=pltpu.SMEM),
            pl.BlockSpec((1, tile, D_MODEL), lambda b, i: (b, i, 0)),
            per_b((6, D_MODEL)),
            whole((4, D_MODEL)),
            per_b((POOL_HIST_PAD, POOL_WIDTH)),
            per_b((hist_keys, KV_WIDTH)),
            per_b((hist_keys, KV_WIDTH)),
            whole((N_EXPERTS, LANES)),
            whole((D_MODEL, IN_WIDTH)),
            whole((len(POOL_WINDOWS), POOL_GROUP_W, POOL_GROUP_W)),
            whole((1, POOL_WIDTH)),
            whole((D_MODEL, D_MODEL)),
            whole((2 * N_EXPERTS, D_MODEL)),
            whole((N_EXPERTS, 1)),
        ],
        out_specs=[
            pl.BlockSpec((1, tile, D_MODEL), lambda b, i: (b, i, 0)),
            pl.BlockSpec((1, tile, D_MODEL), lambda b, i: (b, i, 0)),
            per_b((POOL_HIST_PAD, POOL_WIDTH)),
            per_b((WINDOW, KV_WIDTH)),
            per_b((WINDOW, KV_WIDTH)),
            route, route, route,
            pl.BlockSpec((1, ROUTE_ROWS, N_EXPERTS), lambda b, i: (b * n_tiles + i, 0, 0)),
        ],
        out_shape=out_shape,
        scratch_shapes=[
            pltpu.VMEM((tile + POOL_HIST_PAD, POOL_WIDTH), F32),
            pltpu.VMEM((hist_keys, KV_WIDTH), F32),
            pltpu.VMEM((hist_keys, KV_WIDTH), F32),
            pltpu.VMEM((N_EXPERTS, LANES), F32),
        ],
        compiler_params=pltpu.CompilerParams(
            dimension_semantics=("arbitrary", "arbitrary"), vmem_limit_bytes=VMEM_LIMIT),
        name="front_cached" if has_cache else "front_prompt",
    )(sinks, x, mod, gains, hist_pool, hist_k, hist_v, cnt0,
      w_in, w_pool, pool_scale, w_out, wr_t, rbias)


def _run_copy(src, dst, s_row, d_row, sem, rows=RUN_ROWS):
    return pltpu.make_async_copy(src.at[pl.ds(s_row, rows)], dst.at[pl.ds(d_row, rows)], sem)


def _for_each_run_chunk(step, chunk_tables, fn):
    lrow2_ref, grow2_ref, n2_ref, lrow1_ref, grow1_ref, n1_ref = chunk_tables

    def per_double(j, carry):
        idx = step * MAX_DOUBLE_CHUNKS + j
        fn(pl.multiple_of(lrow2_ref[idx], RUN_ROWS), pl.multiple_of(grow2_ref[idx], RUN_ROWS), 2 * RUN_ROWS)
        return carry

    def per_single(j, carry):
        idx = step * N_EXPERTS + j
        fn(pl.multiple_of(lrow1_ref[idx], RUN_ROWS), pl.multiple_of(grow1_ref[idx], RUN_ROWS), RUN_ROWS)
        return carry

    lax.fori_loop(0, n2_ref[step], per_double, 0)
    lax.fori_loop(0, n1_ref[step], per_single, 0)


def _wait_run_chunks(src, dst, sem, n_double, n_single):
    def one_double(c, carry):
        _run_copy(src, dst, 0, 0, sem, 2 * RUN_ROWS).wait()
        return carry

    def one_single(c, carry):
        _run_copy(src, dst, 0, 0, sem).wait()
        return carry

    lax.fori_loop(0, n_double, one_double, 0)
    lax.fori_loop(0, n_single, one_single, 0)


def _for_row_option(used_rows, fn):
    lower = 0
    for n_rows in LOCAL_ROW_OPTIONS:
        @pl.when((used_rows > lower) & (used_rows <= n_rows))
        def _(n_rows=n_rows):
            fn(n_rows)
        lower = n_rows


def _dispatch_body(n_prompt_steps, lrow2_ref, grow2_ref, n2_ref, lrow1_ref, grow1_ref, n1_ref, used_ref,
                   zrow_ref, znch_ref, nused_ref,
                   h2p_ref, h2s_ref, eidx_ref, rank_ref, off_ref, xs_out, slot_out, loc, zrows, sem):
    chunk_tables = (lrow2_ref, grow2_ref, n2_ref, lrow1_ref, grow1_ref, n1_ref)
    i = pl.program_id(0)
    n_steps = pl.num_programs(0)
    par = lax.rem(i, 2)
    h2 = jnp.where(i < n_prompt_steps, h2p_ref[...], h2s_ref[...])
    eidx = eidx_ref[0]
    rank = rank_ref[0]
    off = off_ref[0]
    expert_id = lax.broadcasted_iota(I32, (N_EXPERTS, MOE_TILE), 0)
    sub_r = lax.broadcasted_iota(I32, (ROUTE_ROWS, MOE_TILE), 0)
    slots = []
    slot_o = jnp.zeros((ROUTE_ROWS, MOE_TILE), I32)
    for kk in range(TOP_K):
        mine = jnp.sum(jnp.where(expert_id == eidx[kk:kk + 1, :], off, 0.0), axis=0, keepdims=True)
        slots.append(rank[kk:kk + 1, :] + mine.astype(I32))
        slot_o = jnp.where(sub_r == kk, slots[kk], slot_o)
    slot_out[0] = slot_o
    buf = loc.at[par]

    def sort_tile(n_rows):
        row_id = lax.broadcasted_iota(I32, (n_rows, MOE_TILE), 0).astype(jnp.int16)
        sel = jnp.zeros((n_rows, MOE_TILE), BF16)
        for kk in range(TOP_K):
            sel = jnp.where(row_id == slots[kk].astype(jnp.int16), jnp.ones((), BF16), sel)
        buf[0:n_rows, :] = jnp.dot(sel, h2, preferred_element_type=F32).astype(BF16)

    _for_row_option(used_ref[i], sort_tile)

    @pl.when(i > 0)
    def _():
        _wait_run_chunks(buf, xs_out, sem, n2_ref[i - 1], n1_ref[i - 1])

    _for_each_run_chunk(i, chunk_tables,
                        lambda lrow, grow, rows: _run_copy(buf, xs_out, lrow, grow, sem, rows).start())

    @pl.when(i == n_steps - 1)
    def _():
        zrows[...] = jnp.zeros_like(zrows)

        def per_expert(e, total):
            def per_chunk(c, carry):
                _run_copy(zrows, xs_out, 0, pl.multiple_of(zrow_ref[e] + RUN_ROWS * c, RUN_ROWS), sem).start()
                return carry
            lax.fori_loop(0, znch_ref[e], per_chunk, 0)
            return total + znch_ref[e]

        n_zero = lax.fori_loop(0, N_EXPERTS, per_expert, 0)
        _wait_run_chunks(buf, xs_out, sem, n2_ref[i], n1_ref[i] + n_zero)

        def block_copy(blk):
            return pltpu.make_async_copy(
                zrows, xs_out.at[pl.ds(pl.multiple_of(blk * EXPERT_ROWS, EXPERT_ROWS), EXPERT_ROWS)], sem)

        n_blocks = xs_out.shape[0] // EXPERT_ROWS

        def start_block(blk, carry):
            block_copy(blk).start()
            return carry

        def wait_block(blk, carry):
            block_copy(blk).wait()
            return carry

        lax.fori_loop(nused_ref[0], n_blocks, start_block, 0)
        lax.fori_loop(nused_ref[0], n_blocks, wait_block, 0)


def _dispatch_call(tables, zero_tables, h2_p, h2_s, eidx_tiles, rank_tiles, off_tiles, n_rows):
    n_prompt_steps = h2_p.shape[0] // MOE_TILE
    n_steps = n_prompt_steps + h2_s.shape[0] // MOE_TILE
    route = lambda: pl.BlockSpec((1, ROUTE_ROWS, MOE_TILE), lambda i, *_: (i, 0, 0))
    grid_spec = pltpu.PrefetchScalarGridSpec(
        num_scalar_prefetch=10,
        grid=(n_steps,),
        in_specs=[
            pl.BlockSpec((MOE_TILE, D_MODEL), lambda i, *_: (jnp.minimum(i, n_prompt_steps - 1), 0)),
            pl.BlockSpec((MOE_TILE, D_MODEL), lambda i, *_: (jnp.maximum(i - n_prompt_steps, 0), 0)),
            route(), route(),
            pl.BlockSpec((1, N_EXPERTS, 1), lambda i, *_: (i, 0, 0)),
        ],
        out_specs=[pl.BlockSpec(memory_space=pl.ANY), route()],
        scratch_shapes=[
            pltpu.VMEM((2, LOCAL_ROWS, ROW_W), BF16),
            pltpu.VMEM((EXPERT_ROWS, ROW_W), BF16),
            pltpu.SemaphoreType.DMA,
        ],
    )
    return pl.pallas_call(
        functools.partial(_dispatch_body, n_prompt_steps),
        grid_spec=grid_spec,
        out_shape=[jax.ShapeDtypeStruct((n_rows, ROW_W), BF16),
                   jax.ShapeDtypeStruct((n_steps, ROUTE_ROWS, MOE_TILE), I32)],
        compiler_params=pltpu.CompilerParams(
            dimension_semantics=("arbitrary",), vmem_limit_bytes=VMEM_LIMIT),
        name="dispatch",
    )(*tables, *zero_tables, h2_p, h2_s, eidx_tiles, rank_tiles, off_tiles)


def _block_rows(ref, blk):
    return ref.at[pl.ds(pl.multiple_of(blk * EXPERT_ROWS, EXPERT_ROWS), EXPERT_ROWS)]


def _experts_body(first_ref, nblk_ref, nused_ref, xs_hbm, wg_ref, wu_ref, wd_ref, ys_hbm,
                  xbuf, ybuf, wgb, wub, wdb, isem, osem):
    e = pl.program_id(0)
    n_exp = pl.num_programs(0)

    def items_of(expert):
        n_blk = nblk_ref[expert]
        n_big = n_blk // BIG_BLOCKS
        return first_ref[expert], n_big, n_big + n_blk - n_big * BIG_BLOCKS

    def item_copy(first, n_big, t, slot, rows_static, outward):
        if rows_static == BIG_ROWS:
            row0 = first * EXPERT_ROWS + t * BIG_ROWS
        else:
            row0 = first * EXPERT_ROWS + n_big * BIG_ROWS + (t - n_big) * EXPERT_ROWS
        row0 = pl.multiple_of(row0, EXPERT_ROWS)
        if outward:
            return pltpu.make_async_copy(ybuf.at[slot, pl.ds(0, rows_static)],
                                         ys_hbm.at[pl.ds(row0, rows_static)], osem.at[slot])
        return pltpu.make_async_copy(xs_hbm.at[pl.ds(row0, rows_static)],
                                     xbuf.at[slot, pl.ds(0, rows_static)], isem.at[slot])

    def for_item(first, n_big, t, slot, outward, action):
        @pl.when(t < n_big)
        def _():
            action(item_copy(first, n_big, t, slot, BIG_ROWS, outward))

        @pl.when(t >= n_big)
        def _():
            action(item_copy(first, n_big, t, slot, EXPERT_ROWS, outward))

    start = lambda c: c.start()
    wait = lambda c: c.wait()
    first, n_big, n_items = items_of(e)

    @pl.when((e == 0) & (n_items > 0))
    def _():
        for_item(first, n_big, 0, 0, False, start)

    def compute(slot, rows_static):
        x = xbuf[slot, 0:rows_static, :]
        g = jnp.dot(x, wgb[...], preferred_element_type=F32)
        u = jnp.dot(x, wub[...], preferred_element_type=F32)
        a = (g * _sigmoid(g) * u).astype(BF16)
        ybuf[slot, 0:rows_static, :] = jnp.dot(a, wdb[...], preferred_element_type=F32).astype(BF16)

    @pl.when(n_items > 0)
    def _():
        wgb[...] = wg_ref[0].astype(BF16)
        wub[...] = wu_ref[0].astype(BF16)
        wdb[...] = wd_ref[0].astype(BF16)

        def one_item(t, carry):
            slot = lax.rem(t, 2)

            @pl.when(t + 1 < n_items)
            def _():
                for_item(first, n_big, t + 1, 1 - slot, False, start)

            for_item(first, n_big, t, slot, False, wait)

            @pl.when(t < n_big)
            def _():
                compute(slot, BIG_ROWS)

            @pl.when(t >= n_big)
            def _():
                compute(slot, EXPERT_ROWS)

            @pl.when(t >= 1)
            def _():
                for_item(first, n_big, t - 1, 1 - slot, True, wait)

            for_item(first, n_big, t, slot, True, start)
            return carry

        lax.fori_loop(0, n_items, one_item, 0)
        for_item(first, n_big, n_items - 1, lax.rem(n_items - 1, 2), True, wait)

    nxt = jnp.minimum(e + 1, n_exp - 1)
    nxt_first, nxt_big, nxt_items = items_of(nxt)

    @pl.when((e + 1 < n_exp) & (nxt_items > 0))
    def _():
        for_item(nxt_first, nxt_big, 0, 0, False, start)

    @pl.when(e == pl.num_programs(0) - 1)
    def _():
        ybuf[0, 0:EXPERT_ROWS, :] = jnp.zeros((EXPERT_ROWS, D_MODEL), BF16)
        n_blocks = ys_hbm.shape[0] // EXPERT_ROWS

        def tail_copy(blk):
            return pltpu.make_async_copy(ybuf.at[0, pl.ds(0, EXPERT_ROWS)], _block_rows(ys_hbm, blk), osem.at[0])

        def start_block(blk, carry):
            tail_copy(blk).start()
            return carry

        def wait_block(blk, carry):
            tail_copy(blk).wait()
            return carry

        lax.fori_loop(nused_ref[0], n_blocks, start_block, 0)
        lax.fori_loop(nused_ref[0], n_blocks, wait_block, 0)


def _experts_call(first_block, n_expert_blocks, n_used, xs, w_gate, w_up, w_down):
    grid_spec = pltpu.PrefetchScalarGridSpec(
        num_scalar_prefetch=3,
        grid=(N_EXPERTS,),
        in_specs=[
            pl.BlockSpec(memory_space=pl.ANY),
            pl.BlockSpec((1, D_MODEL, D_EXPERT), lambda e, *_: (e, 0, 0)),
            pl.BlockSpec((1, D_MODEL, D_EXPERT), lambda e, *_: (e, 0, 0)),
            pl.BlockSpec((1, D_EXPERT, D_MODEL), lambda e, *_: (e, 0, 0)),
        ],
        out_specs=pl.BlockSpec(memory_space=pl.ANY),
        scratch_shapes=[
            pltpu.VMEM((2, BIG_ROWS, ROW_W), BF16),
            pltpu.VMEM((2, BIG_ROWS, D_MODEL), BF16),
            pltpu.VMEM((D_MODEL, D_EXPERT), BF16),
            pltpu.VMEM((D_MODEL, D_EXPERT), BF16),
            pltpu.VMEM((D_EXPERT, D_MODEL), BF16),
            pltpu.SemaphoreType.DMA((2,)),
            pltpu.SemaphoreType.DMA((2,)),
        ],
    )
    return pl.pallas_call(
        _experts_body,
        grid_spec=grid_spec,
        out_shape=jax.ShapeDtypeStruct((xs.shape[0], D_MODEL), BF16),
        compiler_params=pltpu.CompilerParams(
            dimension_semantics=("arbitrary",), vmem_limit_bytes=VMEM_LIMIT),
        name="experts",
    )(first_block, n_expert_blocks, n_used, xs, w_gate, w_up, w_down)


def _combine_body(n_prompt_steps, lrow2_ref, grow2_ref, n2_ref, lrow1_ref, grow1_ref, n1_ref, used_ref,
                  x1p_ref, x1s_ref, h2p_ref, h2s_ref, gate_ref, gains_ref, slot_ref, wts_ref, ys_hbm,
                  wsg_ref, wsu_ref, wsd_ref, outp_ref, outs_ref, gath, routed, sem):
    chunk_tables = (lrow2_ref, grow2_ref, n2_ref, lrow1_ref, grow1_ref, n1_ref)
    i = pl.program_id(0)
    n_steps = pl.num_programs(0)
    par = lax.rem(i, 2)

    def fetch(step, slot):
        buf = gath.at[slot]
        _for_each_run_chunk(
            step, chunk_tables,
            lambda lrow, grow, rows: _run_copy(ys_hbm, buf, grow, lrow, sem.at[slot], rows).start())

    @pl.when(i == 0)
    def _():
        gath[...] = jnp.zeros_like(gath)
        fetch(0, 0)

    @pl.when(i + 1 < n_steps)
    def _():
        fetch(i + 1, 1 - par)

    is_prompt = i < n_prompt_steps
    h2 = jnp.where(is_prompt, h2p_ref[...], h2s_ref[...])
    g = jnp.dot(h2, wsg_ref[...], preferred_element_type=F32)
    u = jnp.dot(h2, wsu_ref[...], preferred_element_type=F32)
    a = (g * _sigmoid(g) * u).astype(BF16)
    ffn = jnp.dot(a, wsd_ref[...], preferred_element_type=F32)

    slot_cols = _to_sublanes(slot_ref[0].astype(F32))
    slot16 = slot_cols.astype(I32).astype(jnp.int16)
    w16 = _to_sublanes(wts_ref[0]).astype(BF16)
    buf = gath.at[par]

    _wait_run_chunks(ys_hbm, buf, sem.at[par], n2_ref[i], n1_ref[i])

    def unsort(n_rows):
        col_id = lax.broadcasted_iota(I32, (MOE_TILE, n_rows), 1).astype(jnp.int16)
        take = jnp.zeros((MOE_TILE, n_rows), BF16)
        for kk in range(TOP_K):
            take = jnp.where(col_id == slot16[:, kk:kk + 1], w16[:, kk:kk + 1], take)
        routed[...] = jnp.dot(take, buf[0:n_rows, :], preferred_element_type=F32)

    _for_row_option(used_ref[i], unsort)
    ffn = ffn + routed[...]

    quarter = MOE_TILE // gate_ref.shape[0]
    gate = jnp.concatenate(
        [jnp.broadcast_to(gate_ref[q:q + 1, :], (quarter, D_MODEL)) for q in range(gate_ref.shape[0])], axis=0)
    x1 = jnp.where(is_prompt, x1p_ref[...], x1s_ref[...])
    out = x1 + gate * (_rms(ffn) * gains_ref[3:4, :])

    @pl.when(is_prompt)
    def _():
        outp_ref[...] = out

    @pl.when(jnp.logical_not(is_prompt))
    def _():
        outs_ref[...] = out


def _combine_call(tables, x1_p, x1_s, h2_p, h2_s, gate_groups, gains, slot_tiles, wts_tiles, ys,
                  ws_gate, ws_up, ws_down):
    n_prompt, n_sample = x1_p.shape[0], x1_s.shape[0]
    n_prompt_steps = n_prompt // MOE_TILE
    n_steps = n_prompt_steps + n_sample // MOE_TILE
    groups_per_tile = gate_groups.shape[0] // n_steps
    tok_p = lambda: pl.BlockSpec((MOE_TILE, D_MODEL), lambda i, *_: (jnp.minimum(i, n_prompt_steps - 1), 0))
    tok_s = lambda: pl.BlockSpec((MOE_TILE, D_MODEL), lambda i, *_: (jnp.maximum(i - n_prompt_steps, 0), 0))
    whole = lambda shape: pl.BlockSpec(shape, lambda i, *_: (0,) * len(shape))
    grid_spec = pltpu.PrefetchScalarGridSpec(
        num_scalar_prefetch=7,
        grid=(n_steps,),
        in_specs=[
            tok_p(), tok_s(), tok_p(), tok_s(),
            pl.BlockSpec((groups_per_tile, D_MODEL), lambda i, *_: (i, 0)),
            whole((4, D_MODEL)),
            pl.BlockSpec((1, ROUTE_ROWS, MOE_TILE), lambda i, *_: (i, 0, 0)),
            pl.BlockSpec((1, ROUTE_ROWS, MOE_TILE), lambda i, *_: (i, 0, 0)),
            pl.BlockSpec(memory_space=pl.ANY),
            whole((D_MODEL, D_SHARED)), whole((D_MODEL, D_SHARED)), whole((D_SHARED, D_MODEL)),
        ],
        out_specs=[tok_p(), tok_s()],
        scratch_shapes=[
            pltpu.VMEM((2, LOCAL_ROWS, D_MODEL), BF16),
            pltpu.VMEM((MOE_TILE, D_MODEL), F32),
            pltpu.SemaphoreType.DMA((2,)),
        ],
    )
    return pl.pallas_call(
        functools.partial(_combine_body, n_prompt_steps),
        grid_spec=grid_spec,
        out_shape=[jax.ShapeDtypeStruct((n_prompt, D_MODEL), F32),
                   jax.ShapeDtypeStruct((n_sample, D_MODEL), F32)],
        compiler_params=pltpu.CompilerParams(
            dimension_semantics=("arbitrary",), vmem_limit_bytes=VMEM_LIMIT),
        name="combine",
    )(*tables, x1_p, x1_s, h2_p, h2_s, gate_groups, gains, slot_tiles, wts_tiles, ys, ws_gate, ws_up, ws_down)


def kernel(x_prompt, x_sample, c_prompt, c_sample, state_pool, cache_k, cache_v, w_ada, b_ada, norm_gains,
           w_in, w_pool, pool_scale, attn_sinks, w_out, w_router, router_bias, w_gate, w_up, w_down,
           ws_gate, ws_up, ws_down):
    assert w_ada.shape[0] == 1, "single-layer kernel"
    bsz, seq, _ = x_prompt.shape
    dbsz, dseq, _ = x_sample.shape
    n_prompt, n_sample = bsz * seq, dbsz * dseq
    n_tok = n_prompt + n_sample
    assert dseq == CHUNK and seq % MOE_TILE == 0 and n_sample % MOE_TILE == 0

    w_in0, w_out0 = w_in[0], w_out[0]
    wq = w_in0[:, POOL_WIDTH:POOL_WIDTH + ATTN_WIDTH].reshape(D_MODEL, N_KV_HEADS, GQ, HEAD_DIM)
    wq = jnp.transpose(wq, (0, 2, 1, 3)).reshape(D_MODEL, ATTN_WIDTH)
    w_in_p = jnp.concatenate(
        [w_in0[:, :POOL_WIDTH], wq, w_in0[:, POOL_WIDTH + ATTN_WIDTH:]], axis=1).astype(BF16)
    wo = w_out0[POOL_WIDTH:].reshape(N_KV_HEADS, GQ, HEAD_DIM, D_MODEL)
    wo = jnp.transpose(wo, (1, 0, 2, 3)).reshape(ATTN_WIDTH, D_MODEL)
    w_out_p = jnp.concatenate([w_out0[:POOL_WIDTH], wo], axis=0).astype(BF16)
    w_pool_b = w_pool[0].astype(BF16)
    pscale = pool_scale[0].reshape(1, POOL_WIDTH)
    wr_t = w_router[0].T
    wr_hi = wr_t.astype(BF16)
    wr_lo = (wr_t - wr_hi.astype(F32)).astype(BF16)
    wr_split = jnp.concatenate([wr_hi, wr_lo], axis=0)
    rbias = router_bias[0].reshape(N_EXPERTS, 1)
    gains = norm_gains[0]
    sinks = attn_sinks[0]

    mod = _ada_call(jnp.concatenate([c_prompt, c_sample], axis=0), w_ada[0], b_ada[0])
    mod = mod.reshape(bsz + dbsz, 6, D_MODEL)

    zeros_pool = jnp.zeros((bsz, POOL_HIST_PAD, POOL_WIDTH), F32)
    zeros_kv = jnp.zeros((bsz, WINDOW, KV_WIDTH), F32)
    cnt0 = jnp.zeros((N_EXPERTS, LANES), F32)
    (x1_p, h2_p, pool_p, nk_p, nv_p, eidx_p, wts_p, rank_p, tcnt_p) = _front_call(
        x_prompt, mod[:bsz], gains, zeros_pool, zeros_kv, zeros_kv, cnt0, sinks,
        w_in_p, w_pool_b, pscale, w_out_p, wr_split, rbias, tile=MOE_TILE, has_cache=False)
    cnt_prompt = jnp.broadcast_to(jnp.sum(tcnt_p[:, 0, :], axis=0)[:, None], (N_EXPERTS, LANES))

    hist_pad = ATTN_KEYS - CHUNK - WINDOW
    pool_s0 = jnp.pad(state_pool[0], ((0, 0), (POOL_HIST_PAD - state_pool.shape[2], 0), (0, 0)))
    ck = jnp.pad(cache_k[0].reshape(dbsz, WINDOW, KV_WIDTH), ((0, 0), (hist_pad, 0), (0, 0)))
    cv = jnp.pad(cache_v[0].reshape(dbsz, WINDOW, KV_WIDTH), ((0, 0), (hist_pad, 0), (0, 0)))
    (x1_s, h2_s, pool_s, nk_s, nv_s, eidx_s, wts_s, rank_s, tcnt_s) = _front_call(
        x_sample, mod[bsz:], gains, pool_s0, ck, cv, cnt_prompt, sinks,
        w_in_p, w_pool_b, pscale, w_out_p, wr_split, rbias, tile=CHUNK, has_cache=True)

    def moe_tiles(route_p, route_s):
        per = MOE_TILE // CHUNK
        regrouped = jnp.transpose(route_s.reshape(-1, per, ROUTE_ROWS, CHUNK), (0, 2, 1, 3))
        return jnp.concatenate([route_p, regrouped.reshape(-1, ROUTE_ROWS, MOE_TILE)], axis=0)

    n_steps = n_tok // MOE_TILE
    cnt_td = jnp.concatenate(
        [tcnt_p[:, 0, :], tcnt_s[:, 0, :].reshape(-1, MOE_TILE // CHUNK, N_EXPERTS).sum(axis=1)],
        axis=0).astype(I32)
    c8 = (cnt_td + RUN_ROWS - 1) // RUN_ROWS * RUN_ROWS
    rows8 = jnp.sum(c8, axis=0)
    padded = (rows8 + EXPERT_ROWS - 1) // EXPERT_ROWS * EXPERT_ROWS
    pends = jnp.cumsum(padded)
    pstarts = pends - padded
    base = jnp.cumsum(cnt_td, axis=0) - cnt_td
    grow = pstarts[None, :] + jnp.cumsum(c8, axis=0) - c8
    lo8 = jnp.cumsum(c8, axis=1) - c8
    nch = c8 // RUN_ROWS
    nct = jnp.sum(nch, axis=1)
    off_tiles = (lo8 - base).astype(F32).reshape(n_steps, N_EXPERTS, 1)
    def flat_chunks(counts, max_n, local0, global0, rows):
        cum = jnp.cumsum(counts, axis=1)
        j = jnp.arange(max_n, dtype=I32)
        owner = jnp.sum(j[None, :, None] >= cum[:, None, :], axis=-1)
        owns = owner[:, :, None] == jnp.arange(N_EXPERTS, dtype=I32)[None, None, :]
        of_owner = lambda tab: jnp.sum(jnp.where(owns, tab[:, None, :], 0), axis=-1)
        within = rows * (j[None, :] - of_owner(cum - counts))
        return ((of_owner(local0) + within).reshape(-1).astype(I32),
                (of_owner(global0) + within).reshape(-1).astype(I32), cum[:, -1].astype(I32))

    n_double = nch // 2
    doubled = 2 * RUN_ROWS * n_double
    tables = (flat_chunks(n_double, MAX_DOUBLE_CHUNKS, lo8, grow, 2 * RUN_ROWS)
              + flat_chunks(nch - 2 * n_double, N_EXPERTS, lo8 + doubled, grow + doubled, RUN_ROWS)
              + ((nct * RUN_ROWS).astype(I32),))
    n_blocks = -(-(n_tok * TOP_K + n_steps * N_EXPERTS * (RUN_ROWS - 1) + N_EXPERTS * (EXPERT_ROWS - 1))
                 // EXPERT_ROWS)
    n_used = (pends[-1] // EXPERT_ROWS).astype(I32).reshape(1)
    zero_tables = ((pstarts + rows8).astype(I32), ((padded - rows8) // RUN_ROWS).astype(I32), n_used)

    h2_pf, h2_sf = h2_p.reshape(n_prompt, D_MODEL), h2_s.reshape(n_sample, D_MODEL)
    xs, slot_tiles = _dispatch_call(
        tables, zero_tables, h2_pf, h2_sf, moe_tiles(eidx_p, eidx_s), moe_tiles(rank_p, rank_s), off_tiles,
        n_blocks * EXPERT_ROWS)
    ys = _experts_call((pstarts // EXPERT_ROWS).astype(I32), (padded // EXPERT_ROWS).astype(I32), n_used,
                       xs, w_gate[0], w_up[0], w_down[0])
    gate_groups = jnp.concatenate(
        [jnp.repeat(mod[:bsz, 5], seq // GATE_GROUP, axis=0),
         jnp.repeat(mod[bsz:, 5], dseq // GATE_GROUP, axis=0)], axis=0)
    y_p, y_s = _combine_call(
        tables, x1_p.reshape(n_prompt, D_MODEL), x1_s.reshape(n_sample, D_MODEL), h2_pf, h2_sf,
        gate_groups, gains, slot_tiles, moe_tiles(wts_p, wts_s), ys,
        ws_gate[0].astype(BF16), ws_up[0].astype(BF16), ws_down[0].astype(BF16))

    n_hist = state_pool.shape[2]
    kv_shape = (1, -1, WINDOW, N_KV_HEADS, HEAD_DIM)
    return (y_p.reshape(bsz, seq, D_MODEL), y_s.reshape(dbsz, dseq, D_MODEL),
            pool_p[None, :, POOL_HIST_PAD - n_hist:], nk_p.reshape(kv_shape), nv_p.reshape(kv_shape),
            pool_s[None, :, POOL_HIST_PAD - n_hist:], nk_s.reshape(kv_shape), nv_s.reshape(kv_shape))
```

```python
import functools

import jax
import jax.numpy as jnp
import numpy as np
from jax import lax
from jax.experimental import pallas as pl
from jax.experimental.pallas import tpu as pltpu

F32 = jnp.float32
BF16 = jnp.bfloat16
I32 = jnp.int32

D_MODEL = 1024
CHUNK = 64
POOL_WIDTH = 512
POOL_WINDOWS = (2, 4, 8, 16)
POOL_GROUP_W = 128
POOL_HIST_PAD = 16
HEAD_DIM = 64
N_HEADS = 8
N_KV_HEADS = 2
GQ = N_HEADS // N_KV_HEADS
ATTN_WIDTH = N_HEADS * HEAD_DIM
KV_WIDTH = N_KV_HEADS * HEAD_DIM
IN_WIDTH = POOL_WIDTH + ATTN_WIDTH + 2 * KV_WIDTH
WINDOW = 128
N_EXPERTS = 64
TOP_K = 6
N_EXPERT_GROUPS = 8
GROUP_SIZE = N_EXPERTS // N_EXPERT_GROUPS
TOPK_GROUPS = 4
D_EXPERT = 256
D_SHARED = 256
ROUTED_SCALE = 2.5
NORM_EPS = 1e-6
NEG_BIG = -1e30

LANES = 128
SUBLANES = 8
ATTN_KEYS = 2 * LANES
ROUTE_ROWS = 8
EXPERT_ROWS = 256
BIG_BLOCKS = 4
BIG_ROWS = BIG_BLOCKS * EXPERT_ROWS
MOE_TILE = 256
GATE_GROUP = MOE_TILE // SUBLANES
RUN_ROWS = 2 * SUBLANES
LOCAL_ROWS = 2560
LOCAL_ROW_OPTIONS = (2048, 2176, LOCAL_ROWS)
MAX_DOUBLE_CHUNKS = LOCAL_ROWS // (2 * RUN_ROWS)
ROW_W = D_MODEL
VMEM_LIMIT = 56 * 1024 * 1024

assert TOP_K * MOE_TILE + N_EXPERTS * (RUN_ROWS - 1) <= LOCAL_ROWS

_NT = (((1,), (1,)), ((), ()))


def _rms(v):
    return v * lax.rsqrt(jnp.mean(v * v, axis=-1, keepdims=True) + NORM_EPS)


def _sigmoid(v):
    return 1.0 / (1.0 + jnp.exp(-v))


def _split3(v):
    hi = v.astype(BF16)
    r1 = v - hi.astype(F32)
    mid = r1.astype(BF16)
    lo = (r1 - mid.astype(F32)).astype(BF16)
    return hi, mid, lo


def _to_sublanes(rows):
    n = rows.shape[1]
    hi, mid, lo = _split3(rows)
    eye = (lax.broadcasted_iota(I32, (n, n), 0) == lax.broadcasted_iota(I32, (n, n), 1)).astype(BF16)
    return (lax.dot_general(eye, hi, _NT, preferred_element_type=F32)
            + lax.dot_general(eye, mid, _NT, preferred_element_type=F32)
            + lax.dot_general(eye, lo, _NT, preferred_element_type=F32))


def _ada_body(c_ref, w_ref, b_ref, o_ref):
    c = c_ref[...]
    s = c * _sigmoid(c)
    o_ref[...] = jnp.dot(s, w_ref[...], preferred_element_type=F32) + b_ref[...]


def _ada_call(c_all, w_ada, b_ada):
    nb = c_all.shape[0]
    n_out = w_ada.shape[1]
    tile = D_MODEL
    return pl.pallas_call(
        _ada_body,
        grid=(n_out // tile,),
        in_specs=[
            pl.BlockSpec((nb, D_MODEL), lambda j: (0, 0)),
            pl.BlockSpec((D_MODEL, tile), lambda j: (0, j)),
            pl.BlockSpec((1, tile), lambda j: (0, j)),
        ],
        out_specs=pl.BlockSpec((nb, tile), lambda j: (0, j)),
        out_shape=jax.ShapeDtypeStruct((nb, n_out), F32),
        compiler_params=pltpu.CompilerParams(dimension_semantics=("arbitrary",)),
        name="ada",
    )(c_all, w_ada, b_ada.reshape(1, n_out))


def _front_body(tile, has_cache,
                sinks_ref, x_ref, mod_ref, gains_ref, hp_ref, hk_ref, hv_ref, cnt0_ref,
                win_ref, wpool_ref, pscale_ref, wout_ref, wrt_ref, rbias_ref,
                x1_ref, h2_ref, npool_ref, nk_ref, nv_ref, eidx_ref, wts_ref, rank_ref, tcnt_ref,
                ubuf, khist, vhist, run):
    b = pl.program_id(0)
    i = pl.program_id(1)
    sub_q = min(tile, LANES)
    n_sub = tile // sub_q
    hist_keys = ATTN_KEYS - sub_q
    n_keys = hist_keys + tile

    @pl.when(i == 0)
    def _():
        ubuf[0:POOL_HIST_PAD, :] = hp_ref[0]
        khist[...] = hk_ref[0]
        vhist[...] = hv_ref[0]

    @pl.when((b == 0) & (i == 0))
    def _():
        run[...] = cnt0_ref[...]

    mod = mod_ref[0]
    gains = gains_ref[...]
    xt = x_ref[0]

    h = _rms(xt) * gains[0:1] * (1.0 + mod[1:2]) + mod[0:1]
    proj = jnp.dot(h.astype(BF16), win_ref[...], preferred_element_type=F32)
    u = proj[:, :POOL_WIDTH]
    o_k = POOL_WIDTH + ATTN_WIDTH
    k_new = proj[:, o_k:o_k + KV_WIDTH]
    v_new = proj[:, o_k + KV_WIDTH:]

    kw = jnp.concatenate([khist[...], k_new], axis=0)
    vw = jnp.concatenate([vhist[...], v_new], axis=0)
    kwb = kw.astype(BF16)
    vwb = vw.astype(BF16)
    qrow = lax.broadcasted_iota(I32, (sub_q, ATTN_KEYS), 0)
    kpos = lax.broadcasted_iota(I32, (sub_q, ATTN_KEYS), 1) - hist_keys
    qchunk = jnp.right_shift(qrow, 6)
    vis_band = (kpos >= CHUNK * (qchunk - 2)) & (kpos < CHUNK * (qchunk + 1))
    side0 = lax.broadcasted_iota(I32, (sub_q, LANES), 1) < HEAD_DIM
    attn_rows = []
    for r in range(n_sub):
        vis = vis_band
        if not has_cache:
            vis = vis & (kpos + (i * tile + r * sub_q) >= 0)
        kb = kwb[r * sub_q:r * sub_q + ATTN_KEYS]
        vb = vwb[r * sub_q:r * sub_q + ATTN_KEYS]
        blocks = []
        for j in range(GQ):
            qp = proj[r * sub_q:(r + 1) * sub_q, POOL_WIDTH + LANES * j:POOL_WIDTH + LANES * (j + 1)]
            qp = qp * (HEAD_DIM ** -0.5)
            outs = []
            for s in range(N_KV_HEADS):
                keep = side0 if s == 0 else jnp.logical_not(side0)
                qm = jnp.where(keep, qp, 0.0).astype(BF16)
                sc = lax.dot_general(qm, kb, _NT, preferred_element_type=F32)
                sc = jnp.where(vis, sc, NEG_BIG)
                sink = sinks_ref[j + GQ * s]
                m = jnp.maximum(jnp.max(sc, axis=-1, keepdims=True), sink)
                p = jnp.exp(sc - m)
                den = jnp.sum(p, axis=-1, keepdims=True) + jnp.exp(sink - m)
                p = (p / den).astype(BF16)
                outs.append(jnp.dot(p, vb, preferred_element_type=F32))
            blocks.append(jnp.where(side0, outs[0], outs[1]))
        attn_rows.append(jnp.concatenate(blocks, axis=-1))
    attn = attn_rows[0] if n_sub == 1 else jnp.concatenate(attn_rows, axis=0)

    ubuf[POOL_HIST_PAD:POOL_HIST_PAD + tile, :] = u
    if has_cache:
        seen = None
    else:
        seen = (lax.broadcasted_iota(I32, (tile, 1), 0) + i * tile + 1).astype(F32)
    pool_blocks = []
    for g, w in enumerate(POOL_WINDOWS):
        cols = slice(POOL_GROUP_W * g, POOL_GROUP_W * (g + 1))
        acc = u[:, cols]
        for s in range(1, w):
            acc = acc + ubuf[POOL_HIST_PAD - s:POOL_HIST_PAD - s + tile, cols]
        cnt = float(w) if seen is None else jnp.minimum(seen, float(w))
        dlt = acc / cnt - u[:, cols]
        pool_blocks.append(jnp.dot(dlt.astype(BF16), wpool_ref[g], preferred_element_type=F32))
    pool = jnp.concatenate(pool_blocks, axis=-1) * pscale_ref[...]

    mixin = jnp.concatenate([pool, attn], axis=-1).astype(BF16)
    mix = jnp.dot(mixin, wout_ref[...], preferred_element_type=F32)
    x1 = xt + mod[2:3] * (_rms(mix) * gains[1:2])
    x1_ref[0] = x1

    new_hist = ubuf[tile:tile + POOL_HIST_PAD, :]
    ubuf[0:POOL_HIST_PAD, :] = new_hist
    npool_ref[0] = new_hist
    khist[...] = kw[tile:, :]
    vhist[...] = vw[tile:, :]
    nk_ref[0] = kw[n_keys - WINDOW:, :]
    nv_ref[0] = vw[n_keys - WINDOW:, :]

    h2f = _rms(x1) * gains[2:3] * (1.0 + mod[4:5]) + mod[3:4]
    h2hi = h2f.astype(BF16)
    h2_ref[0] = h2hi
    h2lo = (h2f - h2hi.astype(F32)).astype(BF16)
    wrt = wrt_ref[...]
    part = lax.dot_general(wrt, h2hi, _NT, preferred_element_type=F32)
    logits = (part[:N_EXPERTS] + part[N_EXPERTS:]
              + lax.dot_general(wrt[:N_EXPERTS], h2lo, _NT, preferred_element_type=F32))
    scores = _sigmoid(logits)
    sel = scores + rbias_ref[...]

    sub_g = lax.broadcasted_iota(I32, (GROUP_SIZE, tile), 0).astype(F32)
    gscore = jnp.zeros((N_EXPERT_GROUPS, tile), F32)
    for gi in range(N_EXPERT_GROUPS):
        blk = sel[GROUP_SIZE * gi:GROUP_SIZE * (gi + 1), :]
        m1 = jnp.max(blk, axis=0, keepdims=True)
        i1 = jnp.min(jnp.where(blk == m1, sub_g, float(GROUP_SIZE)), axis=0, keepdims=True)
        m2 = jnp.max(jnp.where(sub_g == i1, -jnp.inf, blk), axis=0, keepdims=True)
        gscore = jnp.where(sub_g == gi, m1 + m2, gscore)
    chosen = jnp.zeros((N_EXPERT_GROUPS, tile), F32)
    for _ in range(TOPK_GROUPS):
        m = jnp.max(gscore, axis=0, keepdims=True)
        idx = jnp.min(jnp.where(gscore == m, sub_g, float(N_EXPERT_GROUPS)), axis=0, keepdims=True)
        pick = sub_g == idx
        chosen = jnp.where(pick, 1.0, chosen)
        gscore = jnp.where(pick, -jnp.inf, gscore)
    emask = jnp.concatenate(
        [jnp.broadcast_to(chosen[gi:gi + 1, :], (GROUP_SIZE, tile)) for gi in range(N_EXPERT_GROUPS)], axis=0)
    selm = jnp.where(emask > 0.0, sel, -jnp.inf)

    sub_e = lax.broadcasted_iota(I32, (N_EXPERTS, tile), 0).astype(F32)
    picks, idxs, raw_w = [], [], []
    for _ in range(TOP_K):
        m = jnp.max(selm, axis=0, keepdims=True)
        idx = jnp.min(jnp.where(selm == m, sub_e, float(N_EXPERTS)), axis=0, keepdims=True)
        pick = sub_e == idx
        raw_w.append(jnp.sum(jnp.where(pick, scores, 0.0), axis=0, keepdims=True))
        selm = jnp.where(pick, -jnp.inf, selm)
        picks.append(pick)
        idxs.append(idx)
    wsum = raw_w[0]
    for kk in range(1, TOP_K):
        wsum = wsum + raw_w[kk]

    onehot = jnp.zeros((N_EXPERTS, tile), F32)
    for kk in range(TOP_K):
        onehot = jnp.where(picks[kk], 1.0, onehot)
    onehot_b = onehot.astype(BF16)
    tri = (lax.broadcasted_iota(I32, (tile, tile), 0) < lax.broadcasted_iota(I32, (tile, tile), 1)).astype(BF16)
    before = jnp.dot(onehot_b, tri, preferred_element_type=F32) + run[:, 0:1]
    sub_r = lax.broadcasted_iota(I32, (ROUTE_ROWS, tile), 0)
    eidx_o = jnp.zeros((ROUTE_ROWS, tile), I32)
    wts_o = jnp.zeros((ROUTE_ROWS, tile), F32)
    rank_o = jnp.zeros((ROUTE_ROWS, tile), I32)
    for kk in range(TOP_K):
        rk = jnp.sum(jnp.where(picks[kk], before, 0.0), axis=0, keepdims=True).astype(I32)
        eidx_o = jnp.where(sub_r == kk, idxs[kk].astype(I32), eidx_o)
        wts_o = jnp.where(sub_r == kk, raw_w[kk] / wsum * ROUTED_SCALE, wts_o)
        rank_o = jnp.where(sub_r == kk, rk, rank_o)
    eidx_ref[0] = eidx_o
    wts_ref[0] = wts_o
    rank_ref[0] = rank_o
    tcnt_ref[0] = lax.dot_general(jnp.ones((ROUTE_ROWS, tile), BF16), onehot_b, _NT,
                                  preferred_element_type=F32)
    run[...] = run[...] + jnp.sum(onehot, axis=1, keepdims=True)


def _front_call(x, mod, gains, hist_pool, hist_k, hist_v, cnt0, sinks,
                w_in, w_pool, pool_scale, w_out, wr_t, rbias, *, tile, has_cache):
    bsz, seq, _ = x.shape
    n_tiles = seq // tile
    hist_keys = hist_k.shape[1]
    assert hist_keys == ATTN_KEYS - min(tile, LANES)
    body = functools.partial(_front_body, tile, has_cache)
    whole = lambda shape: pl.BlockSpec(shape, lambda b, i: (0,) * len(shape))
    per_b = lambda shape: pl.BlockSpec((1,) + shape, lambda b, i: (b,) + (0,) * len(shape))
    route = pl.BlockSpec((1, ROUTE_ROWS, tile), lambda b, i: (b * n_tiles + i, 0, 0))
    out_shape = [
        jax.ShapeDtypeStruct((bsz, seq, D_MODEL), F32),
        jax.ShapeDtypeStruct((bsz, seq, D_MODEL), BF16),
        jax.ShapeDtypeStruct((bsz, POOL_HIST_PAD, POOL_WIDTH), F32),
        jax.ShapeDtypeStruct((bsz, WINDOW, KV_WIDTH), F32),
        jax.ShapeDtypeStruct((bsz, WINDOW, KV_WIDTH), F32),
        jax.ShapeDtypeStruct((bsz * n_tiles, ROUTE_ROWS, tile), I32),
        jax.ShapeDtypeStruct((bsz * n_tiles, ROUTE_ROWS, tile), F32),
        jax.ShapeDtypeStruct((bsz * n_tiles, ROUTE_ROWS, tile), I32),
        jax.ShapeDtypeStruct((bsz * n_tiles, ROUTE_ROWS, N_EXPERTS), F32),
    ]
    return pl.pallas_call(
        body,
        grid=(bsz, n_tiles),
        in_specs=[
            pl.BlockSpec(memory_space=pltpu.SMEM),
            pl.BlockSpec((1, tile, D_MODEL), lambda b, i: (b, i, 0)),
            per_b((6, D_MODEL)),
            whole((4, D_MODEL)),
            per_b((POOL_HIST_PAD, POOL_WIDTH)),
            per_b((hist_keys, KV_WIDTH)),
            per_b((hist_keys, KV_WIDTH)),
            whole((N_EXPERTS, LANES)),
            whole((D_MODEL, IN_WIDTH)),
            whole((len(POOL_WINDOWS), POOL_GROUP_W, POOL_GROUP_W)),
            whole((1, POOL_WIDTH)),
            whole((D_MODEL, D_MODEL)),
            whole((2 * N_EXPERTS, D_MODEL)),
            whole((N_EXPERTS, 1)),
        ],
        out_specs=[
            pl.BlockSpec((1, tile, D_MODEL), lambda b, i: (b, i, 0)),
            pl.BlockSpec((1, tile, D_MODEL), lambda b, i: (b, i, 0)),
            per_b((POOL_HIST_PAD, POOL_WIDTH)),
            per_b((WINDOW, KV_WIDTH)),
            per_b((WINDOW, KV_WIDTH)),
            route, route, route,
            pl.BlockSpec((1, ROUTE_ROWS, N_EXPERTS), lambda b, i: (b * n_tiles + i, 0, 0)),
        ],
        out_shape=out_shape,
        scratch_shapes=[
            pltpu.VMEM((tile + POOL_HIST_PAD, POOL_WIDTH), F32),
            pltpu.VMEM((hist_keys, KV_WIDTH), F32),
            pltpu.VMEM((hist_keys, KV_WIDTH), F32),
            pltpu.VMEM((N_EXPERTS, LANES), F32),
        ],
        compiler_params=pltpu.CompilerParams(
            dimension_semantics=("arbitrary", "arbitrary"), vmem_limit_bytes=VMEM_LIMIT),
        name="front_cached" if has_cache else "front_prompt",
    )(sinks, x, mod, gains, hist_pool, hist_k, hist_v, cnt0,
      w_in, w_pool, pool_scale, w_out, wr_t, rbias)


def _run_copy(src, dst, s_row, d_row, sem, rows=RUN_ROWS):
    return pltpu.make_async_copy(src.at[pl.ds(s_row, rows)], dst.at[pl.ds(d_row, rows)], sem)


def _for_each_run_chunk(step, chunk_tables, fn):
    lrow2_ref, grow2_ref, n2_ref, lrow1_ref, grow1_ref, n1_ref = chunk_tables

    def per_double(j, carry):
        idx = step * MAX_DOUBLE_CHUNKS + j
        fn(pl.multiple_of(lrow2_ref[idx], RUN_ROWS), pl.multiple_of(grow2_ref[idx], RUN_ROWS), 2 * RUN_ROWS)
        return carry

    def per_single(j, carry):
        idx = step * N_EXPERTS + j
        fn(pl.multiple_of(lrow1_ref[idx], RUN_ROWS), pl.multiple_of(grow1_ref[idx], RUN_ROWS), RUN_ROWS)
        return carry

    lax.fori_loop(0, n2_ref[step], per_double, 0)
    lax.fori_loop(0, n1_ref[step], per_single, 0)


def _wait_run_chunks(src, dst, sem, n_double, n_single):
    def one_double(c, carry):
        _run_copy(src, dst, 0, 0, sem, 2 * RUN_ROWS).wait()
        return carry

    def one_single(c, carry):
        _run_copy(src, dst, 0, 0, sem).wait()
        return carry

    lax.fori_loop(0, n_double, one_double, 0)
    lax.fori_loop(0, n_single, one_single, 0)


def _for_row_option(used_rows, fn):
    lower = 0
    for n_rows in LOCAL_ROW_OPTIONS:
        @pl.when((used_rows > lower) & (used_rows <= n_rows))
        def _(n_rows=n_rows):
            fn(n_rows)
        lower = n_rows


def _dispatch_body(n_prompt_steps, lrow2_ref, grow2_ref, n2_ref, lrow1_ref, grow1_ref, n1_ref, used_ref,
                   zrow_ref, znch_ref, nused_ref,
                   h2p_ref, h2s_ref, eidx_ref, rank_ref, off_ref, xs_out, slot_out, loc, zrows, sem):
    chunk_tables = (lrow2_ref, grow2_ref, n2_ref, lrow1_ref, grow1_ref, n1_ref)
    i = pl.program_id(0)
    n_steps = pl.num_programs(0)
    par = lax.rem(i, 2)
    h2 = jnp.where(i < n_prompt_steps, h2p_ref[...], h2s_ref[...])
    eidx = eidx_ref[0]
    rank = rank_ref[0]
    off = off_ref[0]
    expert_id = lax.broadcasted_iota(I32, (N_EXPERTS, MOE_TILE), 0)
    sub_r = lax.broadcasted_iota(I32, (ROUTE_ROWS, MOE_TILE), 0)
    slots = []
    slot_o = jnp.zeros((ROUTE_ROWS, MOE_TILE), I32)
    for kk in range(TOP_K):
        mine = jnp.sum(jnp.where(expert_id == eidx[kk:kk + 1, :], off, 0.0), axis=0, keepdims=True)
        slots.append(rank[kk:kk + 1, :] + mine.astype(I32))
        slot_o = jnp.where(sub_r == kk, slots[kk], slot_o)
    slot_out[0] = slot_o
    buf = loc.at[par]

    def sort_tile(n_rows):
        row_id = lax.broadcasted_iota(I32, (n_rows, MOE_TILE), 0).astype(jnp.int16)
        sel = jnp.zeros((n_rows, MOE_TILE), BF16)
        for kk in range(TOP_K):
            sel = jnp.where(row_id == slots[kk].astype(jnp.int16), jnp.ones((), BF16), sel)
        buf[0:n_rows, :] = jnp.dot(sel, h2, preferred_element_type=F32).astype(BF16)

    _for_row_option(used_ref[i], sort_tile)

    @pl.when(i > 0)
    def _():
        _wait_run_chunks(buf, xs_out, sem, n2_ref[i - 1], n1_ref[i - 1])

    _for_each_run_chunk(i, chunk_tables,
                        lambda lrow, grow, rows: _run_copy(buf, xs_out, lrow, grow, sem, rows).start())

    @pl.when(i == n_steps - 1)
    def _():
        zrows[...] = jnp.zeros_like(zrows)

        def per_expert(e, total):
            def per_chunk(c, carry):
                _run_copy(zrows, xs_out, 0, pl.multiple_of(zrow_ref[e] + RUN_ROWS * c, RUN_ROWS), sem).start()
                return carry
            lax.fori_loop(0, znch_ref[e], per_chunk, 0)
            return total + znch_ref[e]

        n_zero = lax.fori_loop(0, N_EXPERTS, per_expert, 0)
        _wait_run_chunks(buf, xs_out, sem, n2_ref[i], n1_ref[i] + n_zero)

        def block_copy(blk):
            return pltpu.make_async_copy(
                zrows, xs_out.at[pl.ds(pl.multiple_of(blk * EXPERT_ROWS, EXPERT_ROWS), EXPERT_ROWS)], sem)

        n_blocks = xs_out.shape[0] // EXPERT_ROWS

        def start_block(blk, carry):
            block_copy(blk).start()
            return carry

        def wait_block(blk, carry):
            block_copy(blk).wait()
            return carry

        lax.fori_loop(nused_ref[0], n_blocks, start_block, 0)
        lax.fori_loop(nused_ref[0], n_blocks, wait_block, 0)


def _dispatch_call(tables, zero_tables, h2_p, h2_s, eidx_tiles, rank_tiles, off_tiles, n_rows):
    n_prompt_steps = h2_p.shape[0] // MOE_TILE
    n_steps = n_prompt_steps + h2_s.shape[0] // MOE_TILE
    route = lambda: pl.BlockSpec((1, ROUTE_ROWS, MOE_TILE), lambda i, *_: (i, 0, 0))
    grid_spec = pltpu.PrefetchScalarGridSpec(
        num_scalar_prefetch=10,
        grid=(n_steps,),
        in_specs=[
            pl.BlockSpec((MOE_TILE, D_MODEL), lambda i, *_: (jnp.minimum(i, n_prompt_steps - 1), 0)),
            pl.BlockSpec((MOE_TILE, D_MODEL), lambda i, *_: (jnp.maximum(i - n_prompt_steps, 0), 0)),
            route(), route(),
            pl.BlockSpec((1, N_EXPERTS, 1), lambda i, *_: (i, 0, 0)),
        ],
        out_specs=[pl.BlockSpec(memory_space=pl.ANY), route()],
        scratch_shapes=[
            pltpu.VMEM((2, LOCAL_ROWS, ROW_W), BF16),
            pltpu.VMEM((EXPERT_ROWS, ROW_W), BF16),
            pltpu.SemaphoreType.DMA,
        ],
    )
    return pl.pallas_call(
        functools.partial(_dispatch_body, n_prompt_steps),
        grid_spec=grid_spec,
        out_shape=[jax.ShapeDtypeStruct((n_rows, ROW_W), BF16),
                   jax.ShapeDtypeStruct((n_steps, ROUTE_ROWS, MOE_TILE), I32)],
        compiler_params=pltpu.CompilerParams(
            dimension_semantics=("arbitrary",), vmem_limit_bytes=VMEM_LIMIT),
        name="dispatch",
    )(*tables, *zero_tables, h2_p, h2_s, eidx_tiles, rank_tiles, off_tiles)


def _experts_body(first_ref, nblk_ref, xs_hbm, wg_ref, wu_ref, wd_ref, ys_hbm,
                  xbuf, ybuf, wgb, wub, wdb, isem, osem):
    e = pl.program_id(0)
    n_exp = pl.num_programs(0)

    def items_of(expert):
        n_blk = nblk_ref[expert]
        n_big = n_blk // BIG_BLOCKS
        return first_ref[expert], n_big, n_big + n_blk - n_big * BIG_BLOCKS

    def item_copy(first, n_big, t, slot, rows_static, outward):
        if rows_static == BIG_ROWS:
            row0 = first * EXPERT_ROWS + t * BIG_ROWS
        else:
            row0 = first * EXPERT_ROWS + n_big * BIG_ROWS + (t - n_big) * EXPERT_ROWS
        row0 = pl.multiple_of(row0, EXPERT_ROWS)
        if outward:
            return pltpu.make_async_copy(ybuf.at[slot, pl.ds(0, rows_static)],
                                         ys_hbm.at[pl.ds(row0, rows_static)], osem.at[slot])
        return pltpu.make_async_copy(xs_hbm.at[pl.ds(row0, rows_static)],
                                     xbuf.at[slot, pl.ds(0, rows_static)], isem.at[slot])

    def for_item(first, n_big, t, slot, outward, action):
        @pl.when(t < n_big)
        def _():
            action(item_copy(first, n_big, t, slot, BIG_ROWS, outward))

        @pl.when(t >= n_big)
        def _():
            action(item_copy(first, n_big, t, slot, EXPERT_ROWS, outward))

    start = lambda c: c.start()
    wait = lambda c: c.wait()
    first, n_big, n_items = items_of(e)

    @pl.when((e == 0) & (n_items > 0))
    def _():
        for_item(first, n_big, 0, 0, False, start)

    def compute(slot, rows_static):
        x = xbuf[slot, 0:rows_static, :]
        g = jnp.dot(x, wgb[...], preferred_element_type=F32)
        u = jnp.dot(x, wub[...], preferred_element_type=F32)
        a = (g * _sigmoid(g) * u).astype(BF16)
        ybuf[slot, 0:rows_static, :] = jnp.dot(a, wdb[...], preferred_element_type=F32).astype(BF16)

    @pl.when(n_items > 0)
    def _():
        wgb[...] = wg_ref[0].astype(BF16)
        wub[...] = wu_ref[0].astype(BF16)
        wdb[...] = wd_ref[0].astype(BF16)

        def one_item(t, carry):
            slot = lax.rem(t, 2)

            @pl.when(t + 1 < n_items)
            def _():
                for_item(first, n_big, t + 1, 1 - slot, False, start)

            for_item(first, n_big, t, slot, False, wait)

            @pl.when(t < n_big)
            def _():
                compute(slot, BIG_ROWS)

            @pl.when(t >= n_big)
            def _():
                compute(slot, EXPERT_ROWS)

            @pl.when(t >= 1)
            def _():
                for_item(first, n_big, t - 1, 1 - slot, True, wait)

            for_item(first, n_big, t, slot, True, start)
            return carry

        lax.fori_loop(0, n_items, one_item, 0)
        for_item(first, n_big, n_items - 1, lax.rem(n_items - 1, 2), True, wait)

    nxt = jnp.minimum(e + 1, n_exp - 1)
    nxt_first, nxt_big, nxt_items = items_of(nxt)

    @pl.when((e + 1 < n_exp) & (nxt_items > 0))
    def _():
        for_item(nxt_first, nxt_big, 0, 0, False, start)


def _experts_call(first_block, n_expert_blocks, xs, w_gate, w_up, w_down):
    grid_spec = pltpu.PrefetchScalarGridSpec(
        num_scalar_prefetch=2,
        grid=(N_EXPERTS,),
        in_specs=[
            pl.BlockSpec(memory_space=pl.ANY),
            pl.BlockSpec((1, D_MODEL, D_EXPERT), lambda e, *_: (e, 0, 0)),
            pl.BlockSpec((1, D_MODEL, D_EXPERT), lambda e, *_: (e, 0, 0)),
            pl.BlockSpec((1, D_EXPERT, D_MODEL), lambda e, *_: (e, 0, 0)),
        ],
        out_specs=pl.BlockSpec(memory_space=pl.ANY),
        scratch_shapes=[
            pltpu.VMEM((2, BIG_ROWS, ROW_W), BF16),
            pltpu.VMEM((2, BIG_ROWS, D_MODEL), BF16),
            pltpu.VMEM((D_MODEL, D_EXPERT), BF16),
            pltpu.VMEM((D_MODEL, D_EXPERT), BF16),
            pltpu.VMEM((D_EXPERT, D_MODEL), BF16),
            pltpu.SemaphoreType.DMA((2,)),
            pltpu.SemaphoreType.DMA((2,)),
        ],
    )
    return pl.pallas_call(
        _experts_body,
        grid_spec=grid_spec,
        out_shape=jax.ShapeDtypeStruct(xs.shape, xs.dtype),
        input_output_aliases={2: 0},
        compiler_params=pltpu.CompilerParams(
            dimension_semantics=("arbitrary",), vmem_limit_bytes=VMEM_LIMIT),
        name="experts",
    )(first_block, n_expert_blocks, xs, w_gate, w_up, w_down)


def _combine_body(n_prompt_steps, lrow2_ref, grow2_ref, n2_ref, lrow1_ref, grow1_ref, n1_ref, used_ref,
                  x1p_ref, x1s_ref, h2p_ref, h2s_ref, gate_ref, gains_ref, slot_ref, wts_ref, ys_hbm,
                  wsg_ref, wsu_ref, wsd_ref, outp_ref, outs_ref, gath, routed, sem):
    chunk_tables = (lrow2_ref, grow2_ref, n2_ref, lrow1_ref, grow1_ref, n1_ref)
    i = pl.program_id(0)
    n_steps = pl.num_programs(0)
    par = lax.rem(i, 2)

    def fetch(step, slot):
        buf = gath.at[slot]
        _for_each_run_chunk(
            step, chunk_tables,
            lambda lrow, grow, rows: _run_copy(ys_hbm, buf, grow, lrow, sem.at[slot], rows).start())

    @pl.when(i == 0)
    def _():
        gath[...] = jnp.zeros_like(gath)
        fetch(0, 0)

    @pl.when(i + 1 < n_steps)
    def _():
        fetch(i + 1, 1 - par)

    is_prompt = i < n_prompt_steps
    h2 = jnp.where(is_prompt, h2p_ref[...], h2s_ref[...])
    g = jnp.dot(h2, wsg_ref[...], preferred_element_type=F32)
    u = jnp.dot(h2, wsu_ref[...], preferred_element_type=F32)
    a = (g * _sigmoid(g) * u).astype(BF16)
    ffn = jnp.dot(a, wsd_ref[...], preferred_element_type=F32)

    slot_cols = _to_sublanes(slot_ref[0].astype(F32))
    slot16 = slot_cols.astype(I32).astype(jnp.int16)
    w16 = _to_sublanes(wts_ref[0]).astype(BF16)
    buf = gath.at[par]

    _wait_run_chunks(ys_hbm, buf, sem.at[par], n2_ref[i], n1_ref[i])

    def unsort(n_rows):
        col_id = lax.broadcasted_iota(I32, (MOE_TILE, n_rows), 1).astype(jnp.int16)
        take = jnp.zeros((MOE_TILE, n_rows), BF16)
        for kk in range(TOP_K):
            take = jnp.where(col_id == slot16[:, kk:kk + 1], w16[:, kk:kk + 1], take)
        routed[...] = jnp.dot(take, buf[0:n_rows, :], preferred_element_type=F32)

    _for_row_option(used_ref[i], unsort)
    ffn = ffn + routed[...]

    quarter = MOE_TILE // gate_ref.shape[0]
    gate = jnp.concatenate(
        [jnp.broadcast_to(gate_ref[q:q + 1, :], (quarter, D_MODEL)) for q in range(gate_ref.shape[0])], axis=0)
    x1 = jnp.where(is_prompt, x1p_ref[...], x1s_ref[...])
    out = x1 + gate * (_rms(ffn) * gains_ref[3:4, :])

    @pl.when(is_prompt)
    def _():
        outp_ref[...] = out

    @pl.when(jnp.logical_not(is_prompt))
    def _():
        outs_ref[...] = out


def _combine_call(tables, x1_p, x1_s, h2_p, h2_s, gate_groups, gains, slot_tiles, wts_tiles, ys,
                  ws_gate, ws_up, ws_down):
    n_prompt, n_sample = x1_p.shape[0], x1_s.shape[0]
    n_prompt_steps = n_prompt // MOE_TILE
    n_steps = n_prompt_steps + n_sample // MOE_TILE
    groups_per_tile = gate_groups.shape[0] // n_steps
    tok_p = lambda: pl.BlockSpec((MOE_TILE, D_MODEL), lambda i, *_: (jnp.minimum(i, n_prompt_steps - 1), 0))
    tok_s = lambda: pl.BlockSpec((MOE_TILE, D_MODEL), lambda i, *_: (jnp.maximum(i - n_prompt_steps, 0), 0))
    whole = lambda shape: pl.BlockSpec(shape, lambda i, *_: (0,) * len(shape))
    grid_spec = pltpu.PrefetchScalarGridSpec(
        num_scalar_prefetch=7,
        grid=(n_steps,),
        in_specs=[
            tok_p(), tok_s(), tok_p(), tok_s(),
            pl.BlockSpec((groups_per_tile, D_MODEL), lambda i, *_: (i, 0)),
            whole((4, D_MODEL)),
            pl.BlockSpec((1, ROUTE_ROWS, MOE_TILE), lambda i, *_: (i, 0, 0)),
            pl.BlockSpec((1, ROUTE_ROWS, MOE_TILE), lambda i, *_: (i, 0, 0)),
            pl.BlockSpec(memory_space=pl.ANY),
            whole((D_MODEL, D_SHARED)), whole((D_MODEL, D_SHARED)), whole((D_SHARED, D_MODEL)),
        ],
        out_specs=[tok_p(), tok_s()],
        scratch_shapes=[
            pltpu.VMEM((2, LOCAL_ROWS, D_MODEL), BF16),
            pltpu.VMEM((MOE_TILE, D_MODEL), F32),
            pltpu.SemaphoreType.DMA((2,)),
        ],
    )
    return pl.pallas_call(
        functools.partial(_combine_body, n_prompt_steps),
        grid_spec=grid_spec,
        out_shape=[jax.ShapeDtypeStruct((n_prompt, D_MODEL), F32),
                   jax.ShapeDtypeStruct((n_sample, D_MODEL), F32)],
        compiler_params=pltpu.CompilerParams(
            dimension_semantics=("arbitrary",), vmem_limit_bytes=VMEM_LIMIT),
        name="combine",
    )(*tables, x1_p, x1_s, h2_p, h2_s, gate_groups, gains, slot_tiles, wts_tiles, ys, ws_gate, ws_up, ws_down)


def kernel(x_prompt, x_sample, c_prompt, c_sample, state_pool, cache_k, cache_v, w_ada, b_ada, norm_gains,
           w_in, w_pool, pool_scale, attn_sinks, w_out, w_router, router_bias, w_gate, w_up, w_down,
           ws_gate, ws_up, ws_down):
    assert w_ada.shape[0] == 1, "single-layer kernel"
    bsz, seq, _ = x_prompt.shape
    dbsz, dseq, _ = x_sample.shape
    n_prompt, n_sample = bsz * seq, dbsz * dseq
    n_tok = n_prompt + n_sample
    assert dseq == CHUNK and seq % MOE_TILE == 0 and n_sample % MOE_TILE == 0

    w_in0, w_out0 = w_in[0], w_out[0]
    wq = w_in0[:, POOL_WIDTH:POOL_WIDTH + ATTN_WIDTH].reshape(D_MODEL, N_KV_HEADS, GQ, HEAD_DIM)
    wq = jnp.transpose(wq, (0, 2, 1, 3)).reshape(D_MODEL, ATTN_WIDTH)
    w_in_p = jnp.concatenate(
        [w_in0[:, :POOL_WIDTH], wq, w_in0[:, POOL_WIDTH + ATTN_WIDTH:]], axis=1).astype(BF16)
    wo = w_out0[POOL_WIDTH:].reshape(N_KV_HEADS, GQ, HEAD_DIM, D_MODEL)
    wo = jnp.transpose(wo, (1, 0, 2, 3)).reshape(ATTN_WIDTH, D_MODEL)
    w_out_p = jnp.concatenate([w_out0[:POOL_WIDTH], wo], axis=0).astype(BF16)
    w_pool_b = w_pool[0].astype(BF16)
    pscale = pool_scale[0].reshape(1, POOL_WIDTH)
    wr_t = w_router[0].T
    wr_hi = wr_t.astype(BF16)
    wr_lo = (wr_t - wr_hi.astype(F32)).astype(BF16)
    wr_split = jnp.concatenate([wr_hi, wr_lo], axis=0)
    rbias = router_bias[0].reshape(N_EXPERTS, 1)
    gains = norm_gains[0]
    sinks = attn_sinks[0]

    mod = _ada_call(jnp.concatenate([c_prompt, c_sample], axis=0), w_ada[0], b_ada[0])
    mod = mod.reshape(bsz + dbsz, 6, D_MODEL)

    zeros_pool = jnp.zeros((bsz, POOL_HIST_PAD, POOL_WIDTH), F32)
    zeros_kv = jnp.zeros((bsz, WINDOW, KV_WIDTH), F32)
    cnt0 = jnp.zeros((N_EXPERTS, LANES), F32)
    (x1_p, h2_p, pool_p, nk_p, nv_p, eidx_p, wts_p, rank_p, tcnt_p) = _front_call(
        x_prompt, mod[:bsz], gains, zeros_pool, zeros_kv, zeros_kv, cnt0, sinks,
        w_in_p, w_pool_b, pscale, w_out_p, wr_split, rbias, tile=MOE_TILE, has_cache=False)
    cnt_prompt = jnp.broadcast_to(jnp.sum(tcnt_p[:, 0, :], axis=0)[:, None], (N_EXPERTS, LANES))

    hist_pad = ATTN_KEYS - CHUNK - WINDOW
    pool_s0 = jnp.pad(state_pool[0], ((0, 0), (POOL_HIST_PAD - state_pool.shape[2], 0), (0, 0)))
    ck = jnp.pad(cache_k[0].reshape(dbsz, WINDOW, KV_WIDTH), ((0, 0), (hist_pad, 0), (0, 0)))
    cv = jnp.pad(cache_v[0].reshape(dbsz, WINDOW, KV_WIDTH), ((0, 0), (hist_pad, 0), (0, 0)))
    (x1_s, h2_s, pool_s, nk_s, nv_s, eidx_s, wts_s, rank_s, tcnt_s) = _front_call(
        x_sample, mod[bsz:], gains, pool_s0, ck, cv, cnt_prompt, sinks,
        w_in_p, w_pool_b, pscale, w_out_p, wr_split, rbias, tile=CHUNK, has_cache=True)

    def moe_tiles(route_p, route_s):
        per = MOE_TILE // CHUNK
        regrouped = jnp.transpose(route_s.reshape(-1, per, ROUTE_ROWS, CHUNK), (0, 2, 1, 3))
        return jnp.concatenate([route_p, regrouped.reshape(-1, ROUTE_ROWS, MOE_TILE)], axis=0)

    n_steps = n_tok // MOE_TILE
    cnt_td = jnp.concatenate(
        [tcnt_p[:, 0, :], tcnt_s[:, 0, :].reshape(-1, MOE_TILE // CHUNK, N_EXPERTS).sum(axis=1)],
        axis=0).astype(I32)
    c8 = (cnt_td + RUN_ROWS - 1) // RUN_ROWS * RUN_ROWS
    rows8 = jnp.sum(c8, axis=0)
    padded = (rows8 + EXPERT_ROWS - 1) // EXPERT_ROWS * EXPERT_ROWS
    pends = jnp.cumsum(padded)
    pstarts = pends - padded
    base = jnp.cumsum(cnt_td, axis=0) - cnt_td
    grow = pstarts[None, :] + jnp.cumsum(c8, axis=0) - c8
    lo8 = jnp.cumsum(c8, axis=1) - c8
    nch = c8 // RUN_ROWS
    nct = jnp.sum(nch, axis=1)
    off_tiles = (lo8 - base).astype(F32).reshape(n_steps, N_EXPERTS, 1)
    def flat_chunks(counts, max_n, local0, global0, rows):
        cum = jnp.cumsum(counts, axis=1)
        j = jnp.arange(max_n, dtype=I32)
        owner = jnp.sum(j[None, :, None] >= cum[:, None, :], axis=-1)
        owns = owner[:, :, None] == jnp.arange(N_EXPERTS, dtype=I32)[None, None, :]
        of_owner = lambda tab: jnp.sum(jnp.where(owns, tab[:, None, :], 0), axis=-1)
        within = rows * (j[None, :] - of_owner(cum - counts))
        return ((of_owner(local0) + within).reshape(-1).astype(I32),
                (of_owner(global0) + within).reshape(-1).astype(I32), cum[:, -1].astype(I32))

    n_double = nch // 2
    doubled = 2 * RUN_ROWS * n_double
    tables = (flat_chunks(n_double, MAX_DOUBLE_CHUNKS, lo8, grow, 2 * RUN_ROWS)
              + flat_chunks(nch - 2 * n_double, N_EXPERTS, lo8 + doubled, grow + doubled, RUN_ROWS)
              + ((nct * RUN_ROWS).astype(I32),))
    n_blocks = -(-(n_tok * TOP_K + n_steps * N_EXPERTS * (RUN_ROWS - 1) + N_EXPERTS * (EXPERT_ROWS - 1))
                 // EXPERT_ROWS)
    n_used = (pends[-1] // EXPERT_ROWS).astype(I32).reshape(1)
    zero_tables = ((pstarts + rows8).astype(I32), ((padded - rows8) // RUN_ROWS).astype(I32), n_used)

    h2_pf, h2_sf = h2_p.reshape(n_prompt, D_MODEL), h2_s.reshape(n_sample, D_MODEL)
    xs, slot_tiles = _dispatch_call(
        tables, zero_tables, h2_pf, h2_sf, moe_tiles(eidx_p, eidx_s), moe_tiles(rank_p, rank_s), off_tiles,
        n_blocks * EXPERT_ROWS)
    ys = _experts_call((pstarts // EXPERT_ROWS).astype(I32), (padded // EXPERT_ROWS).astype(I32),
                       xs, w_gate[0], w_up[0], w_down[0])
    gate_groups = jnp.concatenate(
        [jnp.repeat(mod[:bsz, 5], seq // GATE_GROUP, axis=0),
         jnp.repeat(mod[bsz:, 5], dseq // GATE_GROUP, axis=0)], axis=0)
    y_p, y_s = _combine_call(
        tables, x1_p.reshape(n_prompt, D_MODEL), x1_s.reshape(n_sample, D_MODEL), h2_pf, h2_sf,
        gate_groups, gains, slot_tiles, moe_tiles(wts_p, wts_s), ys,
        ws_gate[0].astype(BF16), ws_up[0].astype(BF16), ws_down[0].astype(BF16))

    n_hist = state_pool.shape[2]
    kv_shape = (1, -1, WINDOW, N_KV_HEADS, HEAD_DIM)
    return (y_p.reshape(bsz, seq, D_MODEL), y_s.reshape(dbsz, dseq, D_MODEL),
            pool_p[None, :, POOL_HIST_PAD - n_hist:], nk_p.reshape(kv_shape), nv_p.reshape(kv_shape),
            pool_s[None, :, POOL_HIST_PAD - n_hist:], nk_s.reshape(kv_shape), nv_s.reshape(kv_shape))
```

```python
import functools

import jax
import jax.numpy as jnp
import numpy as np
from jax import lax
from jax.experimental import pallas as pl
from jax.experimental.pallas import tpu as pltpu

F32 = jnp.float32
BF16 = jnp.bfloat16
I32 = jnp.int32

D_MODEL = 1024
CHUNK = 64
POOL_WIDTH = 512
POOL_WINDOWS = (2, 4, 8, 16)
POOL_GROUP_W = 128
POOL_HIST_PAD = 16
HEAD_DIM = 64
N_HEADS = 8
N_KV_HEADS = 2
GQ = N_HEADS // N_KV_HEADS
ATTN_WIDTH = N_HEADS * HEAD_DIM
KV_WIDTH = N_KV_HEADS * HEAD_DIM
IN_WIDTH = POOL_WIDTH + ATTN_WIDTH + 2 * KV_WIDTH
WINDOW = 128
N_EXPERTS = 64
TOP_K = 6
N_EXPERT_GROUPS = 8
GROUP_SIZE = N_EXPERTS // N_EXPERT_GROUPS
TOPK_GROUPS = 4
D_EXPERT = 256
D_SHARED = 256
ROUTED_SCALE = 2.5
NORM_EPS = 1e-6
NEG_BIG = -1e30

LANES = 128
SUBLANES = 8
ATTN_KEYS = 2 * LANES
ROUTE_ROWS = 8
EXPERT_ROWS = 256
ITEM_BLOCKS = (4, 2, 1)
BIG_ROWS = ITEM_BLOCKS[0] * EXPERT_ROWS
MOE_TILE = 256
GATE_GROUP = MOE_TILE // SUBLANES
RUN_ROWS = 2 * SUBLANES
LOCAL_ROWS = 2560
LOCAL_ROW_OPTIONS = (2048, 2176, LOCAL_ROWS)
MAX_DOUBLE_CHUNKS = LOCAL_ROWS // (2 * RUN_ROWS)
ROW_W = D_MODEL
VMEM_LIMIT = 56 * 1024 * 1024

assert TOP_K * MOE_TILE + N_EXPERTS * (RUN_ROWS - 1) <= LOCAL_ROWS

_NT = (((1,), (1,)), ((), ()))


def _rms(v):
    return v * lax.rsqrt(jnp.mean(v * v, axis=-1, keepdims=True) + NORM_EPS)


def _sigmoid(v):
    return 1.0 / (1.0 + jnp.exp(-v))


def _split3(v):
    hi = v.astype(BF16)
    r1 = v - hi.astype(F32)
    mid = r1.astype(BF16)
    lo = (r1 - mid.astype(F32)).astype(BF16)
    return hi, mid, lo


def _to_sublanes(rows):
    n = rows.shape[1]
    hi, mid, lo = _split3(rows)
    eye = (lax.broadcasted_iota(I32, (n, n), 0) == lax.broadcasted_iota(I32, (n, n), 1)).astype(BF16)
    return (lax.dot_general(eye, hi, _NT, preferred_element_type=F32)
            + lax.dot_general(eye, mid, _NT, preferred_element_type=F32)
            + lax.dot_general(eye, lo, _NT, preferred_element_type=F32))


def _ada_body(c_ref, w_ref, b_ref, o_ref):
    c = c_ref[...]
    s = c * _sigmoid(c)
    o_ref[...] = jnp.dot(s, w_ref[...], preferred_element_type=F32) + b_ref[...]


def _ada_call(c_all, w_ada, b_ada):
    nb = c_all.shape[0]
    n_out = w_ada.shape[1]
    tile = D_MODEL
    return pl.pallas_call(
        _ada_body,
        grid=(n_out // tile,),
        in_specs=[
            pl.BlockSpec((nb, D_MODEL), lambda j: (0, 0)),
            pl.BlockSpec((D_MODEL, tile), lambda j: (0, j)),
            pl.BlockSpec((1, tile), lambda j: (0, j)),
        ],
        out_specs=pl.BlockSpec((nb, tile), lambda j: (0, j)),
        out_shape=jax.ShapeDtypeStruct((nb, n_out), F32),
        compiler_params=pltpu.CompilerParams(dimension_semantics=("arbitrary",)),
        name="ada",
    )(c_all, w_ada, b_ada.reshape(1, n_out))


def _front_body(tile, has_cache,
                sinks_ref, x_ref, mod_ref, gains_ref, hp_ref, hk_ref, hv_ref, cnt0_ref,
                win_ref, wpool_ref, pscale_ref, wout_ref, wrt_ref, rbias_ref,
                x1_ref, h2_ref, npool_ref, nk_ref, nv_ref, eidx_ref, wts_ref, rank_ref, tcnt_ref,
                ubuf, khist, vhist, run):
    b = pl.program_id(0)
    i = pl.program_id(1)
    sub_q = min(tile, LANES)
    n_sub = tile // sub_q
    hist_keys = ATTN_KEYS - sub_q
    n_keys = hist_keys + tile

    @pl.when(i == 0)
    def _():
        ubuf[0:POOL_HIST_PAD, :] = hp_ref[0]
        khist[...] = hk_ref[0]
        vhist[...] = hv_ref[0]

    @pl.when((b == 0) & (i == 0))
    def _():
        run[...] = cnt0_ref[...]

    mod = mod_ref[0]
    gains = gains_ref[...]
    xt = x_ref[0]

    h = _rms(xt) * gains[0:1] * (1.0 + mod[1:2]) + mod[0:1]
    proj = jnp.dot(h.astype(BF16), win_ref[...], preferred_element_type=F32)
    u = proj[:, :POOL_WIDTH]
    o_k = POOL_WIDTH + ATTN_WIDTH
    k_new = proj[:, o_k:o_k + KV_WIDTH]
    v_new = proj[:, o_k + KV_WIDTH:]

    kw = jnp.concatenate([khist[...], k_new], axis=0)
    vw = jnp.concatenate([vhist[...], v_new], axis=0)
    kwb = kw.astype(BF16)
    vwb = vw.astype(BF16)
    qrow = lax.broadcasted_iota(I32, (sub_q, ATTN_KEYS), 0)
    kpos = lax.broadcasted_iota(I32, (sub_q, ATTN_KEYS), 1) - hist_keys
    qchunk = jnp.right_shift(qrow, 6)
    vis_band = (kpos >= CHUNK * (qchunk - 2)) & (kpos < CHUNK * (qchunk + 1))
    side0 = lax.broadcasted_iota(I32, (sub_q, LANES), 1) < HEAD_DIM
    attn_rows = []
    for r in range(n_sub):
        vis = vis_band
        if not has_cache:
            vis = vis & (kpos + (i * tile + r * sub_q) >= 0)
        kb = kwb[r * sub_q:r * sub_q + ATTN_KEYS]
        vb = vwb[r * sub_q:r * sub_q + ATTN_KEYS]
        blocks = []
        for j in range(GQ):
            qp = proj[r * sub_q:(r + 1) * sub_q, POOL_WIDTH + LANES * j:POOL_WIDTH + LANES * (j + 1)]
            qp = qp * (HEAD_DIM ** -0.5)
            outs = []
            for s in range(N_KV_HEADS):
                keep = side0 if s == 0 else jnp.logical_not(side0)
                qm = jnp.where(keep, qp, 0.0).astype(BF16)
                sc = lax.dot_general(qm, kb, _NT, preferred_element_type=F32)
                sc = jnp.where(vis, sc, NEG_BIG)
                sink = sinks_ref[j + GQ * s]
                m = jnp.maximum(jnp.max(sc, axis=-1, keepdims=True), sink)
                p = jnp.exp(sc - m)
                den = jnp.sum(p, axis=-1, keepdims=True) + jnp.exp(sink - m)
                p = (p / den).astype(BF16)
                outs.append(jnp.dot(p, vb, preferred_element_type=F32))
            blocks.append(jnp.where(side0, outs[0], outs[1]))
        attn_rows.append(jnp.concatenate(blocks, axis=-1))
    attn = attn_rows[0] if n_sub == 1 else jnp.concatenate(attn_rows, axis=0)

    ubuf[POOL_HIST_PAD:POOL_HIST_PAD + tile, :] = u
    if has_cache:
        seen = None
    else:
        seen = (lax.broadcasted_iota(I32, (tile, 1), 0) + i * tile + 1).astype(F32)
    pool_blocks = []
    for g, w in enumerate(POOL_WINDOWS):
        cols = slice(POOL_GROUP_W * g, POOL_GROUP_W * (g + 1))
        acc = u[:, cols]
        for s in range(1, w):
            acc = acc + ubuf[POOL_HIST_PAD - s:POOL_HIST_PAD - s + tile, cols]
        cnt = float(w) if seen is None else jnp.minimum(seen, float(w))
        dlt = acc / cnt - u[:, cols]
        pool_blocks.append(jnp.dot(dlt.astype(BF16), wpool_ref[g], preferred_element_type=F32))
    pool = jnp.concatenate(pool_blocks, axis=-1) * pscale_ref[...]

    mixin = jnp.concatenate([pool, attn], axis=-1).astype(BF16)
    mix = jnp.dot(mixin, wout_ref[...], preferred_element_type=F32)
    x1 = xt + mod[2:3] * (_rms(mix) * gains[1:2])
    x1_ref[0] = x1

    new_hist = ubuf[tile:tile + POOL_HIST_PAD, :]
    ubuf[0:POOL_HIST_PAD, :] = new_hist
    npool_ref[0] = new_hist
    khist[...] = kw[tile:, :]
    vhist[...] = vw[tile:, :]
    nk_ref[0] = kw[n_keys - WINDOW:, :]
    nv_ref[0] = vw[n_keys - WINDOW:, :]

    h2f = _rms(x1) * gains[2:3] * (1.0 + mod[4:5]) + mod[3:4]
    h2hi = h2f.astype(BF16)
    h2_ref[0] = h2hi
    h2lo = (h2f - h2hi.astype(F32)).astype(BF16)
    wrt = wrt_ref[...]
    part = lax.dot_general(wrt, h2hi, _NT, preferred_element_type=F32)
    logits = (part[:N_EXPERTS] + part[N_EXPERTS:]
              + lax.dot_general(wrt[:N_EXPERTS], h2lo, _NT, preferred_element_type=F32))
    scores = _sigmoid(logits)
    sel = scores + rbias_ref[...]

    sub_g = lax.broadcasted_iota(I32, (GROUP_SIZE, tile), 0).astype(F32)
    gscore = jnp.zeros((N_EXPERT_GROUPS, tile), F32)
    for gi in range(N_EXPERT_GROUPS):
        blk = sel[GROUP_SIZE * gi:GROUP_SIZE * (gi + 1), :]
        m1 = jnp.max(blk, axis=0, keepdims=True)
        i1 = jnp.min(jnp.where(blk == m1, sub_g, float(GROUP_SIZE)), axis=0, keepdims=True)
        m2 = jnp.max(jnp.where(sub_g == i1, -jnp.inf, blk), axis=0, keepdims=True)
        gscore = jnp.where(sub_g == gi, m1 + m2, gscore)
    chosen = jnp.zeros((N_EXPERT_GROUPS, tile), F32)
    for _ in range(TOPK_GROUPS):
        m = jnp.max(gscore, axis=0, keepdims=True)
        idx = jnp.min(jnp.where(gscore == m, sub_g, float(N_EXPERT_GROUPS)), axis=0, keepdims=True)
        pick = sub_g == idx
        chosen = jnp.where(pick, 1.0, chosen)
        gscore = jnp.where(pick, -jnp.inf, gscore)
    emask = jnp.concatenate(
        [jnp.broadcast_to(chosen[gi:gi + 1, :], (GROUP_SIZE, tile)) for gi in range(N_EXPERT_GROUPS)], axis=0)
    selm = jnp.where(emask > 0.0, sel, -jnp.inf)

    sub_e = lax.broadcasted_iota(I32, (N_EXPERTS, tile), 0).astype(F32)
    picks, idxs, raw_w = [], [], []
    for _ in range(TOP_K):
        m = jnp.max(selm, axis=0, keepdims=True)
        idx = jnp.min(jnp.where(selm == m, sub_e, float(N_EXPERTS)), axis=0, keepdims=True)
        pick = sub_e == idx
        raw_w.append(jnp.sum(jnp.where(pick, scores, 0.0), axis=0, keepdims=True))
        selm = jnp.where(pick, -jnp.inf, selm)
        picks.append(pick)
        idxs.append(idx)
    wsum = raw_w[0]
    for kk in range(1, TOP_K):
        wsum = wsum + raw_w[kk]

    onehot = jnp.zeros((N_EXPERTS, tile), F32)
    for kk in range(TOP_K):
        onehot = jnp.where(picks[kk], 1.0, onehot)
    onehot_b = onehot.astype(BF16)
    tri = (lax.broadcasted_iota(I32, (tile, tile), 0) < lax.broadcasted_iota(I32, (tile, tile), 1)).astype(BF16)
    before = jnp.dot(onehot_b, tri, preferred_element_type=F32) + run[:, 0:1]
    sub_r = lax.broadcasted_iota(I32, (ROUTE_ROWS, tile), 0)
    eidx_o = jnp.zeros((ROUTE_ROWS, tile), I32)
    wts_o = jnp.zeros((ROUTE_ROWS, tile), F32)
    rank_o = jnp.zeros((ROUTE_ROWS, tile), I32)
    for kk in range(TOP_K):
        rk = jnp.sum(jnp.where(picks[kk], before, 0.0), axis=0, keepdims=True).astype(I32)
        eidx_o = jnp.where(sub_r == kk, idxs[kk].astype(I32), eidx_o)
        wts_o = jnp.where(sub_r == kk, raw_w[kk] / wsum * ROUTED_SCALE, wts_o)
        rank_o = jnp.where(sub_r == kk, rk, rank_o)
    eidx_ref[0] = eidx_o
    wts_ref[0] = wts_o
    rank_ref[0] = rank_o
    tcnt_ref[0] = lax.dot_general(jnp.ones((ROUTE_ROWS, tile), BF16), onehot_b, _NT,
                                  preferred_element_type=F32)
    run[...] = run[...] + jnp.sum(onehot, axis=1, keepdims=True)


def _front_call(x, mod, gains, hist_pool, hist_k, hist_v, cnt0, sinks,
                w_in, w_pool, pool_scale, w_out, wr_t, rbias, *, tile, has_cache):
    bsz, seq, _ = x.shape
    n_tiles = seq // tile
    hist_keys = hist_k.shape[1]
    assert hist_keys == ATTN_KEYS - min(tile, LANES)
    body = functools.partial(_front_body, tile, has_cache)
    whole = lambda shape: pl.BlockSpec(shape, lambda b, i: (0,) * len(shape))
    per_b = lambda shape: pl.BlockSpec((1,) + shape, lambda b, i: (b,) + (0,) * len(shape))
    route = pl.BlockSpec((1, ROUTE_ROWS, tile), lambda b, i: (b * n_tiles + i, 0, 0))
    out_shape = [
        jax.ShapeDtypeStruct((bsz, seq, D_MODEL), F32),
        jax.ShapeDtypeStruct((bsz, seq, D_MODEL), BF16),
        jax.ShapeDtypeStruct((bsz, POOL_HIST_PAD, POOL_WIDTH), F32),
        jax.ShapeDtypeStruct((bsz, WINDOW, KV_WIDTH), F32),
        jax.ShapeDtypeStruct((bsz, WINDOW, KV_WIDTH), F32),
        jax.ShapeDtypeStruct((bsz * n_tiles, ROUTE_ROWS, tile), I32),
        jax.ShapeDtypeStruct((bsz * n_tiles, ROUTE_ROWS, tile), F32),
        jax.ShapeDtypeStruct((bsz * n_tiles, ROUTE_ROWS, tile), I32),
        jax.ShapeDtypeStruct((bsz * n_tiles, ROUTE_ROWS, N_EXPERTS), F32),
    ]
    return pl.pallas_call(
        body,
        grid=(bsz, n_tiles),
        in_specs=[
            pl.BlockSpec(memory_space=pltpu.SMEM),
            pl.BlockSpec((1, tile, D_MODEL), lambda b, i: (b, i, 0)),
            per_b((6, D_MODEL)),
            whole((4, D_MODEL)),
            per_b((POOL_HIST_PAD, POOL_WIDTH)),
            per_b((hist_keys, KV_WIDTH)),
            per_b((hist_keys, KV_WIDTH)),
            whole((N_EXPERTS, LANES)),
            whole((D_MODEL, IN_WIDTH)),
            whole((len(POOL_WINDOWS), POOL_GROUP_W, POOL_GROUP_W)),
            whole((1, POOL_WIDTH)),
            whole((D_MODEL, D_MODEL)),
            whole((2 * N_EXPERTS, D_MODEL)),
            whole((N_EXPERTS, 1)),
        ],
        out_specs=[
            pl.BlockSpec((1, tile, D_MODEL), lambda b, i: (b, i, 0)),
            pl.BlockSpec((1, tile, D_MODEL), lambda b, i: (b, i, 0)),
            per_b((POOL_HIST_PAD, POOL_WIDTH)),
            per_b((WINDOW, KV_WIDTH)),
            per_b((WINDOW, KV_WIDTH)),
            route, route, route,
            pl.BlockSpec((1, ROUTE_ROWS, N_EXPERTS), lambda b, i: (b * n_tiles + i, 0, 0)),
        ],
        out_shape=out_shape,
        scratch_shapes=[
            pltpu.VMEM((tile + POOL_HIST_PAD, POOL_WIDTH), F32),
            pltpu.VMEM((hist_keys, KV_WIDTH), F32),
            pltpu.VMEM((hist_keys, KV_WIDTH), F32),
            pltpu.VMEM((N_EXPERTS, LANES), F32),
        ],
        compiler_params=pltpu.CompilerParams(
            dimension_semantics=("arbitrary", "arbitrary"), vmem_limit_bytes=VMEM_LIMIT),
        name="front_cached" if has_cache else "front_prompt",
    )(sinks, x, mod, gains, hist_pool, hist_k, hist_v, cnt0,
      w_in, w_pool, pool_scale, w_out, wr_t, rbias)


def _run_copy(src, dst, s_row, d_row, sem, rows=RUN_ROWS):
    return pltpu.make_async_copy(src.at[pl.ds(s_row, rows)], dst.at[pl.ds(d_row, rows)], sem)


def _for_each_run_chunk(step, chunk_tables, fn):
    lrow2_ref, grow2_ref, n2_ref, lrow1_ref, grow1_ref, n1_ref = chunk_tables

    def per_double(j, carry):
        idx = step * MAX_DOUBLE_CHUNKS + j
        fn(pl.multiple_of(lrow2_ref[idx], RUN_ROWS), pl.multiple_of(grow2_ref[idx], RUN_ROWS), 2 * RUN_ROWS)
        return carry

    def per_single(j, carry):
        idx = step * N_EXPERTS + j
        fn(pl.multiple_of(lrow1_ref[idx], RUN_ROWS), pl.multiple_of(grow1_ref[idx], RUN_ROWS), RUN_ROWS)
        return carry

    lax.fori_loop(0, n2_ref[step], per_double, 0)
    lax.fori_loop(0, n1_ref[step], per_single, 0)


def _wait_run_chunks(src, dst, sem, n_double, n_single):
    def one_double(c, carry):
        _run_copy(src, dst, 0, 0, sem, 2 * RUN_ROWS).wait()
        return carry

    def one_single(c, carry):
        _run_copy(src, dst, 0, 0, sem).wait()
        return carry

    lax.fori_loop(0, n_double, one_double, 0)
    lax.fori_loop(0, n_single, one_single, 0)


def _for_row_option(used_rows, fn):
    lower = 0
    for n_rows in LOCAL_ROW_OPTIONS:
        @pl.when((used_rows > lower) & (used_rows <= n_rows))
        def _(n_rows=n_rows):
            fn(n_rows)
        lower = n_rows


def _dispatch_body(n_prompt_steps, lrow2_ref, grow2_ref, n2_ref, lrow1_ref, grow1_ref, n1_ref, used_ref,
                   zrow_ref, znch_ref, nused_ref,
                   h2p_ref, h2s_ref, eidx_ref, rank_ref, off_ref, xs_out, slot_out, loc, zrows, sem):
    chunk_tables = (lrow2_ref, grow2_ref, n2_ref, lrow1_ref, grow1_ref, n1_ref)
    i = pl.program_id(0)
    n_steps = pl.num_programs(0)
    par = lax.rem(i, 2)
    h2 = jnp.where(i < n_prompt_steps, h2p_ref[...], h2s_ref[...])
    eidx = eidx_ref[0]
    rank = rank_ref[0]
    off = off_ref[0]
    expert_id = lax.broadcasted_iota(I32, (N_EXPERTS, MOE_TILE), 0)
    sub_r = lax.broadcasted_iota(I32, (ROUTE_ROWS, MOE_TILE), 0)
    slots = []
    slot_o = jnp.zeros((ROUTE_ROWS, MOE_TILE), I32)
    for kk in range(TOP_K):
        mine = jnp.sum(jnp.where(expert_id == eidx[kk:kk + 1, :], off, 0.0), axis=0, keepdims=True)
        slots.append(rank[kk:kk + 1, :] + mine.astype(I32))
        slot_o = jnp.where(sub_r == kk, slots[kk], slot_o)
    slot_out[0] = slot_o
    buf = loc.at[par]

    def sort_tile(n_rows):
        row_id = lax.broadcasted_iota(I32, (n_rows, MOE_TILE), 0).astype(jnp.int16)
        sel = jnp.zeros((n_rows, MOE_TILE), BF16)
        for kk in range(TOP_K):
            sel = jnp.where(row_id == slots[kk].astype(jnp.int16), jnp.ones((), BF16), sel)
        buf[0:n_rows, :] = jnp.dot(sel, h2, preferred_element_type=F32).astype(BF16)

    _for_row_option(used_ref[i], sort_tile)

    @pl.when(i > 0)
    def _():
        _wait_run_chunks(buf, xs_out, sem, n2_ref[i - 1], n1_ref[i - 1])

    _for_each_run_chunk(i, chunk_tables,
                        lambda lrow, grow, rows: _run_copy(buf, xs_out, lrow, grow, sem, rows).start())

    @pl.when(i == n_steps - 1)
    def _():
        zrows[...] = jnp.zeros_like(zrows)

        def per_expert(e, total):
            def per_chunk(c, carry):
                _run_copy(zrows, xs_out, 0, pl.multiple_of(zrow_ref[e] + RUN_ROWS * c, RUN_ROWS), sem).start()
                return carry
            lax.fori_loop(0, znch_ref[e], per_chunk, 0)
            return total + znch_ref[e]

        n_zero = lax.fori_loop(0, N_EXPERTS, per_expert, 0)
        _wait_run_chunks(buf, xs_out, sem, n2_ref[i], n1_ref[i] + n_zero)

        def block_copy(blk):
            return pltpu.make_async_copy(
                zrows, xs_out.at[pl.ds(pl.multiple_of(blk * EXPERT_ROWS, EXPERT_ROWS), EXPERT_ROWS)], sem)

        n_blocks = xs_out.shape[0] // EXPERT_ROWS

        def start_block(blk, carry):
            block_copy(blk).start()
            return carry

        def wait_block(blk, carry):
            block_copy(blk).wait()
            return carry

        lax.fori_loop(nused_ref[0], n_blocks, start_block, 0)
        lax.fori_loop(nused_ref[0], n_blocks, wait_block, 0)


def _dispatch_call(tables, zero_tables, h2_p, h2_s, eidx_tiles, rank_tiles, off_tiles, n_rows):
    n_prompt_steps = h2_p.shape[0] // MOE_TILE
    n_steps = n_prompt_steps + h2_s.shape[0] // MOE_TILE
    route = lambda: pl.BlockSpec((1, ROUTE_ROWS, MOE_TILE), lambda i, *_: (i, 0, 0))
    grid_spec = pltpu.PrefetchScalarGridSpec(
        num_scalar_prefetch=10,
        grid=(n_steps,),
        in_specs=[
            pl.BlockSpec((MOE_TILE, D_MODEL), lambda i, *_: (jnp.minimum(i, n_prompt_steps - 1), 0)),
            pl.BlockSpec((MOE_TILE, D_MODEL), lambda i, *_: (jnp.maximum(i - n_prompt_steps, 0), 0)),
            route(), route(),
            pl.BlockSpec((1, N_EXPERTS, 1), lambda i, *_: (i, 0, 0)),
        ],
        out_specs=[pl.BlockSpec(memory_space=pl.ANY), route()],
        scratch_shapes=[
            pltpu.VMEM((2, LOCAL_ROWS, ROW_W), BF16),
            pltpu.VMEM((EXPERT_ROWS, ROW_W), BF16),
            pltpu.SemaphoreType.DMA,
        ],
    )
    return pl.pallas_call(
        functools.partial(_dispatch_body, n_prompt_steps),
        grid_spec=grid_spec,
        out_shape=[jax.ShapeDtypeStruct((n_rows, ROW_W), BF16),
                   jax.ShapeDtypeStruct((n_steps, ROUTE_ROWS, MOE_TILE), I32)],
        compiler_params=pltpu.CompilerParams(
            dimension_semantics=("arbitrary",), vmem_limit_bytes=VMEM_LIMIT),
        name="dispatch",
    )(*tables, *zero_tables, h2_p, h2_s, eidx_tiles, rank_tiles, off_tiles)


def _experts_body(first_ref, nblk_ref, xs_hbm, wg_ref, wu_ref, wd_ref, ys_hbm,
                  xbuf, ybuf, wgb, wub, wdb, isem, osem):
    e = pl.program_id(0)
    n_exp = pl.num_programs(0)

    def items_of(expert):
        rem = nblk_ref[expert]
        row = first_ref[expert] * EXPERT_ROWS
        done = 0
        lows, highs, bases = [], [], []
        for blocks in ITEM_BLOCKS:
            n = rem // blocks
            lows.append(done)
            bases.append(row)
            done = done + n
            row = row + n * (blocks * EXPERT_ROWS)
            rem = rem - n * blocks
            highs.append(done)
        return (lows, highs, bases), done

    def for_item(items, t, slot, action):
        lows, highs, bases = items
        for c, blocks in enumerate(ITEM_BLOCKS):
            rows = blocks * EXPERT_ROWS

            @pl.when((t >= lows[c]) & (t < highs[c]))
            def _(c=c, rows=rows):
                action(pl.multiple_of(bases[c] + (t - lows[c]) * rows, EXPERT_ROWS), rows, slot)

    def in_copy(row0, rows, slot):
        return pltpu.make_async_copy(xs_hbm.at[pl.ds(row0, rows)], xbuf.at[slot, pl.ds(0, rows)], isem.at[slot])

    def out_copy(row0, rows, slot):
        return pltpu.make_async_copy(ybuf.at[slot, pl.ds(0, rows)], ys_hbm.at[pl.ds(row0, rows)], osem.at[slot])

    start_in = lambda row0, rows, slot: in_copy(row0, rows, slot).start()
    wait_in = lambda row0, rows, slot: in_copy(row0, rows, slot).wait()
    start_out = lambda row0, rows, slot: out_copy(row0, rows, slot).start()
    wait_out = lambda row0, rows, slot: out_copy(row0, rows, slot).wait()
    items, n_items = items_of(e)

    @pl.when((e == 0) & (n_items > 0))
    def _():
        for_item(items, 0, 0, start_in)

    def compute(row0, rows, slot):
        del row0
        x = xbuf[slot, 0:rows, :]
        g = jnp.dot(x, wgb[...], preferred_element_type=F32)
        u = jnp.dot(x, wub[...], preferred_element_type=F32)
        a = (g * _sigmoid(g) * u).astype(BF16)
        ybuf[slot, 0:rows, :] = jnp.dot(a, wdb[...], preferred_element_type=F32).astype(BF16)

    @pl.when(n_items > 0)
    def _():
        wgb[...] = wg_ref[0].astype(BF16)
        wub[...] = wu_ref[0].astype(BF16)
        wdb[...] = wd_ref[0].astype(BF16)

        def one_item(t, carry):
            slot = lax.rem(t, 2)

            @pl.when(t + 1 < n_items)
            def _():
                for_item(items, t + 1, 1 - slot, start_in)

            for_item(items, t, slot, wait_in)
            for_item(items, t, slot, compute)

            @pl.when(t >= 1)
            def _():
                for_item(items, t - 1, 1 - slot, wait_out)

            for_item(items, t, slot, start_out)
            return carry

        lax.fori_loop(0, n_items, one_item, 0)
        for_item(items, n_items - 1, lax.rem(n_items - 1, 2), wait_out)

    nxt = jnp.minimum(e + 1, n_exp - 1)
    nxt_items, nxt_n = items_of(nxt)

    @pl.when((e + 1 < n_exp) & (nxt_n > 0))
    def _():
        for_item(nxt_items, 0, 0, start_in)


def _experts_call(first_block, n_expert_blocks, xs, w_gate, w_up, w_down):
    grid_spec = pltpu.PrefetchScalarGridSpec(
        num_scalar_prefetch=2,
        grid=(N_EXPERTS,),
        in_specs=[
            pl.BlockSpec(memory_space=pl.ANY),
            pl.BlockSpec((1, D_MODEL, D_EXPERT), lambda e, *_: (e, 0, 0)),
            pl.BlockSpec((1, D_MODEL, D_EXPERT), lambda e, *_: (e, 0, 0)),
            pl.BlockSpec((1, D_EXPERT, D_MODEL), lambda e, *_: (e, 0, 0)),
        ],
        out_specs=pl.BlockSpec(memory_space=pl.ANY),
        scratch_shapes=[
            pltpu.VMEM((2, BIG_ROWS, ROW_W), BF16),
            pltpu.VMEM((2, BIG_ROWS, D_MODEL), BF16),
            pltpu.VMEM((D_MODEL, D_EXPERT), BF16),
            pltpu.VMEM((D_MODEL, D_EXPERT), BF16),
            pltpu.VMEM((D_EXPERT, D_MODEL), BF16),
            pltpu.SemaphoreType.DMA((2,)),
            pltpu.SemaphoreType.DMA((2,)),
        ],
    )
    return pl.pallas_call(
        _experts_body,
        grid_spec=grid_spec,
        out_shape=jax.ShapeDtypeStruct(xs.shape, xs.dtype),
        input_output_aliases={2: 0},
        compiler_params=pltpu.CompilerParams(
            dimension_semantics=("arbitrary",), vmem_limit_bytes=VMEM_LIMIT),
        name="experts",
    )(first_block, n_expert_blocks, xs, w_gate, w_up, w_down)


def _combine_body(n_prompt_steps, lrow2_ref, grow2_ref, n2_ref, lrow1_ref, grow1_ref, n1_ref, used_ref,
                  x1p_ref, x1s_ref, h2p_ref, h2s_ref, gate_ref, gains_ref, slot_ref, wts_ref, ys_hbm,
                  wsg_ref, wsu_ref, wsd_ref, outp_ref, outs_ref, gath, routed, sem):
    chunk_tables = (lrow2_ref, grow2_ref, n2_ref, lrow1_ref, grow1_ref, n1_ref)
    i = pl.program_id(0)
    n_steps = pl.num_programs(0)
    par = lax.rem(i, 2)

    def fetch(step, slot):
        buf = gath.at[slot]
        _for_each_run_chunk(
            step, chunk_tables,
            lambda lrow, grow, rows: _run_copy(ys_hbm, buf, grow, lrow, sem.at[slot], rows).start())

    @pl.when(i == 0)
    def _():
        gath[...] = jnp.zeros_like(gath)
        fetch(0, 0)

    @pl.when(i + 1 < n_steps)
    def _():
        fetch(i + 1, 1 - par)

    is_prompt = i < n_prompt_steps
    h2 = jnp.where(is_prompt, h2p_ref[...], h2s_ref[...])
    g = jnp.dot(h2, wsg_ref[...], preferred_element_type=F32)
    u = jnp.dot(h2, wsu_ref[...], preferred_element_type=F32)
    a = (g * _sigmoid(g) * u).astype(BF16)
    ffn = jnp.dot(a, wsd_ref[...], preferred_element_type=F32)

    slot_cols = _to_sublanes(slot_ref[0].astype(F32))
    slot16 = slot_cols.astype(I32).astype(jnp.int16)
    w16 = _to_sublanes(wts_ref[0]).astype(BF16)
    buf = gath.at[par]

    _wait_run_chunks(ys_hbm, buf, sem.at[par], n2_ref[i], n1_ref[i])

    def unsort(n_rows):
        col_id = lax.broadcasted_iota(I32, (MOE_TILE, n_rows), 1).astype(jnp.int16)
        take = jnp.zeros((MOE_TILE, n_rows), BF16)
        for kk in range(TOP_K):
            take = jnp.where(col_id == slot16[:, kk:kk + 1], w16[:, kk:kk + 1], take)
        routed[...] = jnp.dot(take, buf[0:n_rows, :], preferred_element_type=F32)

    _for_row_option(used_ref[i], unsort)
    ffn = ffn + routed[...]

    quarter = MOE_TILE // gate_ref.shape[0]
    gate = jnp.concatenate(
        [jnp.broadcast_to(gate_ref[q:q + 1, :], (quarter, D_MODEL)) for q in range(gate_ref.shape[0])], axis=0)
    x1 = jnp.where(is_prompt, x1p_ref[...], x1s_ref[...])
    out = x1 + gate * (_rms(ffn) * gains_ref[3:4, :])

    @pl.when(is_prompt)
    def _():
        outp_ref[...] = out

    @pl.when(jnp.logical_not(is_prompt))
    def _():
        outs_ref[...] = out


def _combine_call(tables, x1_p, x1_s, h2_p, h2_s, gate_groups, gains, slot_tiles, wts_tiles, ys,
                  ws_gate, ws_up, ws_down):
    n_prompt, n_sample = x1_p.shape[0], x1_s.shape[0]
    n_prompt_steps = n_prompt // MOE_TILE
    n_steps = n_prompt_steps + n_sample // MOE_TILE
    groups_per_tile = gate_groups.shape[0] // n_steps
    tok_p = lambda: pl.BlockSpec((MOE_TILE, D_MODEL), lambda i, *_: (jnp.minimum(i, n_prompt_steps - 1), 0))
    tok_s = lambda: pl.BlockSpec((MOE_TILE, D_MODEL), lambda i, *_: (jnp.maximum(i - n_prompt_steps, 0), 0))
    whole = lambda shape: pl.BlockSpec(shape, lambda i, *_: (0,) * len(shape))
    grid_spec = pltpu.PrefetchScalarGridSpec(
        num_scalar_prefetch=7,
        grid=(n_steps,),
        in_specs=[
            tok_p(), tok_s(), tok_p(), tok_s(),
            pl.BlockSpec((groups_per_tile, D_MODEL), lambda i, *_: (i, 0)),
            whole((4, D_MODEL)),
            pl.BlockSpec((1, ROUTE_ROWS, MOE_TILE), lambda i, *_: (i, 0, 0)),
            pl.BlockSpec((1, ROUTE_ROWS, MOE_TILE), lambda i, *_: (i, 0, 0)),
            pl.BlockSpec(memory_space=pl.ANY),
            whole((D_MODEL, D_SHARED)), whole((D_MODEL, D_SHARED)), whole((D_SHARED, D_MODEL)),
        ],
        out_specs=[tok_p(), tok_s()],
        scratch_shapes=[
            pltpu.VMEM((2, LOCAL_ROWS, D_MODEL), BF16),
            pltpu.VMEM((MOE_TILE, D_MODEL), F32),
            pltpu.SemaphoreType.DMA((2,)),
        ],
    )
    return pl.pallas_call(
        functools.partial(_combine_body, n_prompt_steps),
        grid_spec=grid_spec,
        out_shape=[jax.ShapeDtypeStruct((n_prompt, D_MODEL), F32),
                   jax.ShapeDtypeStruct((n_sample, D_MODEL), F32)],
        compiler_params=pltpu.CompilerParams(
            dimension_semantics=("arbitrary",), vmem_limit_bytes=VMEM_LIMIT),
        name="combine",
    )(*tables, x1_p, x1_s, h2_p, h2_s, gate_groups, gains, slot_tiles, wts_tiles, ys, ws_gate, ws_up, ws_down)


def kernel(x_prompt, x_sample, c_prompt, c_sample, state_pool, cache_k, cache_v, w_ada, b_ada, norm_gains,
           w_in, w_pool, pool_scale, attn_sinks, w_out, w_router, router_bias, w_gate, w_up, w_down,
           ws_gate, ws_up, ws_down):
    assert w_ada.shape[0] == 1, "single-layer kernel"
    bsz, seq, _ = x_prompt.shape
    dbsz, dseq, _ = x_sample.shape
    n_prompt, n_sample = bsz * seq, dbsz * dseq
    n_tok = n_prompt + n_sample
    assert dseq == CHUNK and seq % MOE_TILE == 0 and n_sample % MOE_TILE == 0

    w_in0, w_out0 = w_in[0], w_out[0]
    wq = w_in0[:, POOL_WIDTH:POOL_WIDTH + ATTN_WIDTH].reshape(D_MODEL, N_KV_HEADS, GQ, HEAD_DIM)
    wq = jnp.transpose(wq, (0, 2, 1, 3)).reshape(D_MODEL, ATTN_WIDTH)
    w_in_p = jnp.concatenate(
        [w_in0[:, :POOL_WIDTH], wq, w_in0[:, POOL_WIDTH + ATTN_WIDTH:]], axis=1).astype(BF16)
    wo = w_out0[POOL_WIDTH:].reshape(N_KV_HEADS, GQ, HEAD_DIM, D_MODEL)
    wo = jnp.transpose(wo, (1, 0, 2, 3)).reshape(ATTN_WIDTH, D_MODEL)
    w_out_p = jnp.concatenate([w_out0[:POOL_WIDTH], wo], axis=0).astype(BF16)
    w_pool_b = w_pool[0].astype(BF16)
    pscale = pool_scale[0].reshape(1, POOL_WIDTH)
    wr_t = w_router[0].T
    wr_hi = wr_t.astype(BF16)
    wr_lo = (wr_t - wr_hi.astype(F32)).astype(BF16)
    wr_split = jnp.concatenate([wr_hi, wr_lo], axis=0)
    rbias = router_bias[0].reshape(N_EXPERTS, 1)
    gains = norm_gains[0]
    sinks = attn_sinks[0]

    mod = _ada_call(jnp.concatenate([c_prompt, c_sample], axis=0), w_ada[0], b_ada[0])
    mod = mod.reshape(bsz + dbsz, 6, D_MODEL)

    zeros_pool = jnp.zeros((bsz, POOL_HIST_PAD, POOL_WIDTH), F32)
    zeros_kv = jnp.zeros((bsz, WINDOW, KV_WIDTH), F32)
    cnt0 = jnp.zeros((N_EXPERTS, LANES), F32)
    (x1_p, h2_p, pool_p, nk_p, nv_p, eidx_p, wts_p, rank_p, tcnt_p) = _front_call(
        x_prompt, mod[:bsz], gains, zeros_pool, zeros_kv, zeros_kv, cnt0, sinks,
        w_in_p, w_pool_b, pscale, w_out_p, wr_split, rbias, tile=MOE_TILE, has_cache=False)
    cnt_prompt = jnp.broadcast_to(jnp.sum(tcnt_p[:, 0, :], axis=0)[:, None], (N_EXPERTS, LANES))

    hist_pad = ATTN_KEYS - CHUNK - WINDOW
    pool_s0 = jnp.pad(state_pool[0], ((0, 0), (POOL_HIST_PAD - state_pool.shape[2], 0), (0, 0)))
    ck = jnp.pad(cache_k[0].reshape(dbsz, WINDOW, KV_WIDTH), ((0, 0), (hist_pad, 0), (0, 0)))
    cv = jnp.pad(cache_v[0].reshape(dbsz, WINDOW, KV_WIDTH), ((0, 0), (hist_pad, 0), (0, 0)))
    (x1_s, h2_s, pool_s, nk_s, nv_s, eidx_s, wts_s, rank_s, tcnt_s) = _front_call(
        x_sample, mod[bsz:], gains, pool_s0, ck, cv, cnt_prompt, sinks,
        w_in_p, w_pool_b, pscale, w_out_p, wr_split, rbias, tile=CHUNK, has_cache=True)

    def moe_tiles(route_p, route_s):
        per = MOE_TILE // CHUNK
        regrouped = jnp.transpose(route_s.reshape(-1, per, ROUTE_ROWS, CHUNK), (0, 2, 1, 3))
        return jnp.concatenate([route_p, regrouped.reshape(-1, ROUTE_ROWS, MOE_TILE)], axis=0)

    n_steps = n_tok // MOE_TILE
    cnt_td = jnp.concatenate(
        [tcnt_p[:, 0, :], tcnt_s[:, 0, :].reshape(-1, MOE_TILE // CHUNK, N_EXPERTS).sum(axis=1)],
        axis=0).astype(I32)
    c8 = (cnt_td + RUN_ROWS - 1) // RUN_ROWS * RUN_ROWS
    rows8 = jnp.sum(c8, axis=0)
    padded = (rows8 + EXPERT_ROWS - 1) // EXPERT_ROWS * EXPERT_ROWS
    pends = jnp.cumsum(padded)
    pstarts = pends - padded
    base = jnp.cumsum(cnt_td, axis=0) - cnt_td
    grow = pstarts[None, :] + jnp.cumsum(c8, axis=0) - c8
    lo8 = jnp.cumsum(c8, axis=1) - c8
    nch = c8 // RUN_ROWS
    nct = jnp.sum(nch, axis=1)
    off_tiles = (lo8 - base).astype(F32).reshape(n_steps, N_EXPERTS, 1)
    def flat_chunks(counts, max_n, local0, global0, rows):
        cum = jnp.cumsum(counts, axis=1)
        j = jnp.arange(max_n, dtype=I32)
        owner = jnp.sum(j[None, :, None] >= cum[:, None, :], axis=-1)
        owns = owner[:, :, None] == jnp.arange(N_EXPERTS, dtype=I32)[None, None, :]
        of_owner = lambda tab: jnp.sum(jnp.where(owns, tab[:, None, :], 0), axis=-1)
        within = rows * (j[None, :] - of_owner(cum - counts))
        return ((of_owner(local0) + within).reshape(-1).astype(I32),
                (of_owner(global0) + within).reshape(-1).astype(I32), cum[:, -1].astype(I32))

    n_double = nch // 2
    doubled = 2 * RUN_ROWS * n_double
    tables = (flat_chunks(n_double, MAX_DOUBLE_CHUNKS, lo8, grow, 2 * RUN_ROWS)
              + flat_chunks(nch - 2 * n_double, N_EXPERTS, lo8 + doubled, grow + doubled, RUN_ROWS)
              + ((nct * RUN_ROWS).astype(I32),))
    n_blocks = -(-(n_tok * TOP_K + n_steps * N_EXPERTS * (RUN_ROWS - 1) + N_EXPERTS * (EXPERT_ROWS - 1))
                 // EXPERT_ROWS)
    n_used = (pends[-1] // EXPERT_ROWS).astype(I32).reshape(1)
    zero_tables = ((pstarts + rows8).astype(I32), ((padded - rows8) // RUN_ROWS).astype(I32), n_used)

    h2_pf, h2_sf = h2_p.reshape(n_prompt, D_MODEL), h2_s.reshape(n_sample, D_MODEL)
    xs, slot_tiles = _dispatch_call(
        tables, zero_tables, h2_pf, h2_sf, moe_tiles(eidx_p, eidx_s), moe_tiles(rank_p, rank_s), off_tiles,
        n_blocks * EXPERT_ROWS)
    ys = _experts_call((pstarts // EXPERT_ROWS).astype(I32), (padded // EXPERT_ROWS).astype(I32),
                       xs, w_gate[0], w_up[0], w_down[0])
    gate_groups = jnp.concatenate(
        [jnp.repeat(mod[:bsz, 5], seq // GATE_GROUP, axis=0),
         jnp.repeat(mod[bsz:, 5], dseq // GATE_GROUP, axis=0)], axis=0)
    y_p, y_s = _combine_call(
        tables, x1_p.reshape(n_prompt, D_MODEL), x1_s.reshape(n_sample, D_MODEL), h2_pf, h2_sf,
        gate_groups, gains, slot_tiles, moe_tiles(wts_p, wts_s), ys,
        ws_gate[0].astype(BF16), ws_up[0].astype(BF16), ws_down[0].astype(BF16))

    n_hist = state_pool.shape[2]
    kv_shape = (1, -1, WINDOW, N_KV_HEADS, HEAD_DIM)
    return (y_p.reshape(bsz, seq, D_MODEL), y_s.reshape(dbsz, dseq, D_MODEL),
            pool_p[None, :, POOL_HIST_PAD - n_hist:], nk_p.reshape(kv_shape), nv_p.reshape(kv_shape),
            pool_s[None, :, POOL_HIST_PAD - n_hist:], nk_s.reshape(kv_shape), nv_s.reshape(kv_shape))
```

```python
import functools

import jax
import jax.numpy as jnp
from jax import lax
from jax.experimental import pallas as pl
from jax.experimental.pallas import tpu as pltpu

F32 = jnp.float32
BF16 = jnp.bfloat16
I32 = jnp.int32

D_MODEL = 1024
CHUNK = 64
POOL_WIDTH = 512
POOL_WINDOWS = (2, 4, 8, 16)
POOL_GROUP_W = 128
POOL_HIST_PAD = 16
HEAD_DIM = 64
N_HEADS = 8
N_KV_HEADS = 2
GQ = N_HEADS // N_KV_HEADS
ATTN_WIDTH = N_HEADS * HEAD_DIM
KV_WIDTH = N_KV_HEADS * HEAD_DIM
IN_WIDTH = POOL_WIDTH + ATTN_WIDTH + 2 * KV_WIDTH
WINDOW = 128
N_EXPERTS = 64
TOP_K = 6
N_EXPERT_GROUPS = 8
GROUP_SIZE = N_EXPERTS // N_EXPERT_GROUPS
TOPK_GROUPS = 4
D_EXPERT = 256
D_SHARED = 256
ROUTED_SCALE = 2.5
NORM_EPS = 1e-6
NEG_BIG = -1e30

LANES = 128
SUBLANES = 8
ATTN_KEYS = 2 * LANES
ROUTE_ROWS = 8
EXPERT_ROWS = 256
ITEM_BLOCKS = (4, 2, 1)
BIG_ROWS = ITEM_BLOCKS[0] * EXPERT_ROWS
MOE_TILE = 256
GATE_GROUP = MOE_TILE // SUBLANES
RUN_ROWS = 2 * SUBLANES
LOCAL_ROWS = 2560
LOCAL_ROW_OPTIONS = (2048, 2176, LOCAL_ROWS)
MAX_DOUBLE_CHUNKS = LOCAL_ROWS // (2 * RUN_ROWS)
ROW_W = D_MODEL
VMEM_LIMIT = 56 * 1024 * 1024

assert TOP_K * MOE_TILE + N_EXPERTS * (RUN_ROWS - 1) <= LOCAL_ROWS

_NT = (((1,), (1,)), ((), ()))


def _rms(v):
    return v * lax.rsqrt(jnp.mean(v * v, axis=-1, keepdims=True) + NORM_EPS)


def _sigmoid(v):
    return 1.0 / (1.0 + jnp.exp(-v))


def _split3(v):
    hi = v.astype(BF16)
    r1 = v - hi.astype(F32)
    mid = r1.astype(BF16)
    lo = (r1 - mid.astype(F32)).astype(BF16)
    return hi, mid, lo


def _to_sublanes(rows):
    n = rows.shape[1]
    hi, mid, lo = _split3(rows)
    eye = (lax.broadcasted_iota(I32, (n, n), 0) == lax.broadcasted_iota(I32, (n, n), 1)).astype(BF16)
    return (lax.dot_general(eye, hi, _NT, preferred_element_type=F32)
            + lax.dot_general(eye, mid, _NT, preferred_element_type=F32)
            + lax.dot_general(eye, lo, _NT, preferred_element_type=F32))


def _ada_body(c_ref, w_ref, b_ref, o_ref):
    c = c_ref[...]
    s = c * _sigmoid(c)
    o_ref[...] = jnp.dot(s, w_ref[...], preferred_element_type=F32) + b_ref[...]


def _ada_call(c_all, w_ada, b_ada):
    nb = c_all.shape[0]
    n_out = w_ada.shape[1]
    tile = D_MODEL
    return pl.pallas_call(
        _ada_body,
        grid=(n_out // tile,),
        in_specs=[
            pl.BlockSpec((nb, D_MODEL), lambda j: (0, 0)),
            pl.BlockSpec((D_MODEL, tile), lambda j: (0, j)),
            pl.BlockSpec((1, tile), lambda j: (0, j)),
        ],
        out_specs=pl.BlockSpec((nb, tile), lambda j: (0, j)),
        out_shape=jax.ShapeDtypeStruct((nb, n_out), F32),
        compiler_params=pltpu.CompilerParams(dimension_semantics=("arbitrary",)),
        name="ada",
    )(c_all, w_ada, b_ada.reshape(1, n_out))


def _front_body(tile, has_cache,
                sinks_ref, x_ref, mod_ref, gains_ref, hp_ref, hk_ref, hv_ref, cnt0_ref,
                win_ref, wpool_ref, pscale_ref, wout_ref, wrt_ref, rbias_ref,
                x1_ref, h2_ref, npool_ref, nk_ref, nv_ref, eidx_ref, wts_ref, rank_ref, tcnt_ref,
                ubuf, khist, vhist, run):
    b = pl.program_id(0)
    i = pl.program_id(1)
    sub_q = min(tile, LANES)
    n_sub = tile // sub_q
    hist_keys = ATTN_KEYS - sub_q
    n_keys = hist_keys + tile

    @pl.when(i == 0)
    def _():
        ubuf[0:POOL_HIST_PAD, :] = hp_ref[0]
        khist[...] = hk_ref[0]
        vhist[...] = hv_ref[0]

    @pl.when((b == 0) & (i == 0))
    def _():
        run[...] = cnt0_ref[...]

    mod = mod_ref[0]
    gains = gains_ref[...]
    xt = x_ref[0]

    h = _rms(xt) * gains[0:1] * (1.0 + mod[1:2]) + mod[0:1]
    proj = jnp.dot(h.astype(BF16), win_ref[...], preferred_element_type=F32)
    u = proj[:, :POOL_WIDTH]
    o_k = POOL_WIDTH + ATTN_WIDTH
    k_new = proj[:, o_k:o_k + KV_WIDTH]
    v_new = proj[:, o_k + KV_WIDTH:]

    kw = jnp.concatenate([khist[...], k_new], axis=0)
    vw = jnp.concatenate([vhist[...], v_new], axis=0)
    kwb = kw.astype(BF16)
    vwb = vw.astype(BF16)
    qrow = lax.broadcasted_iota(I32, (sub_q, ATTN_KEYS), 0)
    kpos = lax.broadcasted_iota(I32, (sub_q, ATTN_KEYS), 1) - hist_keys
    qchunk = jnp.right_shift(qrow, 6)
    vis_band = (kpos >= CHUNK * (qchunk - 2)) & (kpos < CHUNK * (qchunk + 1))
    side0 = lax.broadcasted_iota(I32, (sub_q, LANES), 1) < HEAD_DIM
    attn_rows = []
    for r in range(n_sub):
        vis = vis_band
        if not has_cache:
            vis = vis & (kpos + (i * tile + r * sub_q) >= 0)
        kb = kwb[r * sub_q:r * sub_q + ATTN_KEYS]
        vb = vwb[r * sub_q:r * sub_q + ATTN_KEYS]
        blocks = []
        for j in range(GQ):
            qp = proj[r * sub_q:(r + 1) * sub_q, POOL_WIDTH + LANES * j:POOL_WIDTH + LANES * (j + 1)]
            qp = qp * (HEAD_DIM ** -0.5)
            outs = []
            for s in range(N_KV_HEADS):
                keep = side0 if s == 0 else jnp.logical_not(side0)
                qm = jnp.where(keep, qp, 0.0).astype(BF16)
                sc = lax.dot_general(qm, kb, _NT, preferred_element_type=F32)
                sc = jnp.where(vis, sc, NEG_BIG)
                sink = sinks_ref[j + GQ * s]
                m = jnp.maximum(jnp.max(sc, axis=-1, keepdims=True), sink)
                p = jnp.exp(sc - m)
                den = jnp.sum(p, axis=-1, keepdims=True) + jnp.exp(sink - m)
                p = (p / den).astype(BF16)
                outs.append(jnp.dot(p, vb, preferred_element_type=F32))
            blocks.append(jnp.where(side0, outs[0], outs[1]))
        attn_rows.append(jnp.concatenate(blocks, axis=-1))
    attn = attn_rows[0] if n_sub == 1 else jnp.concatenate(attn_rows, axis=0)

    ubuf[POOL_HIST_PAD:POOL_HIST_PAD + tile, :] = u
    if has_cache:
        seen = None
    else:
        seen = (lax.broadcasted_iota(I32, (tile, 1), 0) + i * tile + 1).astype(F32)
    pool_blocks = []
    for g, w in enumerate(POOL_WINDOWS):
        cols = slice(POOL_GROUP_W * g, POOL_GROUP_W * (g + 1))
        acc = u[:, cols]
        for s in range(1, w):
            acc = acc + ubuf[POOL_HIST_PAD - s:POOL_HIST_PAD - s + tile, cols]
        cnt = float(w) if seen is None else jnp.minimum(seen, float(w))
        dlt = acc / cnt - u[:, cols]
        pool_blocks.append(jnp.dot(dlt.astype(BF16), wpool_ref[g], preferred_element_type=F32))
    pool = jnp.concatenate(pool_blocks, axis=-1) * pscale_ref[...]

    mixin = jnp.concatenate([pool, attn], axis=-1).astype(BF16)
    mix = jnp.dot(mixin, wout_ref[...], preferred_element_type=F32)
    x1 = xt + mod[2:3] * (_rms(mix) * gains[1:2])
    x1_ref[0] = x1

    new_hist = ubuf[tile:tile + POOL_HIST_PAD, :]
    ubuf[0:POOL_HIST_PAD, :] = new_hist
    npool_ref[0] = new_hist
    khist[...] = kw[tile:, :]
    vhist[...] = vw[tile:, :]
    nk_ref[0] = kw[n_keys - WINDOW:, :]
    nv_ref[0] = vw[n_keys - WINDOW:, :]

    h2f = _rms(x1) * gains[2:3] * (1.0 + mod[4:5]) + mod[3:4]
    h2hi = h2f.astype(BF16)
    h2_ref[0] = h2hi
    h2lo = (h2f - h2hi.astype(F32)).astype(BF16)
    wrt = wrt_ref[...]
    part = lax.dot_general(wrt, h2hi, _NT, preferred_element_type=F32)
    logits = (part[:N_EXPERTS] + part[N_EXPERTS:]
              + lax.dot_general(wrt[:N_EXPERTS], h2lo, _NT, preferred_element_type=F32))
    scores = _sigmoid(logits)
    sel = scores + rbias_ref[...]

    sub_g = lax.broadcasted_iota(I32, (GROUP_SIZE, tile), 0).astype(F32)
    gscore = jnp.zeros((N_EXPERT_GROUPS, tile), F32)
    for gi in range(N_EXPERT_GROUPS):
        blk = sel[GROUP_SIZE * gi:GROUP_SIZE * (gi + 1), :]
        m1 = jnp.max(blk, axis=0, keepdims=True)
        i1 = jnp.min(jnp.where(blk == m1, sub_g, float(GROUP_SIZE)), axis=0, keepdims=True)
        m2 = jnp.max(jnp.where(sub_g == i1, -jnp.inf, blk), axis=0, keepdims=True)
        gscore = jnp.where(sub_g == gi, m1 + m2, gscore)
    chosen = jnp.zeros((N_EXPERT_GROUPS, tile), F32)
    for _ in range(TOPK_GROUPS):
        m = jnp.max(gscore, axis=0, keepdims=True)
        idx = jnp.min(jnp.where(gscore == m, sub_g, float(N_EXPERT_GROUPS)), axis=0, keepdims=True)
        pick = sub_g == idx
        chosen = jnp.where(pick, 1.0, chosen)
        gscore = jnp.where(pick, -jnp.inf, gscore)
    emask = jnp.concatenate(
        [jnp.broadcast_to(chosen[gi:gi + 1, :], (GROUP_SIZE, tile)) for gi in range(N_EXPERT_GROUPS)], axis=0)
    selm = jnp.where(emask > 0.0, sel, -jnp.inf)

    sub_e = lax.broadcasted_iota(I32, (N_EXPERTS, tile), 0).astype(F32)
    picks, idxs, raw_w = [], [], []
    for _ in range(TOP_K):
        m = jnp.max(selm, axis=0, keepdims=True)
        idx = jnp.min(jnp.where(selm == m, sub_e, float(N_EXPERTS)), axis=0, keepdims=True)
        pick = sub_e == idx
        raw_w.append(jnp.sum(jnp.where(pick, scores, 0.0), axis=0, keepdims=True))
        selm = jnp.where(pick, -jnp.inf, selm)
        picks.append(pick)
        idxs.append(idx)
    wsum = raw_w[0]
    for kk in range(1, TOP_K):
        wsum = wsum + raw_w[kk]

    onehot = jnp.zeros((N_EXPERTS, tile), F32)
    for kk in range(TOP_K):
        onehot = jnp.where(picks[kk], 1.0, onehot)
    onehot_b = onehot.astype(BF16)
    tri = (lax.broadcasted_iota(I32, (tile, tile), 0) < lax.broadcasted_iota(I32, (tile, tile), 1)).astype(BF16)
    before = jnp.dot(onehot_b, tri, preferred_element_type=F32) + run[:, 0:1]
    sub_r = lax.broadcasted_iota(I32, (ROUTE_ROWS, tile), 0)
    eidx_o = jnp.zeros((ROUTE_ROWS, tile), I32)
    wts_o = jnp.zeros((ROUTE_ROWS, tile), F32)
    rank_o = jnp.zeros((ROUTE_ROWS, tile), I32)
    for kk in range(TOP_K):
        rk = jnp.sum(jnp.where(picks[kk], before, 0.0), axis=0, keepdims=True).astype(I32)
        eidx_o = jnp.where(sub_r == kk, idxs[kk].astype(I32), eidx_o)
        wts_o = jnp.where(sub_r == kk, raw_w[kk] / wsum * ROUTED_SCALE, wts_o)
        rank_o = jnp.where(sub_r == kk, rk, rank_o)
    eidx_ref[0] = eidx_o
    wts_ref[0] = wts_o
    rank_ref[0] = rank_o
    tcnt_ref[0] = lax.dot_general(jnp.ones((ROUTE_ROWS, tile), BF16), onehot_b, _NT,
                                  preferred_element_type=F32)
    run[...] = run[...] + jnp.sum(onehot, axis=1, keepdims=True)


def _front_call(x, mod, gains, hist_pool, hist_k, hist_v, cnt0, sinks,
                w_in, w_pool, pool_scale, w_out, wr_t, rbias, *, tile, has_cache):
    bsz, seq, _ = x.shape
    n_tiles = seq // tile
    hist_keys = hist_k.shape[1]
    assert hist_keys == ATTN_KEYS - min(tile, LANES)
    body = functools.partial(_front_body, tile, has_cache)
    whole = lambda shape: pl.BlockSpec(shape, lambda b, i: (0,) * len(shape))
    per_b = lambda shape: pl.BlockSpec((1,) + shape, lambda b, i: (b,) + (0,) * len(shape))
    route = pl.BlockSpec((1, ROUTE_ROWS, tile), lambda b, i: (b * n_tiles + i, 0, 0))
    out_shape = [
        jax.ShapeDtypeStruct((bsz, seq, D_MODEL), F32),
        jax.ShapeDtypeStruct((bsz, seq, D_MODEL), BF16),
        jax.ShapeDtypeStruct((bsz, POOL_HIST_PAD, POOL_WIDTH), F32),
        jax.ShapeDtypeStruct((bsz, WINDOW, KV_WIDTH), F32),
        jax.ShapeDtypeStruct((bsz, WINDOW, KV_WIDTH), F32),
        jax.ShapeDtypeStruct((bsz * n_tiles, ROUTE_ROWS, tile), I32),
        jax.ShapeDtypeStruct((bsz * n_tiles, ROUTE_ROWS, tile), F32),
        jax.ShapeDtypeStruct((bsz * n_tiles, ROUTE_ROWS, tile), I32),
        jax.ShapeDtypeStruct((bsz * n_tiles, ROUTE_ROWS, N_EXPERTS), F32),
    ]
    return pl.pallas_call(
        body,
        grid=(bsz, n_tiles),
        in_specs=[
            pl.BlockSpec(memory_space=pltpu.SMEM),
            pl.BlockSpec((1, tile, D_MODEL), lambda b, i: (b, i, 0)),
            per_b((6, D_MODEL)),
            whole((4, D_MODEL)),
            per_b((POOL_HIST_PAD, POOL_WIDTH)),
            per_b((hist_keys, KV_WIDTH)),
            per_b((hist_keys, KV_WIDTH)),
            whole((N_EXPERTS, LANES)),
            whole((D_MODEL, IN_WIDTH)),
            whole((len(POOL_WINDOWS), POOL_GROUP_W, POOL_GROUP_W)),
            whole((1, POOL_WIDTH)),
            whole((D_MODEL, D_MODEL)),
            whole((2 * N_EXPERTS, D_MODEL)),
            whole((N_EXPERTS, 1)),
        ],
        out_specs=[
            pl.BlockSpec((1, tile, D_MODEL), lambda b, i: (b, i, 0)),
            pl.BlockSpec((1, tile, D_MODEL), lambda b, i: (b, i, 0)),
            per_b((POOL_HIST_PAD, POOL_WIDTH)),
            per_b((WINDOW, KV_WIDTH)),
            per_b((WINDOW, KV_WIDTH)),
            route, route, route,
            pl.BlockSpec((1, ROUTE_ROWS, N_EXPERTS), lambda b, i: (b * n_tiles + i, 0, 0)),
        ],
        out_shape=out_shape,
        scratch_shapes=[
            pltpu.VMEM((tile + POOL_HIST_PAD, POOL_WIDTH), F32),
            pltpu.VMEM((hist_keys, KV_WIDTH), F32),
            pltpu.VMEM((hist_keys, KV_WIDTH), F32),
            pltpu.VMEM((N_EXPERTS, LANES), F32),
        ],
        compiler_params=pltpu.CompilerParams(
            dimension_semantics=("arbitrary", "arbitrary"), vmem_limit_bytes=VMEM_LIMIT),
        name="front_cached" if has_cache else "front_prompt",
    )(sinks, x, mod, gains, hist_pool, hist_k, hist_v, cnt0,
      w_in, w_pool, pool_scale, w_out, wr_t, rbias)


def _run_copy(src, dst, s_row, d_row, sem, rows=RUN_ROWS):
    return pltpu.make_async_copy(src.at[pl.ds(s_row, rows)], dst.at[pl.ds(d_row, rows)], sem)


def _for_each_run_chunk(step, chunk_tables, fn):
    lrow2_ref, grow2_ref, n2_ref, lrow1_ref, grow1_ref, n1_ref = chunk_tables

    def per_double(j, carry):
        idx = step * MAX_DOUBLE_CHUNKS + j
        fn(pl.multiple_of(lrow2_ref[idx], RUN_ROWS), pl.multiple_of(grow2_ref[idx], RUN_ROWS), 2 * RUN_ROWS)
        return carry

    def per_single(j, carry):
        idx = step * N_EXPERTS + j
        fn(pl.multiple_of(lrow1_ref[idx], RUN_ROWS), pl.multiple_of(grow1_ref[idx], RUN_ROWS), RUN_ROWS)
        return carry

    lax.fori_loop(0, n2_ref[step], per_double, 0)
    lax.fori_loop(0, n1_ref[step], per_single, 0)


def _wait_run_chunks(src, dst, sem, n_double, n_single):
    def one_double(c, carry):
        _run_copy(src, dst, 0, 0, sem, 2 * RUN_ROWS).wait()
        return carry

    def one_single(c, carry):
        _run_copy(src, dst, 0, 0, sem).wait()
        return carry

    lax.fori_loop(0, n_double, one_double, 0)
    lax.fori_loop(0, n_single, one_single, 0)


def _for_row_option(used_rows, fn):
    lower = 0
    for n_rows in LOCAL_ROW_OPTIONS:
        @pl.when((used_rows > lower) & (used_rows <= n_rows))
        def _(n_rows=n_rows):
            fn(n_rows)
        lower = n_rows


def _dispatch_body(n_prompt_steps, lrow2_ref, grow2_ref, n2_ref, lrow1_ref, grow1_ref, n1_ref, used_ref,
                   zrow_ref, znch_ref, nused_ref,
                   h2p_ref, h2s_ref, eidx_ref, rank_ref, off_ref, xs_out, slot_out, loc, zrows, sem):
    chunk_tables = (lrow2_ref, grow2_ref, n2_ref, lrow1_ref, grow1_ref, n1_ref)
    i = pl.program_id(0)
    n_steps = pl.num_programs(0)
    par = lax.rem(i, 2)
    h2 = jnp.where(i < n_prompt_steps, h2p_ref[...], h2s_ref[...])
    eidx = eidx_ref[0]
    rank = rank_ref[0]
    off = off_ref[0]
    expert_id = lax.broadcasted_iota(I32, (N_EXPERTS, MOE_TILE), 0)
    sub_r = lax.broadcasted_iota(I32, (ROUTE_ROWS, MOE_TILE), 0)
    slots = []
    slot_o = jnp.zeros((ROUTE_ROWS, MOE_TILE), I32)
    for kk in range(TOP_K):
        mine = jnp.sum(jnp.where(expert_id == eidx[kk:kk + 1, :], off, 0.0), axis=0, keepdims=True)
        slots.append(rank[kk:kk + 1, :] + mine.astype(I32))
        slot_o = jnp.where(sub_r == kk, slots[kk], slot_o)
    slot_out[0] = slot_o
    buf = loc.at[par]

    def sort_tile(n_rows):
        row_id = lax.broadcasted_iota(I32, (n_rows, MOE_TILE), 0).astype(jnp.int16)
        sel = jnp.zeros((n_rows, MOE_TILE), BF16)
        for kk in range(TOP_K):
            sel = jnp.where(row_id == slots[kk].astype(jnp.int16), jnp.ones((), BF16), sel)
        buf[0:n_rows, :] = jnp.dot(sel, h2, preferred_element_type=F32).astype(BF16)

    _for_row_option(used_ref[i], sort_tile)

    @pl.when(i > 0)
    def _():
        _wait_run_chunks(buf, xs_out, sem, n2_ref[i - 1], n1_ref[i - 1])

    _for_each_run_chunk(i, chunk_tables,
                        lambda lrow, grow, rows: _run_copy(buf, xs_out, lrow, grow, sem, rows).start())

    @pl.when(i == n_steps - 1)
    def _():
        zrows[...] = jnp.zeros_like(zrows)

        def per_expert(e, total):
            def per_chunk(c, carry):
                _run_copy(zrows, xs_out, 0, pl.multiple_of(zrow_ref[e] + RUN_ROWS * c, RUN_ROWS), sem).start()
                return carry
            lax.fori_loop(0, znch_ref[e], per_chunk, 0)
            return total + znch_ref[e]

        n_zero = lax.fori_loop(0, N_EXPERTS, per_expert, 0)
        _wait_run_chunks(buf, xs_out, sem, n2_ref[i], n1_ref[i] + n_zero)

        def block_copy(blk):
            return pltpu.make_async_copy(
                zrows, xs_out.at[pl.ds(pl.multiple_of(blk * EXPERT_ROWS, EXPERT_ROWS), EXPERT_ROWS)], sem)

        n_blocks = xs_out.shape[0] // EXPERT_ROWS

        def start_block(blk, carry):
            block_copy(blk).start()
            return carry

        def wait_block(blk, carry):
            block_copy(blk).wait()
            return carry

        lax.fori_loop(nused_ref[0], n_blocks, start_block, 0)
        lax.fori_loop(nused_ref[0], n_blocks, wait_block, 0)


def _dispatch_call(tables, zero_tables, h2_p, h2_s, eidx_tiles, rank_tiles, off_tiles, n_rows):
    n_prompt_steps = h2_p.shape[0] // MOE_TILE
    n_steps = n_prompt_steps + h2_s.shape[0] // MOE_TILE
    route = lambda: pl.BlockSpec((1, ROUTE_ROWS, MOE_TILE), lambda i, *_: (i, 0, 0))
    grid_spec = pltpu.PrefetchScalarGridSpec(
        num_scalar_prefetch=10,
        grid=(n_steps,),
        in_specs=[
            pl.BlockSpec((MOE_TILE, D_MODEL), lambda i, *_: (jnp.minimum(i, n_prompt_steps - 1), 0)),
            pl.BlockSpec((MOE_TILE, D_MODEL), lambda i, *_: (jnp.maximum(i - n_prompt_steps, 0), 0)),
            route(), route(),
            pl.BlockSpec((1, N_EXPERTS, 1), lambda i, *_: (i, 0, 0)),
        ],
        out_specs=[pl.BlockSpec(memory_space=pl.ANY), route()],
        scratch_shapes=[
            pltpu.VMEM((2, LOCAL_ROWS, ROW_W), BF16),
            pltpu.VMEM((EXPERT_ROWS, ROW_W), BF16),
            pltpu.SemaphoreType.DMA,
        ],
    )
    return pl.pallas_call(
        functools.partial(_dispatch_body, n_prompt_steps),
        grid_spec=grid_spec,
        out_shape=[jax.ShapeDtypeStruct((n_rows, ROW_W), BF16),
                   jax.ShapeDtypeStruct((n_steps, ROUTE_ROWS, MOE_TILE), I32)],
        compiler_params=pltpu.CompilerParams(
            dimension_semantics=("arbitrary",), vmem_limit_bytes=VMEM_LIMIT),
        name="dispatch",
    )(*tables, *zero_tables, h2_p, h2_s, eidx_tiles, rank_tiles, off_tiles)


def _experts_body(first_ref, nblk_ref, xs_hbm, wg_ref, wu_ref, wd_ref, ys_hbm,
                  xbuf, ybuf, wgb, wub, wdb, isem, osem):
    e = pl.program_id(0)
    n_exp = pl.num_programs(0)

    def items_of(expert):
        rem = nblk_ref[expert]
        row = first_ref[expert] * EXPERT_ROWS
        done = 0
        lows, highs, bases = [], [], []
        for blocks in ITEM_BLOCKS:
            n = rem // blocks
            lows.append(done)
            bases.append(row)
            done = done + n
            row = row + n * (blocks * EXPERT_ROWS)
            rem = rem - n * blocks
            highs.append(done)
        return (lows, highs, bases), done

    def for_item(items, t, slot, action):
        lows, highs, bases = items
        for c, blocks in enumerate(ITEM_BLOCKS):
            rows = blocks * EXPERT_ROWS

            @pl.when((t >= lows[c]) & (t < highs[c]))
            def _(c=c, rows=rows):
                action(pl.multiple_of(bases[c] + (t - lows[c]) * rows, EXPERT_ROWS), rows, slot)

    def in_copy(row0, rows, slot):
        return pltpu.make_async_copy(xs_hbm.at[pl.ds(row0, rows)], xbuf.at[slot, pl.ds(0, rows)], isem.at[slot])

    def out_copy(row0, rows, slot):
        return pltpu.make_async_copy(ybuf.at[slot, pl.ds(0, rows)], ys_hbm.at[pl.ds(row0, rows)], osem.at[slot])

    start_in = lambda row0, rows, slot: in_copy(row0, rows, slot).start()
    wait_in = lambda row0, rows, slot: in_copy(row0, rows, slot).wait()
    start_out = lambda row0, rows, slot: out_copy(row0, rows, slot).start()
    wait_out = lambda row0, rows, slot: out_copy(row0, rows, slot).wait()
    items, n_items = items_of(e)

    @pl.when((e == 0) & (n_items > 0))
    def _():
        for_item(items, 0, 0, start_in)

    def compute(row0, rows, slot):
        del row0
        x = xbuf[slot, 0:rows, :]
        g = jnp.dot(x, wgb[...], preferred_element_type=F32)
        u = jnp.dot(x, wub[...], preferred_element_type=F32)
        a = (g * _sigmoid(g) * u).astype(BF16)
        ybuf[slot, 0:rows, :] = jnp.dot(a, wdb[...], preferred_element_type=F32).astype(BF16)

    prev_items, prev_n = items_of(jnp.maximum(e - 1, 0))

    def wait_previous_expert():
        @pl.when((e > 0) & (prev_n > 0))
        def _():
            for_item(prev_items, prev_n - 1, lax.rem(prev_n - 1, 2), wait_out)

    @pl.when(n_items == 0)
    def _():
        wait_previous_expert()

    @pl.when(n_items > 0)
    def _():
        wgb[...] = wg_ref[0].astype(BF16)
        wub[...] = wu_ref[0].astype(BF16)
        wdb[...] = wd_ref[0].astype(BF16)
        wait_previous_expert()

        def one_item(t, carry):
            slot = lax.rem(t, 2)

            @pl.when(t + 1 < n_items)
            def _():
                for_item(items, t + 1, 1 - slot, start_in)

            for_item(items, t, slot, wait_in)
            for_item(items, t, slot, compute)

            @pl.when(t >= 1)
            def _():
                for_item(items, t - 1, 1 - slot, wait_out)

            for_item(items, t, slot, start_out)
            return carry

        lax.fori_loop(0, n_items, one_item, 0)

        @pl.when(e == n_exp - 1)
        def _():
            for_item(items, n_items - 1, lax.rem(n_items - 1, 2), wait_out)

    nxt = jnp.minimum(e + 1, n_exp - 1)
    nxt_items, nxt_n = items_of(nxt)

    @pl.when((e + 1 < n_exp) & (nxt_n > 0))
    def _():
        for_item(nxt_items, 0, 0, start_in)


def _experts_call(first_block, n_expert_blocks, xs, w_gate, w_up, w_down):
    grid_spec = pltpu.PrefetchScalarGridSpec(
        num_scalar_prefetch=2,
        grid=(N_EXPERTS,),
        in_specs=[
            pl.BlockSpec(memory_space=pl.ANY),
            pl.BlockSpec((1, D_MODEL, D_EXPERT), lambda e, *_: (e, 0, 0)),
            pl.BlockSpec((1, D_MODEL, D_EXPERT), lambda e, *_: (e, 0, 0)),
            pl.BlockSpec((1, D_EXPERT, D_MODEL), lambda e, *_: (e, 0, 0)),
        ],
        out_specs=pl.BlockSpec(memory_space=pl.ANY),
        scratch_shapes=[
            pltpu.VMEM((2, BIG_ROWS, ROW_W), BF16),
            pltpu.VMEM((2, BIG_ROWS, D_MODEL), BF16),
            pltpu.VMEM((D_MODEL, D_EXPERT), BF16),
            pltpu.VMEM((D_MODEL, D_EXPERT), BF16),
            pltpu.VMEM((D_EXPERT, D_MODEL), BF16),
            pltpu.SemaphoreType.DMA((2,)),
            pltpu.SemaphoreType.DMA((2,)),
        ],
    )
    return pl.pallas_call(
        _experts_body,
        grid_spec=grid_spec,
        out_shape=jax.ShapeDtypeStruct(xs.shape, xs.dtype),
        input_output_aliases={2: 0},
        compiler_params=pltpu.CompilerParams(
            dimension_semantics=("arbitrary",), vmem_limit_bytes=VMEM_LIMIT),
        name="experts",
    )(first_block, n_expert_blocks, xs, w_gate, w_up, w_down)


def _combine_body(n_prompt_steps, lrow2_ref, grow2_ref, n2_ref, lrow1_ref, grow1_ref, n1_ref, used_ref,
                  x1p_ref, x1s_ref, h2p_ref, h2s_ref, gate_ref, gains_ref, slot_ref, wts_ref, ys_hbm,
                  wsg_ref, wsu_ref, wsd_ref, outp_ref, outs_ref, gath, routed, sem):
    chunk_tables = (lrow2_ref, grow2_ref, n2_ref, lrow1_ref, grow1_ref, n1_ref)
    i = pl.program_id(0)
    n_steps = pl.num_programs(0)
    par = lax.rem(i, 2)

    def fetch(step, slot):
        buf = gath.at[slot]
        _for_each_run_chunk(
            step, chunk_tables,
            lambda lrow, grow, rows: _run_copy(ys_hbm, buf, grow, lrow, sem.at[slot], rows).start())

    @pl.when(i == 0)
    def _():
        gath[...] = jnp.zeros_like(gath)
        fetch(0, 0)

    @pl.when(i + 1 < n_steps)
    def _():
        fetch(i + 1, 1 - par)

    is_prompt = i < n_prompt_steps
    h2 = jnp.where(is_prompt, h2p_ref[...], h2s_ref[...])
    g = jnp.dot(h2, wsg_ref[...], preferred_element_type=F32)
    u = jnp.dot(h2, wsu_ref[...], preferred_element_type=F32)
    a = (g * _sigmoid(g) * u).astype(BF16)
    ffn = jnp.dot(a, wsd_ref[...], preferred_element_type=F32)

    slot_cols = _to_sublanes(slot_ref[0].astype(F32))
    slot16 = slot_cols.astype(I32).astype(jnp.int16)
    w16 = _to_sublanes(wts_ref[0]).astype(BF16)
    buf = gath.at[par]

    _wait_run_chunks(ys_hbm, buf, sem.at[par], n2_ref[i], n1_ref[i])

    def unsort(n_rows):
        col_id = lax.broadcasted_iota(I32, (MOE_TILE, n_rows), 1).astype(jnp.int16)
        take = jnp.zeros((MOE_TILE, n_rows), BF16)
        for kk in range(TOP_K):
            take = jnp.where(col_id == slot16[:, kk:kk + 1], w16[:, kk:kk + 1], take)
        routed[...] = jnp.dot(take, buf[0:n_rows, :], preferred_element_type=F32)

    _for_row_option(used_ref[i], unsort)
    ffn = ffn + routed[...]

    quarter = MOE_TILE // gate_ref.shape[0]
    gate = jnp.concatenate(
        [jnp.broadcast_to(gate_ref[q:q + 1, :], (quarter, D_MODEL)) for q in range(gate_ref.shape[0])], axis=0)
    x1 = jnp.where(is_prompt, x1p_ref[...], x1s_ref[...])
    out = x1 + gate * (_rms(ffn) * gains_ref[3:4, :])

    @pl.when(is_prompt)
    def _():
        outp_ref[...] = out

    @pl.when(jnp.logical_not(is_prompt))
    def _():
        outs_ref[...] = out


def _combine_call(tables, x1_p, x1_s, h2_p, h2_s, gate_groups, gains, slot_tiles, wts_tiles, ys,
                  ws_gate, ws_up, ws_down):
    n_prompt, n_sample = x1_p.shape[0], x1_s.shape[0]
    n_prompt_steps = n_prompt // MOE_TILE
    n_steps = n_prompt_steps + n_sample // MOE_TILE
    groups_per_tile = gate_groups.shape[0] // n_steps
    tok_p = lambda: pl.BlockSpec((MOE_TILE, D_MODEL), lambda i, *_: (jnp.minimum(i, n_prompt_steps - 1), 0))
    tok_s = lambda: pl.BlockSpec((MOE_TILE, D_MODEL), lambda i, *_: (jnp.maximum(i - n_prompt_steps, 0), 0))
    whole = lambda shape: pl.BlockSpec(shape, lambda i, *_: (0,) * len(shape))
    grid_spec = pltpu.PrefetchScalarGridSpec(
        num_scalar_prefetch=7,
        grid=(n_steps,),
        in_specs=[
            tok_p(), tok_s(), tok_p(), tok_s(),
            pl.BlockSpec((groups_per_tile, D_MODEL), lambda i, *_: (i, 0)),
            whole((4, D_MODEL)),
            pl.BlockSpec((1, ROUTE_ROWS, MOE_TILE), lambda i, *_: (i, 0, 0)),
            pl.BlockSpec((1, ROUTE_ROWS, MOE_TILE), lambda i, *_: (i, 0, 0)),
            pl.BlockSpec(memory_space=pl.ANY),
            whole((D_MODEL, D_SHARED)), whole((D_MODEL, D_SHARED)), whole((D_SHARED, D_MODEL)),
        ],
        out_specs=[tok_p(), tok_s()],
        scratch_shapes=[
            pltpu.VMEM((2, LOCAL_ROWS, D_MODEL), BF16),
            pltpu.VMEM((MOE_TILE, D_MODEL), F32),
            pltpu.SemaphoreType.DMA((2,)),
        ],
    )
    return pl.pallas_call(
        functools.partial(_combine_body, n_prompt_steps),
        grid_spec=grid_spec,
        out_shape=[jax.ShapeDtypeStruct((n_prompt, D_MODEL), F32),
                   jax.ShapeDtypeStruct((n_sample, D_MODEL), F32)],
        compiler_params=pltpu.CompilerParams(
            dimension_semantics=("arbitrary",), vmem_limit_bytes=VMEM_LIMIT),
        name="combine",
    )(*tables, x1_p, x1_s, h2_p, h2_s, gate_groups, gains, slot_tiles, wts_tiles, ys, ws_gate, ws_up, ws_down)


def kernel(x_prompt, x_sample, c_prompt, c_sample, state_pool, cache_k, cache_v, w_ada, b_ada, norm_gains,
           w_in, w_pool, pool_scale, attn_sinks, w_out, w_router, router_bias, w_gate, w_up, w_down,
           ws_gate, ws_up, ws_down):
    assert w_ada.shape[0] == 1, "single-layer kernel"
    bsz, seq, _ = x_prompt.shape
    dbsz, dseq, _ = x_sample.shape
    n_prompt, n_sample = bsz * seq, dbsz * dseq
    n_tok = n_prompt + n_sample
    assert dseq == CHUNK and seq % MOE_TILE == 0 and n_sample % MOE_TILE == 0

    w_in0, w_out0 = w_in[0], w_out[0]
    wq = w_in0[:, POOL_WIDTH:POOL_WIDTH + ATTN_WIDTH].reshape(D_MODEL, N_KV_HEADS, GQ, HEAD_DIM)
    wq = jnp.transpose(wq, (0, 2, 1, 3)).reshape(D_MODEL, ATTN_WIDTH)
    w_in_p = jnp.concatenate(
        [w_in0[:, :POOL_WIDTH], wq, w_in0[:, POOL_WIDTH + ATTN_WIDTH:]], axis=1).astype(BF16)
    wo = w_out0[POOL_WIDTH:].reshape(N_KV_HEADS, GQ, HEAD_DIM, D_MODEL)
    wo = jnp.transpose(wo, (1, 0, 2, 3)).reshape(ATTN_WIDTH, D_MODEL)
    w_out_p = jnp.concatenate([w_out0[:POOL_WIDTH], wo], axis=0).astype(BF16)
    w_pool_b = w_pool[0].astype(BF16)
    pscale = pool_scale[0].reshape(1, POOL_WIDTH)
    wr_t = w_router[0].T
    wr_hi = wr_t.astype(BF16)
    wr_lo = (wr_t - wr_hi.astype(F32)).astype(BF16)
    wr_split = jnp.concatenate([wr_hi, wr_lo], axis=0)
    rbias = router_bias[0].reshape(N_EXPERTS, 1)
    gains = norm_gains[0]
    sinks = attn_sinks[0]

    mod = _ada_call(jnp.concatenate([c_prompt, c_sample], axis=0), w_ada[0], b_ada[0])
    mod = mod.reshape(bsz + dbsz, 6, D_MODEL)

    zeros_pool = jnp.zeros((bsz, POOL_HIST_PAD, POOL_WIDTH), F32)
    zeros_kv = jnp.zeros((bsz, WINDOW, KV_WIDTH), F32)
    cnt0 = jnp.zeros((N_EXPERTS, LANES), F32)
    (x1_p, h2_p, pool_p, nk_p, nv_p, eidx_p, wts_p, rank_p, tcnt_p) = _front_call(
        x_prompt, mod[:bsz], gains, zeros_pool, zeros_kv, zeros_kv, cnt0, sinks,
        w_in_p, w_pool_b, pscale, w_out_p, wr_split, rbias, tile=MOE_TILE, has_cache=False)
    cnt_prompt = jnp.broadcast_to(jnp.sum(tcnt_p[:, 0, :], axis=0)[:, None], (N_EXPERTS, LANES))

    hist_pad = ATTN_KEYS - CHUNK - WINDOW
    pool_s0 = jnp.pad(state_pool[0], ((0, 0), (POOL_HIST_PAD - state_pool.shape[2], 0), (0, 0)))
    ck = jnp.pad(cache_k[0].reshape(dbsz, WINDOW, KV_WIDTH), ((0, 0), (hist_pad, 0), (0, 0)))
    cv = jnp.pad(cache_v[0].reshape(dbsz, WINDOW, KV_WIDTH), ((0, 0), (hist_pad, 0), (0, 0)))
    (x1_s, h2_s, pool_s, nk_s, nv_s, eidx_s, wts_s, rank_s, tcnt_s) = _front_call(
        x_sample, mod[bsz:], gains, pool_s0, ck, cv, cnt_prompt, sinks,
        w_in_p, w_pool_b, pscale, w_out_p, wr_split, rbias, tile=CHUNK, has_cache=True)

    def moe_tiles(route_p, route_s):
        per = MOE_TILE // CHUNK
        regrouped = jnp.transpose(route_s.reshape(-1, per, ROUTE_ROWS, CHUNK), (0, 2, 1, 3))
        return jnp.concatenate([route_p, regrouped.reshape(-1, ROUTE_ROWS, MOE_TILE)], axis=0)

    n_steps = n_tok // MOE_TILE
    cnt_td = jnp.concatenate(
        [tcnt_p[:, 0, :], tcnt_s[:, 0, :].reshape(-1, MOE_TILE // CHUNK, N_EXPERTS).sum(axis=1)],
        axis=0).astype(I32)
    c8 = (cnt_td + RUN_ROWS - 1) // RUN_ROWS * RUN_ROWS
    rows8 = jnp.sum(c8, axis=0)
    padded = (rows8 + EXPERT_ROWS - 1) // EXPERT_ROWS * EXPERT_ROWS
    pends = jnp.cumsum(padded)
    pstarts = pends - padded
    base = jnp.cumsum(cnt_td, axis=0) - cnt_td
    grow = pstarts[None, :] + jnp.cumsum(c8, axis=0) - c8
    lo8 = jnp.cumsum(c8, axis=1) - c8
    nch = c8 // RUN_ROWS
    nct = jnp.sum(nch, axis=1)
    off_tiles = (lo8 - base).astype(F32).reshape(n_steps, N_EXPERTS, 1)
    def flat_chunks(counts, max_n, local0, global0, rows):
        cum = jnp.cumsum(counts, axis=1)
        j = jnp.arange(max_n, dtype=I32)
        owner = jnp.sum(j[None, :, None] >= cum[:, None, :], axis=-1)
        owns = owner[:, :, None] == jnp.arange(N_EXPERTS, dtype=I32)[None, None, :]
        of_owner = lambda tab: jnp.sum(jnp.where(owns, tab[:, None, :], 0), axis=-1)
        within = rows * (j[None, :] - of_owner(cum - counts))
        return ((of_owner(local0) + within).reshape(-1).astype(I32),
                (of_owner(global0) + within).reshape(-1).astype(I32), cum[:, -1].astype(I32))

    n_double = nch // 2
    doubled = 2 * RUN_ROWS * n_double
    tables = (flat_chunks(n_double, MAX_DOUBLE_CHUNKS, lo8, grow, 2 * RUN_ROWS)
              + flat_chunks(nch - 2 * n_double, N_EXPERTS, lo8 + doubled, grow + doubled, RUN_ROWS)
              + ((nct * RUN_ROWS).astype(I32),))
    n_blocks = -(-(n_tok * TOP_K + n_steps * N_EXPERTS * (RUN_ROWS - 1) + N_EXPERTS * (EXPERT_ROWS - 1))
                 // EXPERT_ROWS)
    n_used = (pends[-1] // EXPERT_ROWS).astype(I32).reshape(1)
    zero_tables = ((pstarts + rows8).astype(I32), ((padded - rows8) // RUN_ROWS).astype(I32), n_used)

    h2_pf, h2_sf = h2_p.reshape(n_prompt, D_MODEL), h2_s.reshape(n_sample, D_MODEL)
    xs, slot_tiles = _dispatch_call(
        tables, zero_tables, h2_pf, h2_sf, moe_tiles(eidx_p, eidx_s), moe_tiles(rank_p, rank_s), off_tiles,
        n_blocks * EXPERT_ROWS)
    ys = _experts_call((pstarts // EXPERT_ROWS).astype(I32), (padded // EXPERT_ROWS).astype(I32),
                       xs, w_gate[0], w_up[0], w_down[0])
    gate_groups = jnp.concatenate(
        [jnp.repeat(mod[:bsz, 5], seq // GATE_GROUP, axis=0),
         jnp.repeat(mod[bsz:, 5], dseq // GATE_GROUP, axis=0)], axis=0)
    y_p, y_s = _combine_call(
        tables, x1_p.reshape(n_prompt, D_MODEL), x1_s.reshape(n_sample, D_MODEL), h2_pf, h2_sf,
        gate_groups, gains, slot_tiles, moe_tiles(wts_p, wts_s), ys,
        ws_gate[0].astype(BF16), ws_up[0].astype(BF16), ws_down[0].astype(BF16))

    n_hist = state_pool.shape[2]
    kv_shape = (1, -1, WINDOW, N_KV_HEADS, HEAD_DIM)
    return (y_p.reshape(bsz, seq, D_MODEL), y_s.reshape(dbsz, dseq, D_MODEL),
            pool_p[None, :, POOL_HIST_PAD - n_hist:], nk_p.reshape(kv_shape), nv_p.reshape(kv_shape),
            pool_s[None, :, POOL_HIST_PAD - n_hist:], nk_s.reshape(kv_shape), nv_s.reshape(kv_shape))
```

```python
import functools

import jax
import jax.numpy as jnp
from jax import lax
from jax.experimental import pallas as pl
from jax.experimental.pallas import tpu as pltpu

F32 = jnp.float32
BF16 = jnp.bfloat16
I32 = jnp.int32

D_MODEL = 1024
CHUNK = 64
POOL_WIDTH = 512
POOL_WINDOWS = (2, 4, 8, 16)
POOL_GROUP_W = 128
POOL_HIST_PAD = 16
HEAD_DIM = 64
N_HEADS = 8
N_KV_HEADS = 2
GQ = N_HEADS // N_KV_HEADS
ATTN_WIDTH = N_HEADS * HEAD_DIM
KV_WIDTH = N_KV_HEADS * HEAD_DIM
IN_WIDTH = POOL_WIDTH + ATTN_WIDTH + 2 * KV_WIDTH
WINDOW = 128
N_EXPERTS = 64
TOP_K = 6
N_EXPERT_GROUPS = 8
GROUP_SIZE = N_EXPERTS // N_EXPERT_GROUPS
TOPK_GROUPS = 4
D_EXPERT = 256
D_SHARED = 256
ROUTED_SCALE = 2.5
NORM_EPS = 1e-6
NEG_BIG = -1e30

LANES = 128
SUBLANES = 8
ATTN_KEYS = 2 * LANES
ROUTE_ROWS = 8
EXPERT_ROWS = 256
ITEM_BLOCKS = (4, 2, 1)
BIG_ROWS = ITEM_BLOCKS[0] * EXPERT_ROWS
MOE_TILE = 256
GATE_GROUP = MOE_TILE // SUBLANES
RUN_ROWS = 2 * SUBLANES
LOCAL_ROWS = 2560
LOCAL_ROW_OPTIONS = (2048, 2176, LOCAL_ROWS)
MAX_DOUBLE_CHUNKS = LOCAL_ROWS // (2 * RUN_ROWS)
ROW_W = D_MODEL
VMEM_LIMIT = 56 * 1024 * 1024

assert TOP_K * MOE_TILE + N_EXPERTS * (RUN_ROWS - 1) <= LOCAL_ROWS

_NT = (((1,), (1,)), ((), ()))


def _rms(v):
    return v * lax.rsqrt(jnp.mean(v * v, axis=-1, keepdims=True) + NORM_EPS)


def _sigmoid(v):
    return 1.0 / (1.0 + jnp.exp(-v))


def _split3(v):
    hi = v.astype(BF16)
    r1 = v - hi.astype(F32)
    mid = r1.astype(BF16)
    lo = (r1 - mid.astype(F32)).astype(BF16)
    return hi, mid, lo


def _to_sublanes(rows):
    n = rows.shape[1]
    hi, mid, lo = _split3(rows)
    eye = (lax.broadcasted_iota(I32, (n, n), 0) == lax.broadcasted_iota(I32, (n, n), 1)).astype(BF16)
    return (lax.dot_general(eye, hi, _NT, preferred_element_type=F32)
            + lax.dot_general(eye, mid, _NT, preferred_element_type=F32)
            + lax.dot_general(eye, lo, _NT, preferred_element_type=F32))


def _ada_body(c_ref, w_ref, b_ref, o_ref):
    c = c_ref[...]
    s = c * _sigmoid(c)
    o_ref[...] = jnp.dot(s, w_ref[...], preferred_element_type=F32) + b_ref[...]


def _ada_call(c_all, w_ada, b_ada):
    nb = c_all.shape[0]
    n_out = w_ada.shape[1]
    tile = D_MODEL
    return pl.pallas_call(
        _ada_body,
        grid=(n_out // tile,),
        in_specs=[
            pl.BlockSpec((nb, D_MODEL), lambda j: (0, 0)),
            pl.BlockSpec((D_MODEL, tile), lambda j: (0, j)),
            pl.BlockSpec((1, tile), lambda j: (0, j)),
        ],
        out_specs=pl.BlockSpec((nb, tile), lambda j: (0, j)),
        out_shape=jax.ShapeDtypeStruct((nb, n_out), F32),
        compiler_params=pltpu.CompilerParams(dimension_semantics=("arbitrary",)),
        name="ada",
    )(c_all, w_ada, b_ada.reshape(1, n_out))


def _front_body(tile, has_cache,
                sinks_ref, x_ref, mod_ref, gains_ref, hp_ref, hk_ref, hv_ref, cnt0_ref,
                win_ref, wpool_ref, pscale_ref, wout_ref, wrt_ref, rbias_ref,
                x1_ref, h2_ref, npool_ref, nk_ref, nv_ref, eidx_ref, wts_ref, rank_ref, tcnt_ref,
                ubuf, khist, vhist, run):
    b = pl.program_id(0)
    i = pl.program_id(1)
    sub_q = min(tile, LANES)
    n_sub = tile // sub_q
    hist_keys = ATTN_KEYS - sub_q
    n_keys = hist_keys + tile

    @pl.when(i == 0)
    def _():
        ubuf[0:POOL_HIST_PAD, :] = hp_ref[0]
        khist[...] = hk_ref[0]
        vhist[...] = hv_ref[0]

    @pl.when((b == 0) & (i == 0))
    def _():
        run[...] = cnt0_ref[...]

    mod = mod_ref[0]
    gains = gains_ref[...]
    xt = x_ref[0]

    h = _rms(xt) * gains[0:1] * (1.0 + mod[1:2]) + mod[0:1]
    proj = jnp.dot(h.astype(BF16), win_ref[...], preferred_element_type=F32)
    u = proj[:, :POOL_WIDTH]
    o_k = POOL_WIDTH + ATTN_WIDTH
    k_new = proj[:, o_k:o_k + KV_WIDTH]
    v_new = proj[:, o_k + KV_WIDTH:]

    kw = jnp.concatenate([khist[...], k_new], axis=0)
    vw = jnp.concatenate([vhist[...], v_new], axis=0)
    kwb = kw.astype(BF16)
    vwb = vw.astype(BF16)
    qrow = lax.broadcasted_iota(I32, (sub_q, ATTN_KEYS), 0)
    kpos = lax.broadcasted_iota(I32, (sub_q, ATTN_KEYS), 1) - hist_keys
    qchunk = jnp.right_shift(qrow, 6)
    vis_band = (kpos >= CHUNK * (qchunk - 2)) & (kpos < CHUNK * (qchunk + 1))
    side0 = lax.broadcasted_iota(I32, (sub_q, LANES), 1) < HEAD_DIM
    attn_rows = []
    for r in range(n_sub):
        vis = vis_band
        if not has_cache:
            vis = vis & (kpos + (i * tile + r * sub_q) >= 0)
        kb = kwb[r * sub_q:r * sub_q + ATTN_KEYS]
        vb = vwb[r * sub_q:r * sub_q + ATTN_KEYS]
        if sub_q < LANES:
            q_parts = []
            for j in range(GQ):
                qp = proj[r * sub_q:(r + 1) * sub_q, POOL_WIDTH + LANES * j:POOL_WIDTH + LANES * (j + 1)]
                qp = qp * (HEAD_DIM ** -0.5)
                q_parts.append(jnp.where(side0, qp, 0.0).astype(BF16))
                q_parts.append(jnp.where(side0, 0.0, qp).astype(BF16))
            sink_col = jnp.concatenate(
                [jnp.full((sub_q, 1), sinks_ref[j + GQ * s], F32) for j in range(GQ) for s in range(N_KV_HEADS)],
                axis=0)
            vis_all = jnp.concatenate([vis.astype(F32)] * N_HEADS, axis=0) > 0.0
            sc = lax.dot_general(jnp.concatenate(q_parts, axis=0), kb, _NT, preferred_element_type=F32)
            sc = jnp.where(vis_all, sc, NEG_BIG)
            m = jnp.maximum(jnp.max(sc, axis=-1, keepdims=True), sink_col)
            p = jnp.exp(sc - m)
            den = jnp.sum(p, axis=-1, keepdims=True) + jnp.exp(sink_col - m)
            out = jnp.dot((p / den).astype(BF16), vb, preferred_element_type=F32)
            attn_rows.append(jnp.concatenate(
                [jnp.where(side0, out[(2 * j) * sub_q:(2 * j + 1) * sub_q],
                           out[(2 * j + 1) * sub_q:(2 * j + 2) * sub_q]) for j in range(GQ)], axis=-1))
            continue
        blocks = []
        for j in range(GQ):
            qp = proj[r * sub_q:(r + 1) * sub_q, POOL_WIDTH + LANES * j:POOL_WIDTH + LANES * (j + 1)]
            qp = qp * (HEAD_DIM ** -0.5)
            outs = []
            for s in range(N_KV_HEADS):
                keep = side0 if s == 0 else jnp.logical_not(side0)
                qm = jnp.where(keep, qp, 0.0).astype(BF16)
                sc = lax.dot_general(qm, kb, _NT, preferred_element_type=F32)
                sc = jnp.where(vis, sc, NEG_BIG)
                sink = sinks_ref[j + GQ * s]
                m = jnp.maximum(jnp.max(sc, axis=-1, keepdims=True), sink)
                p = jnp.exp(sc - m)
                den = jnp.sum(p, axis=-1, keepdims=True) + jnp.exp(sink - m)
                p = (p / den).astype(BF16)
                outs.append(jnp.dot(p, vb, preferred_element_type=F32))
            blocks.append(jnp.where(side0, outs[0], outs[1]))
        attn_rows.append(jnp.concatenate(blocks, axis=-1))
    attn = attn_rows[0] if n_sub == 1 else jnp.concatenate(attn_rows, axis=0)

    ubuf[POOL_HIST_PAD:POOL_HIST_PAD + tile, :] = u
    if has_cache:
        seen = None
    else:
        seen = (lax.broadcasted_iota(I32, (tile, 1), 0) + i * tile + 1).astype(F32)
    pool_blocks = []
    for g, w in enumerate(POOL_WINDOWS):
        cols = slice(POOL_GROUP_W * g, POOL_GROUP_W * (g + 1))
        acc = u[:, cols]
        for s in range(1, w):
            acc = acc + ubuf[POOL_HIST_PAD - s:POOL_HIST_PAD - s + tile, cols]
        cnt = float(w) if seen is None else jnp.minimum(seen, float(w))
        dlt = acc / cnt - u[:, cols]
        pool_blocks.append(jnp.dot(dlt.astype(BF16), wpool_ref[g], preferred_element_type=F32))
    pool = jnp.concatenate(pool_blocks, axis=-1) * pscale_ref[...]

    mixin = jnp.concatenate([pool, attn], axis=-1).astype(BF16)
    mix = jnp.dot(mixin, wout_ref[...], preferred_element_type=F32)
    x1 = xt + mod[2:3] * (_rms(mix) * gains[1:2])
    x1_ref[0] = x1

    new_hist = ubuf[tile:tile + POOL_HIST_PAD, :]
    ubuf[0:POOL_HIST_PAD, :] = new_hist
    npool_ref[0] = new_hist
    khist[...] = kw[tile:, :]
    vhist[...] = vw[tile:, :]
    nk_ref[0] = kw[n_keys - WINDOW:, :]
    nv_ref[0] = vw[n_keys - WINDOW:, :]

    h2f = _rms(x1) * gains[2:3] * (1.0 + mod[4:5]) + mod[3:4]
    h2hi = h2f.astype(BF16)
    h2_ref[0] = h2hi
    h2lo = (h2f - h2hi.astype(F32)).astype(BF16)
    wrt = wrt_ref[...]
    part = lax.dot_general(wrt, h2hi, _NT, preferred_element_type=F32)
    logits = (part[:N_EXPERTS] + part[N_EXPERTS:]
              + lax.dot_general(wrt[:N_EXPERTS], h2lo, _NT, preferred_element_type=F32))
    scores = _sigmoid(logits)
    sel = scores + rbias_ref[...]

    sub_g = lax.broadcasted_iota(I32, (GROUP_SIZE, tile), 0).astype(F32)
    gscore = jnp.zeros((N_EXPERT_GROUPS, tile), F32)
    for gi in range(N_EXPERT_GROUPS):
        blk = sel[GROUP_SIZE * gi:GROUP_SIZE * (gi + 1), :]
        m1 = jnp.max(blk, axis=0, keepdims=True)
        i1 = jnp.min(jnp.where(blk == m1, sub_g, float(GROUP_SIZE)), axis=0, keepdims=True)
        m2 = jnp.max(jnp.where(sub_g == i1, -jnp.inf, blk), axis=0, keepdims=True)
        gscore = jnp.where(sub_g == gi, m1 + m2, gscore)
    chosen = jnp.zeros((N_EXPERT_GROUPS, tile), F32)
    for _ in range(TOPK_GROUPS):
        m = jnp.max(gscore, axis=0, keepdims=True)
        idx = jnp.min(jnp.where(gscore == m, sub_g, float(N_EXPERT_GROUPS)), axis=0, keepdims=True)
        pick = sub_g == idx
        chosen = jnp.where(pick, 1.0, chosen)
        gscore = jnp.where(pick, -jnp.inf, gscore)
    emask = jnp.concatenate(
        [jnp.broadcast_to(chosen[gi:gi + 1, :], (GROUP_SIZE, tile)) for gi in range(N_EXPERT_GROUPS)], axis=0)
    selm = jnp.where(emask > 0.0, sel, -jnp.inf)

    sub_e = lax.broadcasted_iota(I32, (N_EXPERTS, tile), 0).astype(F32)
    picks, idxs, raw_w = [], [], []
    for _ in range(TOP_K):
        m = jnp.max(selm, axis=0, keepdims=True)
        idx = jnp.min(jnp.where(selm == m, sub_e, float(N_EXPERTS)), axis=0, keepdims=True)
        pick = sub_e == idx
        raw_w.append(jnp.sum(jnp.where(pick, scores, 0.0), axis=0, keepdims=True))
        selm = jnp.where(pick, -jnp.inf, selm)
        picks.append(pick)
        idxs.append(idx)
    wsum = raw_w[0]
    for kk in range(1, TOP_K):
        wsum = wsum + raw_w[kk]

    onehot = jnp.zeros((N_EXPERTS, tile), F32)
    for kk in range(TOP_K):
        onehot = jnp.where(picks[kk], 1.0, onehot)
    onehot_b = onehot.astype(BF16)
    tri = (lax.broadcasted_iota(I32, (tile, tile), 0) < lax.broadcasted_iota(I32, (tile, tile), 1)).astype(BF16)
    before = jnp.dot(onehot_b, tri, preferred_element_type=F32) + run[:, 0:1]
    sub_r = lax.broadcasted_iota(I32, (ROUTE_ROWS, tile), 0)
    eidx_o = jnp.zeros((ROUTE_ROWS, tile), I32)
    wts_o = jnp.zeros((ROUTE_ROWS, tile), F32)
    rank_o = jnp.zeros((ROUTE_ROWS, tile), I32)
    for kk in range(TOP_K):
        rk = jnp.sum(jnp.where(picks[kk], before, 0.0), axis=0, keepdims=True).astype(I32)
        eidx_o = jnp.where(sub_r == kk, idxs[kk].astype(I32), eidx_o)
        wts_o = jnp.where(sub_r == kk, raw_w[kk] / wsum * ROUTED_SCALE, wts_o)
        rank_o = jnp.where(sub_r == kk, rk, rank_o)
    eidx_ref[0] = eidx_o
    wts_ref[0] = wts_o
    rank_ref[0] = rank_o
    tcnt_ref[0] = lax.dot_general(jnp.ones((ROUTE_ROWS, tile), BF16), onehot_b, _NT,
                                  preferred_element_type=F32)
    run[...] = run[...] + jnp.sum(onehot, axis=1, keepdims=True)


def _front_call(x, mod, gains, hist_pool, hist_k, hist_v, cnt0, sinks,
                w_in, w_pool, pool_scale, w_out, wr_t, rbias, *, tile, has_cache):
    bsz, seq, _ = x.shape
    n_tiles = seq // tile
    hist_keys = hist_k.shape[1]
    assert hist_keys == ATTN_KEYS - min(tile, LANES)
    body = functools.partial(_front_body, tile, has_cache)
    whole = lambda shape: pl.BlockSpec(shape, lambda b, i: (0,) * len(shape))
    per_b = lambda shape: pl.BlockSpec((1,) + shape, lambda b, i: (b,) + (0,) * len(shape))
    route = pl.BlockSpec((1, ROUTE_ROWS, tile), lambda b, i: (b * n_tiles + i, 0, 0))
    out_shape = [
        jax.ShapeDtypeStruct((bsz, seq, D_MODEL), F32),
        jax.ShapeDtypeStruct((bsz, seq, D_MODEL), BF16),
        jax.ShapeDtypeStruct((bsz, POOL_HIST_PAD, POOL_WIDTH), F32),
        jax.ShapeDtypeStruct((bsz, WINDOW, KV_WIDTH), F32),
        jax.ShapeDtypeStruct((bsz, WINDOW, KV_WIDTH), F32),
        jax.ShapeDtypeStruct((bsz * n_tiles, ROUTE_ROWS, tile), I32),
        jax.ShapeDtypeStruct((bsz * n_tiles, ROUTE_ROWS, tile), F32),
        jax.ShapeDtypeStruct((bsz * n_tiles, ROUTE_ROWS, tile), I32),
        jax.ShapeDtypeStruct((bsz * n_tiles, ROUTE_ROWS, N_EXPERTS), F32),
    ]
    return pl.pallas_call(
        body,
        grid=(bsz, n_tiles),
        in_specs=[
            pl.BlockSpec(memory_space=pltpu.SMEM),
            pl.BlockSpec((1, tile, D_MODEL), lambda b, i: (b, i, 0)),
            per_b((6, D_MODEL)),
            whole((4, D_MODEL)),
            per_b((POOL_HIST_PAD, POOL_WIDTH)),
            per_b((hist_keys, KV_WIDTH)),
            per_b((hist_keys, KV_WIDTH)),
            whole((N_EXPERTS, LANES)),
            whole((D_MODEL, IN_WIDTH)),
            whole((len(POOL_WINDOWS), POOL_GROUP_W, POOL_GROUP_W)),
            whole((1, POOL_WIDTH)),
            whole((D_MODEL, D_MODEL)),
            whole((2 * N_EXPERTS, D_MODEL)),
            whole((N_EXPERTS, 1)),
        ],
        out_specs=[
            pl.BlockSpec((1, tile, D_MODEL), lambda b, i: (b, i, 0)),
            pl.BlockSpec((1, tile, D_MODEL), lambda b, i: (b, i, 0)),
            per_b((POOL_HIST_PAD, POOL_WIDTH)),
            per_b((WINDOW, KV_WIDTH)),
            per_b((WINDOW, KV_WIDTH)),
            route, route, route,
            pl.BlockSpec((1, ROUTE_ROWS, N_EXPERTS), lambda b, i: (b * n_tiles + i, 0, 0)),
        ],
        out_shape=out_shape,
        scratch_shapes=[
            pltpu.VMEM((tile + POOL_HIST_PAD, POOL_WIDTH), F32),
            pltpu.VMEM((hist_keys, KV_WIDTH), F32),
            pltpu.VMEM((hist_keys, KV_WIDTH), F32),
            pltpu.VMEM((N_EXPERTS, LANES), F32),
        ],
        compiler_params=pltpu.CompilerParams(
            dimension_semantics=("arbitrary", "arbitrary"), vmem_limit_bytes=VMEM_LIMIT),
        name="front_cached" if has_cache else "front_prompt",
    )(sinks, x, mod, gains, hist_pool, hist_k, hist_v, cnt0,
      w_in, w_pool, pool_scale, w_out, wr_t, rbias)


def _run_copy(src, dst, s_row, d_row, sem, rows=RUN_ROWS):
    return pltpu.make_async_copy(src.at[pl.ds(s_row, rows)], dst.at[pl.ds(d_row, rows)], sem)


def _for_each_run_chunk(step, chunk_tables, fn):
    lrow2_ref, grow2_ref, n2_ref, lrow1_ref, grow1_ref, n1_ref = chunk_tables

    def per_double(j, carry):
        idx = step * MAX_DOUBLE_CHUNKS + j
        fn(pl.multiple_of(lrow2_ref[idx], RUN_ROWS), pl.multiple_of(grow2_ref[idx], RUN_ROWS), 2 * RUN_ROWS)
        return carry

    def per_single(j, carry):
        idx = step * N_EXPERTS + j
        fn(pl.multiple_of(lrow1_ref[idx], RUN_ROWS), pl.multiple_of(grow1_ref[idx], RUN_ROWS), RUN_ROWS)
        return carry

    lax.fori_loop(0, n2_ref[step], per_double, 0)
    lax.fori_loop(0, n1_ref[step], per_single, 0)


def _wait_run_chunks(src, dst, sem, n_double, n_single):
    def one_double(c, carry):
        _run_copy(src, dst, 0, 0, sem, 2 * RUN_ROWS).wait()
        return carry

    def one_single(c, carry):
        _run_copy(src, dst, 0, 0, sem).wait()
        return carry

    lax.fori_loop(0, n_double, one_double, 0)
    lax.fori_loop(0, n_single, one_single, 0)


def _for_row_option(used_rows, fn):
    lower = 0
    for n_rows in LOCAL_ROW_OPTIONS:
        @pl.when((used_rows > lower) & (used_rows <= n_rows))
        def _(n_rows=n_rows):
            fn(n_rows)
        lower = n_rows


def _dispatch_body(n_prompt_steps, lrow2_ref, grow2_ref, n2_ref, lrow1_ref, grow1_ref, n1_ref, used_ref,
                   zrow_ref, znch_ref, nused_ref,
                   h2p_ref, h2s_ref, eidx_ref, rank_ref, off_ref, xs_out, slot_out, loc, zrows, sem):
    chunk_tables = (lrow2_ref, grow2_ref, n2_ref, lrow1_ref, grow1_ref, n1_ref)
    i = pl.program_id(0)
    n_steps = pl.num_programs(0)
    par = lax.rem(i, 2)
    h2 = jnp.where(i < n_prompt_steps, h2p_ref[...], h2s_ref[...])
    eidx = eidx_ref[0]
    rank = rank_ref[0]
    off = off_ref[0]
    expert_id = lax.broadcasted_iota(I32, (N_EXPERTS, MOE_TILE), 0)
    sub_r = lax.broadcasted_iota(I32, (ROUTE_ROWS, MOE_TILE), 0)
    slots = []
    slot_o = jnp.zeros((ROUTE_ROWS, MOE_TILE), I32)
    for kk in range(TOP_K):
        mine = jnp.sum(jnp.where(expert_id == eidx[kk:kk + 1, :], off, 0.0), axis=0, keepdims=True)
        slots.append(rank[kk:kk + 1, :] + mine.astype(I32))
        slot_o = jnp.where(sub_r == kk, slots[kk], slot_o)
    slot_out[0] = slot_o
    buf = loc.at[par]

    def sort_tile(n_rows):
        row_id = lax.broadcasted_iota(I32, (n_rows, MOE_TILE), 0).astype(jnp.int16)
        sel = jnp.zeros((n_rows, MOE_TILE), BF16)
        for kk in range(TOP_K):
            sel = jnp.where(row_id == slots[kk].astype(jnp.int16), jnp.ones((), BF16), sel)
        buf[0:n_rows, :] = jnp.dot(sel, h2, preferred_element_type=F32).astype(BF16)

    _for_row_option(used_ref[i], sort_tile)

    @pl.when(i > 0)
    def _():
        _wait_run_chunks(buf, xs_out, sem, n2_ref[i - 1], n1_ref[i - 1])

    _for_each_run_chunk(i, chunk_tables,
                        lambda lrow, grow, rows: _run_copy(buf, xs_out, lrow, grow, sem, rows).start())

    @pl.when(i == n_steps - 1)
    def _():
        zrows[...] = jnp.zeros_like(zrows)

        def per_expert(e, total):
            def per_chunk(c, carry):
                _run_copy(zrows, xs_out, 0, pl.multiple_of(zrow_ref[e] + RUN_ROWS * c, RUN_ROWS), sem).start()
                return carry
            lax.fori_loop(0, znch_ref[e], per_chunk, 0)
            return total + znch_ref[e]

        n_zero = lax.fori_loop(0, N_EXPERTS, per_expert, 0)
        _wait_run_chunks(buf, xs_out, sem, n2_ref[i], n1_ref[i] + n_zero)

        def block_copy(blk):
            return pltpu.make_async_copy(
                zrows, xs_out.at[pl.ds(pl.multiple_of(blk * EXPERT_ROWS, EXPERT_ROWS), EXPERT_ROWS)], sem)

        n_blocks = xs_out.shape[0] // EXPERT_ROWS

        def start_block(blk, carry):
            block_copy(blk).start()
            return carry

        def wait_block(blk, carry):
            block_copy(blk).wait()
            return carry

        lax.fori_loop(nused_ref[0], n_blocks, start_block, 0)
        lax.fori_loop(nused_ref[0], n_blocks, wait_block, 0)


def _dispatch_call(tables, zero_tables, h2_p, h2_s, eidx_tiles, rank_tiles, off_tiles, n_rows):
    n_prompt_steps = h2_p.shape[0] // MOE_TILE
    n_steps = n_prompt_steps + h2_s.shape[0] // MOE_TILE
    route = lambda: pl.BlockSpec((1, ROUTE_ROWS, MOE_TILE), lambda i, *_: (i, 0, 0))
    grid_spec = pltpu.PrefetchScalarGridSpec(
        num_scalar_prefetch=10,
        grid=(n_steps,),
        in_specs=[
            pl.BlockSpec((MOE_TILE, D_MODEL), lambda i, *_: (jnp.minimum(i, n_prompt_steps - 1), 0)),
            pl.BlockSpec((MOE_TILE, D_MODEL), lambda i, *_: (jnp.maximum(i - n_prompt_steps, 0), 0)),
            route(), route(),
            pl.BlockSpec((1, N_EXPERTS, 1), lambda i, *_: (i, 0, 0)),
        ],
        out_specs=[pl.BlockSpec(memory_space=pl.ANY), route()],
        scratch_shapes=[
            pltpu.VMEM((2, LOCAL_ROWS, ROW_W), BF16),
            pltpu.VMEM((EXPERT_ROWS, ROW_W), BF16),
            pltpu.SemaphoreType.DMA,
        ],
    )
    return pl.pallas_call(
        functools.partial(_dispatch_body, n_prompt_steps),
        grid_spec=grid_spec,
        out_shape=[jax.ShapeDtypeStruct((n_rows, ROW_W), BF16),
                   jax.ShapeDtypeStruct((n_steps, ROUTE_ROWS, MOE_TILE), I32)],
        compiler_params=pltpu.CompilerParams(
            dimension_semantics=("arbitrary",), vmem_limit_bytes=VMEM_LIMIT),
        name="dispatch",
    )(*tables, *zero_tables, h2_p, h2_s, eidx_tiles, rank_tiles, off_tiles)


def _experts_body(first_ref, nblk_ref, xs_hbm, wg_ref, wu_ref, wd_ref, ys_hbm,
                  xbuf, ybuf, wgb, wub, wdb, isem, osem):
    e = pl.program_id(0)
    n_exp = pl.num_programs(0)

    def items_of(expert):
        rem = nblk_ref[expert]
        row = first_ref[expert] * EXPERT_ROWS
        done = 0
        lows, highs, bases = [], [], []
        for blocks in ITEM_BLOCKS:
            n = rem // blocks
            lows.append(done)
            bases.append(row)
            done = done + n
            row = row + n * (blocks * EXPERT_ROWS)
            rem = rem - n * blocks
            highs.append(done)
        return (lows, highs, bases), done

    def for_item(items, t, slot, action):
        lows, highs, bases = items
        for c, blocks in enumerate(ITEM_BLOCKS):
            rows = blocks * EXPERT_ROWS

            @pl.when((t >= lows[c]) & (t < highs[c]))
            def _(c=c, rows=rows):
                action(pl.multiple_of(bases[c] + (t - lows[c]) * rows, EXPERT_ROWS), rows, slot)

    def in_copy(row0, rows, slot):
        return pltpu.make_async_copy(xs_hbm.at[pl.ds(row0, rows)], xbuf.at[slot, pl.ds(0, rows)], isem.at[slot])

    def out_copy(row0, rows, slot):
        return pltpu.make_async_copy(ybuf.at[slot, pl.ds(0, rows)], ys_hbm.at[pl.ds(row0, rows)], osem.at[slot])

    start_in = lambda row0, rows, slot: in_copy(row0, rows, slot).start()
    wait_in = lambda row0, rows, slot: in_copy(row0, rows, slot).wait()
    start_out = lambda row0, rows, slot: out_copy(row0, rows, slot).start()
    wait_out = lambda row0, rows, slot: out_copy(row0, rows, slot).wait()
    items, n_items = items_of(e)

    @pl.when((e == 0) & (n_items > 0))
    def _():
        for_item(items, 0, 0, start_in)

    def compute(row0, rows, slot):
        del row0
        x = xbuf[slot, 0:rows, :]
        g = jnp.dot(x, wgb[...], preferred_element_type=F32)
        u = jnp.dot(x, wub[...], preferred_element_type=F32)
        a = (g * _sigmoid(g) * u).astype(BF16)
        ybuf[slot, 0:rows, :] = jnp.dot(a, wdb[...], preferred_element_type=F32).astype(BF16)

    prev_items, prev_n = items_of(jnp.maximum(e - 1, 0))

    def wait_previous_expert():
        @pl.when((e > 0) & (prev_n > 0))
        def _():
            for_item(prev_items, prev_n - 1, lax.rem(prev_n - 1, 2), wait_out)

    @pl.when(n_items == 0)
    def _():
        wait_previous_expert()

    @pl.when(n_items > 0)
    def _():
        wgb[...] = wg_ref[0].astype(BF16)
        wub[...] = wu_ref[0].astype(BF16)
        wdb[...] = wd_ref[0].astype(BF16)
        wait_previous_expert()

        def one_item(t, carry):
            slot = lax.rem(t, 2)

            @pl.when(t + 1 < n_items)
            def _():
                for_item(items, t + 1, 1 - slot, start_in)

            for_item(items, t, slot, wait_in)
            for_item(items, t, slot, compute)

            @pl.when(t >= 1)
            def _():
                for_item(items, t - 1, 1 - slot, wait_out)

            for_item(items, t, slot, start_out)
            return carry

        lax.fori_loop(0, n_items, one_item, 0)

        @pl.when(e == n_exp - 1)
        def _():
            for_item(items, n_items - 1, lax.rem(n_items - 1, 2), wait_out)

    nxt = jnp.minimum(e + 1, n_exp - 1)
    nxt_items, nxt_n = items_of(nxt)

    @pl.when((e + 1 < n_exp) & (nxt_n > 0))
    def _():
        for_item(nxt_items, 0, 0, start_in)


def _experts_call(first_block, n_expert_blocks, xs, w_gate, w_up, w_down):
    grid_spec = pltpu.PrefetchScalarGridSpec(
        num_scalar_prefetch=2,
        grid=(N_EXPERTS,),
        in_specs=[
            pl.BlockSpec(memory_space=pl.ANY),
            pl.BlockSpec((1, D_MODEL, D_EXPERT), lambda e, *_: (e, 0, 0)),
            pl.BlockSpec((1, D_MODEL, D_EXPERT), lambda e, *_: (e, 0, 0)),
            pl.BlockSpec((1, D_EXPERT, D_MODEL), lambda e, *_: (e, 0, 0)),
        ],
        out_specs=pl.BlockSpec(memory_space=pl.ANY),
        scratch_shapes=[
            pltpu.VMEM((2, BIG_ROWS, ROW_W), BF16),
            pltpu.VMEM((2, BIG_ROWS, D_MODEL), BF16),
            pltpu.VMEM((D_MODEL, D_EXPERT), BF16),
            pltpu.VMEM((D_MODEL, D_EXPERT), BF16),
            pltpu.VMEM((D_EXPERT, D_MODEL), BF16),
            pltpu.SemaphoreType.DMA((2,)),
            pltpu.SemaphoreType.DMA((2,)),
        ],
    )
    return pl.pallas_call(
        _experts_body,
        grid_spec=grid_spec,
        out_shape=jax.ShapeDtypeStruct(xs.shape, xs.dtype),
        input_output_aliases={2: 0},
        compiler_params=pltpu.CompilerParams(
            dimension_semantics=("arbitrary",), vmem_limit_bytes=VMEM_LIMIT),
        name="experts",
    )(first_block, n_expert_blocks, xs, w_gate, w_up, w_down)


def _combine_body(n_prompt_steps, lrow2_ref, grow2_ref, n2_ref, lrow1_ref, grow1_ref, n1_ref, used_ref,
                  x1p_ref, x1s_ref, h2p_ref, h2s_ref, gate_ref, gains_ref, slot_ref, wts_ref, ys_hbm,
                  wsg_ref, wsu_ref, wsd_ref, outp_ref, outs_ref, gath, routed, sem):
    chunk_tables = (lrow2_ref, grow2_ref, n2_ref, lrow1_ref, grow1_ref, n1_ref)
    i = pl.program_id(0)
    n_steps = pl.num_programs(0)
    par = lax.rem(i, 2)

    def fetch(step, slot):
        buf = gath.at[slot]
        _for_each_run_chunk(
            step, chunk_tables,
            lambda lrow, grow, rows: _run_copy(ys_hbm, buf, grow, lrow, sem.at[slot], rows).start())

    @pl.when(i == 0)
    def _():
        gath[...] = jnp.zeros_like(gath)
        fetch(0, 0)

    @pl.when(i + 1 < n_steps)
    def _():
        fetch(i + 1, 1 - par)

    is_prompt = i < n_prompt_steps
    h2 = jnp.where(is_prompt, h2p_ref[...], h2s_ref[...])
    g = jnp.dot(h2, wsg_ref[...], preferred_element_type=F32)
    u = jnp.dot(h2, wsu_ref[...], preferred_element_type=F32)
    a = (g * _sigmoid(g) * u).astype(BF16)
    ffn = jnp.dot(a, wsd_ref[...], preferred_element_type=F32)

    slot_cols = _to_sublanes(slot_ref[0].astype(F32))
    slot16 = slot_cols.astype(I32).astype(jnp.int16)
    w16 = _to_sublanes(wts_ref[0]).astype(BF16)
    buf = gath.at[par]

    _wait_run_chunks(ys_hbm, buf, sem.at[par], n2_ref[i], n1_ref[i])

    def unsort(n_rows):
        col_id = lax.broadcasted_iota(I32, (MOE_TILE, n_rows), 1).astype(jnp.int16)
        take = jnp.zeros((MOE_TILE, n_rows), BF16)
        for kk in range(TOP_K):
            take = jnp.where(col_id == slot16[:, kk:kk + 1], w16[:, kk:kk + 1], take)
        routed[...] = jnp.dot(take, buf[0:n_rows, :], preferred_element_type=F32)

    _for_row_option(used_ref[i], unsort)
    ffn = ffn + routed[...]

    quarter = MOE_TILE // gate_ref.shape[0]
    gate = jnp.concatenate(
        [jnp.broadcast_to(gate_ref[q:q + 1, :], (quarter, D_MODEL)) for q in range(gate_ref.shape[0])], axis=0)
    x1 = jnp.where(is_prompt, x1p_ref[...], x1s_ref[...])
    out = x1 + gate * (_rms(ffn) * gains_ref[3:4, :])

    @pl.when(is_prompt)
    def _():
        outp_ref[...] = out

    @pl.when(jnp.logical_not(is_prompt))
    def _():
        outs_ref[...] = out


def _combine_call(tables, x1_p, x1_s, h2_p, h2_s, gate_groups, gains, slot_tiles, wts_tiles, ys,
                  ws_gate, ws_up, ws_down):
    n_prompt, n_sample = x1_p.shape[0], x1_s.shape[0]
    n_prompt_steps = n_prompt // MOE_TILE
    n_steps = n_prompt_steps + n_sample // MOE_TILE
    groups_per_tile = gate_groups.shape[0] // n_steps
    tok_p = lambda: pl.BlockSpec((MOE_TILE, D_MODEL), lambda i, *_: (jnp.minimum(i, n_prompt_steps - 1), 0))
    tok_s = lambda: pl.BlockSpec((MOE_TILE, D_MODEL), lambda i, *_: (jnp.maximum(i - n_prompt_steps, 0), 0))
    whole = lambda shape: pl.BlockSpec(shape, lambda i, *_: (0,) * len(shape))
    grid_spec = pltpu.PrefetchScalarGridSpec(
        num_scalar_prefetch=7,
        grid=(n_steps,),
        in_specs=[
            tok_p(), tok_s(), tok_p(), tok_s(),
            pl.BlockSpec((groups_per_tile, D_MODEL), lambda i, *_: (i, 0)),
            whole((4, D_MODEL)),
            pl.BlockSpec((1, ROUTE_ROWS, MOE_TILE), lambda i, *_: (i, 0, 0)),
            pl.BlockSpec((1, ROUTE_ROWS, MOE_TILE), lambda i, *_: (i, 0, 0)),
            pl.BlockSpec(memory_space=pl.ANY),
            whole((D_MODEL, D_SHARED)), whole((D_MODEL, D_SHARED)), whole((D_SHARED, D_MODEL)),
        ],
        out_specs=[tok_p(), tok_s()],
        scratch_shapes=[
            pltpu.VMEM((2, LOCAL_ROWS, D_MODEL), BF16),
            pltpu.VMEM((MOE_TILE, D_MODEL), F32),
            pltpu.SemaphoreType.DMA((2,)),
        ],
    )
    return pl.pallas_call(
        functools.partial(_combine_body, n_prompt_steps),
        grid_spec=grid_spec,
        out_shape=[jax.ShapeDtypeStruct((n_prompt, D_MODEL), F32),
                   jax.ShapeDtypeStruct((n_sample, D_MODEL), F32)],
        compiler_params=pltpu.CompilerParams(
            dimension_semantics=("arbitrary",), vmem_limit_bytes=VMEM_LIMIT),
        name="combine",
    )(*tables, x1_p, x1_s, h2_p, h2_s, gate_groups, gains, slot_tiles, wts_tiles, ys, ws_gate, ws_up, ws_down)


def kernel(x_prompt, x_sample, c_prompt, c_sample, state_pool, cache_k, cache_v, w_ada, b_ada, norm_gains,
           w_in, w_pool, pool_scale, attn_sinks, w_out, w_router, router_bias, w_gate, w_up, w_down,
           ws_gate, ws_up, ws_down):
    assert w_ada.shape[0] == 1, "single-layer kernel"
    bsz, seq, _ = x_prompt.shape
    dbsz, dseq, _ = x_sample.shape
    n_prompt, n_sample = bsz * seq, dbsz * dseq
    n_tok = n_prompt + n_sample
    assert dseq == CHUNK and seq % MOE_TILE == 0 and n_sample % MOE_TILE == 0

    w_in0, w_out0 = w_in[0], w_out[0]
    wq = w_in0[:, POOL_WIDTH:POOL_WIDTH + ATTN_WIDTH].reshape(D_MODEL, N_KV_HEADS, GQ, HEAD_DIM)
    wq = jnp.transpose(wq, (0, 2, 1, 3)).reshape(D_MODEL, ATTN_WIDTH)
    w_in_p = jnp.concatenate(
        [w_in0[:, :POOL_WIDTH], wq, w_in0[:, POOL_WIDTH + ATTN_WIDTH:]], axis=1).astype(BF16)
    wo = w_out0[POOL_WIDTH:].reshape(N_KV_HEADS, GQ, HEAD_DIM, D_MODEL)
    wo = jnp.transpose(wo, (1, 0, 2, 3)).reshape(ATTN_WIDTH, D_MODEL)
    w_out_p = jnp.concatenate([w_out0[:POOL_WIDTH], wo], axis=0).astype(BF16)
    w_pool_b = w_pool[0].astype(BF16)
    pscale = pool_scale[0].reshape(1, POOL_WIDTH)
    wr_t = w_router[0].T
    wr_hi = wr_t.astype(BF16)
    wr_lo = (wr_t - wr_hi.astype(F32)).astype(BF16)
    wr_split = jnp.concatenate([wr_hi, wr_lo], axis=0)
    rbias = router_bias[0].reshape(N_EXPERTS, 1)
    gains = norm_gains[0]
    sinks = attn_sinks[0]

    mod = _ada_call(jnp.concatenate([c_prompt, c_sample], axis=0), w_ada[0], b_ada[0])
    mod = mod.reshape(bsz + dbsz, 6, D_MODEL)

    zeros_pool = jnp.zeros((bsz, POOL_HIST_PAD, POOL_WIDTH), F32)
    zeros_kv = jnp.zeros((bsz, WINDOW, KV_WIDTH), F32)
    cnt0 = jnp.zeros((N_EXPERTS, LANES), F32)
    (x1_p, h2_p, pool_p, nk_p, nv_p, eidx_p, wts_p, rank_p, tcnt_p) = _front_call(
        x_prompt, mod[:bsz], gains, zeros_pool, zeros_kv, zeros_kv, cnt0, sinks,
        w_in_p, w_pool_b, pscale, w_out_p, wr_split, rbias, tile=MOE_TILE, has_cache=False)
    cnt_prompt = jnp.broadcast_to(jnp.sum(tcnt_p[:, 0, :], axis=0)[:, None], (N_EXPERTS, LANES))

    hist_pad = ATTN_KEYS - CHUNK - WINDOW
    pool_s0 = jnp.pad(state_pool[0], ((0, 0), (POOL_HIST_PAD - state_pool.shape[2], 0), (0, 0)))
    ck = jnp.pad(cache_k[0].reshape(dbsz, WINDOW, KV_WIDTH), ((0, 0), (hist_pad, 0), (0, 0)))
    cv = jnp.pad(cache_v[0].reshape(dbsz, WINDOW, KV_WIDTH), ((0, 0), (hist_pad, 0), (0, 0)))
    (x1_s, h2_s, pool_s, nk_s, nv_s, eidx_s, wts_s, rank_s, tcnt_s) = _front_call(
        x_sample, mod[bsz:], gains, pool_s0, ck, cv, cnt_prompt, sinks,
        w_in_p, w_pool_b, pscale, w_out_p, wr_split, rbias, tile=CHUNK, has_cache=True)

    def moe_tiles(route_p, route_s):
        per = MOE_TILE // CHUNK
        regrouped = jnp.transpose(route_s.reshape(-1, per, ROUTE_ROWS, CHUNK), (0, 2, 1, 3))
        return jnp.concatenate([route_p, regrouped.reshape(-1, ROUTE_ROWS, MOE_TILE)], axis=0)

    n_steps = n_tok // MOE_TILE
    cnt_td = jnp.concatenate(
        [tcnt_p[:, 0, :], tcnt_s[:, 0, :].reshape(-1, MOE_TILE // CHUNK, N_EXPERTS).sum(axis=1)],
        axis=0).astype(I32)
    c8 = (cnt_td + RUN_ROWS - 1) // RUN_ROWS * RUN_ROWS
    rows8 = jnp.sum(c8, axis=0)
    padded = (rows8 + EXPERT_ROWS - 1) // EXPERT_ROWS * EXPERT_ROWS
    pends = jnp.cumsum(padded)
    pstarts = pends - padded
    base = jnp.cumsum(cnt_td, axis=0) - cnt_td
    grow = pstarts[None, :] + jnp.cumsum(c8, axis=0) - c8
    lo8 = jnp.cumsum(c8, axis=1) - c8
    nch = c8 // RUN_ROWS
    nct = jnp.sum(nch, axis=1)
    off_tiles = (lo8 - base).astype(F32).reshape(n_steps, N_EXPERTS, 1)
    def flat_chunks(counts, max_n, local0, global0, rows):
        cum = jnp.cumsum(counts, axis=1)
        j = jnp.arange(max_n, dtype=I32)
        owner = jnp.sum(j[None, :, None] >= cum[:, None, :], axis=-1)
        owns = owner[:, :, None] == jnp.arange(N_EXPERTS, dtype=I32)[None, None, :]
        of_owner = lambda tab: jnp.sum(jnp.where(owns, tab[:, None, :], 0), axis=-1)
        within = rows * (j[None, :] - of_owner(cum - counts))
        return ((of_owner(local0) + within).reshape(-1).astype(I32),
                (of_owner(global0) + within).reshape(-1).astype(I32), cum[:, -1].astype(I32))

    n_double = nch // 2
    doubled = 2 * RUN_ROWS * n_double
    tables = (flat_chunks(n_double, MAX_DOUBLE_CHUNKS, lo8, grow, 2 * RUN_ROWS)
              + flat_chunks(nch - 2 * n_double, N_EXPERTS, lo8 + doubled, grow + doubled, RUN_ROWS)
              + ((nct * RUN_ROWS).astype(I32),))
    n_blocks = -(-(n_tok * TOP_K + n_steps * N_EXPERTS * (RUN_ROWS - 1) + N_EXPERTS * (EXPERT_ROWS - 1))
                 // EXPERT_ROWS)
    n_used = (pends[-1] // EXPERT_ROWS).astype(I32).reshape(1)
    zero_tables = ((pstarts + rows8).astype(I32), ((padded - rows8) // RUN_ROWS).astype(I32), n_used)

    h2_pf, h2_sf = h2_p.reshape(n_prompt, D_MODEL), h2_s.reshape(n_sample, D_MODEL)
    xs, slot_tiles = _dispatch_call(
        tables, zero_tables, h2_pf, h2_sf, moe_tiles(eidx_p, eidx_s), moe_tiles(rank_p, rank_s), off_tiles,
        n_blocks * EXPERT_ROWS)
    ys = _experts_call((pstarts // EXPERT_ROWS).astype(I32), (padded // EXPERT_ROWS).astype(I32),
                       xs, w_gate[0], w_up[0], w_down[0])
    gate_groups = jnp.concatenate(
        [jnp.repeat(mod[:bsz, 5], seq // GATE_GROUP, axis=0),
         jnp.repeat(mod[bsz:, 5], dseq // GATE_GROUP, axis=0)], axis=0)
    y_p, y_s = _combine_call(
        tables, x1_p.reshape(n_prompt, D_MODEL), x1_s.reshape(n_sample, D_MODEL), h2_pf, h2_sf,
        gate_groups, gains, slot_tiles, moe_tiles(wts_p, wts_s), ys,
        ws_gate[0].astype(BF16), ws_up[0].astype(BF16), ws_down[0].astype(BF16))

    n_hist = state_pool.shape[2]
    kv_shape = (1, -1, WINDOW, N_KV_HEADS, HEAD_DIM)
    return (y_p.reshape(bsz, seq, D_MODEL), y_s.reshape(dbsz, dseq, D_MODEL),
            pool_p[None, :, POOL_HIST_PAD - n_hist:], nk_p.reshape(kv_shape), nv_p.reshape(kv_shape),
            pool_s[None, :, POOL_HIST_PAD - n_hist:], nk_s.reshape(kv_shape), nv_s.reshape(kv_shape))
```
